```python
import math
import jax, jax.numpy as jnp
from jax import lax
import numpy as np

D_MODEL = 2048
BATCH = 4
SEQ = 2048
DEPTH = 2
DEC_BATCH = 128
DEC_SEQ = 4
PAST_LEN = 16384
PAGE_SIZE = 128

N_META = 16
CHUNK = 64
N_EVEN = (DEPTH + 1) // 2
N_ODD = DEPTH // 2

H_A = 8
DK_A = 128
DV_A = 128
H_B = 4
DK_B = 128
DV_B = 256
EVEN_SIZES = (H_A * DK_A, H_A * DK_A, H_A * DV_A, H_A * DV_A, H_B * DK_B, H_B * DK_B, H_B * DV_B, H_B * DV_B)
EVEN_IN = sum(EVEN_SIZES)
EVEN_MIX = H_A * DV_A + H_B * DV_B

H_C = 16
P_C = 64
N_C = 128
G_C = 2
CONV_W = 4
DI_C = H_C * P_C
CONV_DIM = DI_C + 2 * G_C * N_C
H_D = 16
P_D = 64
DI_D = H_D * P_D
R_W = 64
R_A = 64
R_G = 160
RWKV_SIZES = (DI_D, DI_D, DI_D, R_W, R_A, R_G)
SHIFT_DIM = sum(RWKV_SIZES)
ODD_SIZES = (DI_C, CONV_DIM, H_C, SHIFT_DIM)
ODD_IN = sum(ODD_SIZES)
ODD_MIX = DI_C + DI_D

PEER_KEYS = 128
PEER_EXPERTS = PEER_KEYS * PEER_KEYS
PEER_HEADS = 8
PEER_TOPK = 16
PEER_QDIM = 256
PEER_BLOCK = 128

ALPHA = (2.0 * DEPTH) ** 0.25
BETA = (8.0 * DEPTH) ** -0.25
LN_EPS = 1e-5
RMS_EPS = 1e-6
RWKV_GN_EPS = 64e-5
ROPE_BASE = 10000.0
F32 = jnp.float32

kernel_name = 'hybrid_hgrn2_retnet_mamba2_rwkv7_peer_step'


def split_cols(a, sizes):
    out, start = [], 0
    for s in sizes:
        out.append(a[..., start:start + s])
        start += s
    return out


def layer_norm(x, g=None, b=None, eps=LN_EPS):
    xf = x.astype(F32)
    mu = jnp.mean(xf, -1, keepdims=True)
    var = jnp.mean(jnp.square(xf - mu), -1, keepdims=True)
    y = (xf - mu) * lax.rsqrt(var + eps)
    if g is not None:
        y = y * g + b
    return y.astype(x.dtype)


def rms_norm(x, g, eps=RMS_EPS):
    xf = x.astype(F32)
    return (xf * lax.rsqrt(jnp.mean(xf * xf, -1, keepdims=True) + eps) * g).astype(x.dtype)


def to_heads(a, h):
    bn, t, _ = a.shape
    return a.reshape(bn, t, h, -1).transpose(0, 2, 1, 3)


def rotary(x, pos):
    half = x.shape[-1] // 2
    inv = ROPE_BASE ** (-jnp.arange(half, dtype=F32) / half)
    ang = pos.astype(F32)[:, None] * inv
    cos, sin = jnp.cos(ang)[:, None, :], jnp.sin(ang)[:, None, :]
    x1, x2 = x[..., :half], x[..., half:]
    return jnp.concatenate([x1 * cos - x2 * sin, x1 * sin + x2 * cos], -1).astype(x.dtype)


def scalar_decay_chunk(q, k, v, logf, s0):
    L = q.shape[2]
    causal = jnp.tril(jnp.ones((L, L), dtype=bool))
    b = jnp.cumsum(logf.astype(F32), axis=-1)
    seg = jnp.exp(jnp.where(causal, b[..., :, None] - b[..., None, :], -jnp.inf))
    scores = jnp.einsum('bhtk,bhsk->bhts', q, k) * seg
    o = jnp.einsum('bhts,bhsv->bhtv', scores, v) + jnp.einsum('bhtk,bhkv->bhtv', q * jnp.exp(b)[..., None], s0)
    b_end = b[..., -1:]
    s = jnp.exp(b_end)[..., None] * s0 + jnp.einsum('bhsk,bhsv->bhkv', k * jnp.exp(b_end - b)[..., None], v)
    return o, s


def vector_decay_chunk(q, k, v, logf, s0):
    L = q.shape[2]
    causal = jnp.tril(jnp.ones((L, L), dtype=bool))
    b = jnp.cumsum(logf.astype(F32), axis=2)
    diff = b[:, :, :, None, :] - b[:, :, None, :, :]
    seg = jnp.exp(jnp.where(causal[:, :, None], diff, -jnp.inf))
    scores = jnp.einsum('bhtk,bhtsk,bhsk->bhts', q, seg, k)
    o = jnp.einsum('bhts,bhsv->bhtv', scores, v) + jnp.einsum('bhtk,bhkv->bhtv', q * jnp.exp(b), s0)
    b_end = b[:, :, -1:, :]
    s = jnp.exp(b_end[:, :, 0, :])[..., None] * s0 + jnp.einsum('bhsk,bhsv->bhkv', k * jnp.exp(b_end - b), v)
    return o, s


def run_chunks(chunk_fn, q, k, v, logf, s0, lead):
    T = q.shape[2]
    s = s0.astype(F32)
    parts = []
    if lead > 0:
        o_head, s = chunk_fn(q[:, :, :lead], k[:, :, :lead], v[:, :, :lead], logf[:, :, :lead], s)
        parts.append(o_head)
    rest = T - lead
    if rest > 0:
        n = rest // CHUNK

        def split(a):
            a = a[:, :, lead:]
            a = a.reshape(a.shape[:2] + (n, CHUNK) + a.shape[3:])
            return jnp.moveaxis(a, 2, 0)

        def step(carry, xs):
            o_c, carry = chunk_fn(xs[0], xs[1], xs[2], xs[3], carry)
            return carry, o_c

        s, o_rest = lax.scan(step, s, (split(q), split(k), split(v), split(logf)))
        o_rest = jnp.moveaxis(o_rest, 0, 2)
        parts.append(o_rest.reshape(o_rest.shape[:2] + (rest,) + o_rest.shape[4:]))
    return jnp.concatenate(parts, axis=2), s


def rwkv7_scan(r, decay, k, v, kk, a, s0):
    def step(s, inp):
        r_t, w_t, k_t, v_t, kk_t, a_t = inp
        sa = jnp.einsum('bhij,bhj->bhi', s, -kk_t)
        s = s * w_t[:, :, None, :] + sa[..., None] * (kk_t * a_t)[:, :, None, :] + v_t[..., None] * k_t[:, :, None, :]
        return s, jnp.einsum('bhij,bhj->bhi', s, r_t)

    xs = tuple(jnp.moveaxis(t.astype(F32), 1, 0) for t in (r, decay, k, v, kk, a))
    s, ys = lax.scan(step, s0.astype(F32), xs)
    return jnp.moveaxis(ys, 0, 1), s


def even_mixer(x, pos, lead, s_hgrn, s_ret, lb, w_in, norm_g, w_out):
    bn, T, _ = x.shape
    aq, af, ai, ag, bq, bk, bv, bg = split_cols(x @ w_in, EVEN_SIZES)
    af32 = af.astype(F32)
    log_f = jnp.log(lb + (1.0 - lb) * jax.nn.sigmoid(af32))
    k_a = (1.0 - lb) * jax.nn.sigmoid(-af32)
    o_a, s_hgrn_new = run_chunks(vector_decay_chunk, to_heads(jax.nn.silu(aq), H_A), to_heads(k_a, H_A),
                                 to_heads(ai, H_A), to_heads(log_f, H_A), s_hgrn, lead)
    o_a = rms_norm(o_a.transpose(0, 2, 1, 3), norm_g).reshape(bn, T, -1) * jax.nn.silu(ag)
    qb = rotary(bq.reshape(bn, T, H_B, DK_B), pos)
    kb = rotary(bk.reshape(bn, T, H_B, DK_B), pos) * DK_B ** -0.5
    log_gamma = jnp.log(1.0 - 2.0 ** (-5.0 - jnp.arange(H_B, dtype=F32)))
    log_f_b = jnp.broadcast_to(log_gamma[None, :, None], (bn, H_B, T))
    o_b, s_ret_new = run_chunks(scalar_decay_chunk, qb.transpose(0, 2, 1, 3), kb.transpose(0, 2, 1, 3),
                                to_heads(bv, H_B), log_f_b, s_ret, lead)
    o_b = layer_norm(o_b.transpose(0, 2, 1, 3)).reshape(bn, T, -1) * jax.nn.silu(bg)
    y = jnp.concatenate([o_a, o_b], -1) @ w_out
    return y, s_hgrn_new.astype(s_hgrn.dtype), s_ret_new.astype(s_ret.dtype)


def odd_mixer(x, lead, s_ssm, s_conv, s_wkv, s_shift, w, i):
    bn, T, _ = x.shape
    z, xbc, dt_raw, rw = split_cols(x @ w['odd_w_in'][i], ODD_SIZES)
    xpad = jnp.concatenate([s_conv.astype(xbc.dtype), xbc], axis=1)
    new_conv = xpad[:, -(CONV_W - 1):]
    cw = w['conv_w'][i]
    conv = w['conv_b'][i] + xpad[:, 0:T] * cw[0]
    for j in range(1, CONV_W):
        conv = conv + xpad[:, j:j + T] * cw[j]
    xc, bm, cm = split_cols(jax.nn.silu(conv), (DI_C, G_C * N_C, G_C * N_C))
    dt = jax.nn.softplus(dt_raw.astype(F32) + w['dt_bias'][i])
    log_f = jnp.transpose(dt * -jnp.exp(w['a_log'][i].astype(F32)), (0, 2, 1))
    xh = xc.reshape(bn, T, H_C, P_C)
    rep = H_C // G_C
    bh = jnp.repeat(bm.reshape(bn, T, G_C, N_C), rep, axis=2)
    ch = jnp.repeat(cm.reshape(bn, T, G_C, N_C), rep, axis=2)
    o_c, s_ssm_new = run_chunks(scalar_decay_chunk, ch.transpose(0, 2, 1, 3), bh.transpose(0, 2, 1, 3),
                                (xh * dt[..., None]).transpose(0, 2, 1, 3), log_f, s_ssm, lead)
    y_c = o_c.transpose(0, 2, 1, 3) + xh * w['d_skip'][i][:, None]
    y_c = (y_c.reshape(bn, T, DI_C) * jax.nn.silu(z)).reshape(bn, T, G_C, DI_C // G_C)
    y_c = rms_norm(y_c, w['ssm_norm_g'][i].reshape(G_C, DI_C // G_C)).reshape(bn, T, DI_C)
    prev = jnp.concatenate([s_shift[:, None].astype(rw.dtype), rw[:, :-1]], axis=1)
    new_shift = rw[:, -1]
    rw = rw + (prev - rw) * w['shift_mu'][i]
    r, k, v, dw, da, dg = split_cols(rw, RWKV_SIZES)
    w_log = -jax.nn.softplus(-(w['rwkv_w0'][i] + jnp.tanh(dw) @ w['rwkv_w2'][i]).astype(F32)) - 0.5
    decay = jnp.exp(-jnp.exp(w_log))
    a = jax.nn.sigmoid(w['rwkv_a0'][i] + da @ w['rwkv_a2'][i])
    g = jax.nn.sigmoid(dg) @ w['rwkv_g2'][i]

    def hd(t):
        return t.reshape(bn, T, H_D, P_D)

    kk = hd(k * w['rwkv_k_k'][i]).astype(F32)
    kk = kk / jnp.maximum(jnp.linalg.norm(kk, axis=-1, keepdims=True), 1e-12)
    k = k * (1.0 + (a - 1.0) * w['rwkv_k_a'][i])
    y_d, s_wkv_new = rwkv7_scan(hd(r), hd(decay), hd(k), hd(v), kk, hd(a), s_wkv)
    y_d = layer_norm(y_d, w['lnx_g'][i].reshape(H_D, P_D), w['lnx_b'][i].reshape(H_D, P_D), RWKV_GN_EPS)
    y_d = y_d + jnp.sum(hd(r) * hd(k) * w['rwkv_r_k'][i], -1, keepdims=True) * hd(v)
    y_d = y_d.reshape(bn, T, DI_D) * g
    y = jnp.concatenate([y_c, y_d], -1) @ w['odd_w_out'][i]
    return (y, s_ssm_new.astype(s_ssm.dtype), new_conv.astype(s_conv.dtype),
            s_wkv_new.astype(s_wkv.dtype), new_shift.astype(s_shift.dtype))


def peer(x, w_query, sub_keys, u_tab, v_tab):
    bn, T, _ = x.shape
    xt = x.reshape(-1, D_MODEL)
    M = xt.shape[0]
    q = (xt @ w_query).reshape(M, PEER_HEADS, 2, PEER_QDIM // 2)
    s = jnp.einsum('mhcd,hcnd->mhcn', q, sub_keys).astype(F32)
    top_s, top_i = lax.top_k(s, PEER_TOPK)
    cand = top_s[:, :, 0, :, None] + top_s[:, :, 1, None, :]
    cand_i = top_i[:, :, 0, :, None] * PEER_KEYS + top_i[:, :, 1, None, :]
    best_s, best_j = lax.top_k(cand.reshape(M, PEER_HEADS, PEER_TOPK * PEER_TOPK), PEER_TOPK)
    idx = jnp.take_along_axis(cand_i.reshape(M, PEER_HEADS, PEER_TOPK * PEER_TOPK), best_j, axis=-1)
    gate = jax.nn.softmax(best_s, axis=-1)
    pad = (-M) % PEER_BLOCK
    nb = (M + pad) // PEER_BLOCK
    xb = jnp.pad(xt, ((0, pad), (0, 0))).reshape(nb, PEER_BLOCK, D_MODEL)
    ib = jnp.pad(idx.reshape(M, -1), ((0, pad), (0, 0))).reshape(nb, PEER_BLOCK, PEER_HEADS * PEER_TOPK)
    gb = jnp.pad(gate.reshape(M, -1), ((0, pad), (0, 0))).reshape(nb, PEER_BLOCK, PEER_HEADS * PEER_TOPK)

    def block(args):
        x_blk, i_blk, g_blk = args
        act = jax.nn.gelu(jnp.einsum('md,mkd->mk', x_blk, u_tab[i_blk]).astype(F32), approximate=False)
        return jnp.einsum('mk,mkd->md', g_blk * act, v_tab[i_blk])

    out = lax.map(block, (xb, ib, gb)).reshape(-1, D_MODEL)[:M]
    return out.reshape(bn, T, D_MODEL)


def run_trunk(x, pos, lead, st_hgrn, st_ret, st_ssm, st_conv, st_wkv, st_shift, w):
    hg, rt, ssm, cv, wkv, sh = [], [], [], [], [], []
    lb_table = jnp.cumsum(jax.nn.softmax(w['hgrn_lb_logits'].astype(F32), axis=0), axis=0)
    for l in range(DEPTH):
        i = l // 2
        if l % 2 == 0:
            mix, s_h, s_r = even_mixer(x, pos, lead, st_hgrn[i], st_ret[i], lb_table[l], w['even_w_in'][i],
                                       w['hgrn_norm_g'][i], w['even_w_out'][i])
            hg.append(s_h)
            rt.append(s_r)
        else:
            mix, s_s, s_c, s_w, s_sh = odd_mixer(x, lead, st_ssm[i], st_conv[i], st_wkv[i], st_shift[i], w, i)
            ssm.append(s_s)
            cv.append(s_c)
            wkv.append(s_w)
            sh.append(s_sh)
        x = layer_norm(ALPHA * x + mix, w['ln_g'][l, 0], w['ln_b'][l, 0])
        ffn = peer(x, w['peer_w_query'][l], w['peer_sub_keys'][l], w['peer_u'][l], w['peer_v'][l])
        x = layer_norm(ALPHA * x + ffn, w['ln_g'][l, 1], w['ln_b'][l, 1])
    return x, jnp.stack(hg), jnp.stack(rt), jnp.stack(ssm), jnp.stack(cv), jnp.stack(wkv), jnp.stack(sh)


def setup_inputs(seed: int = 0) -> dict:
    key = jax.random.key(seed)
    keys = list(jax.random.split(key, 48))

    def nrm(shape, scale):
        return jax.random.normal(keys.pop(), shape, F32) * scale

    def unif(shape, lo, hi):
        return jax.random.uniform(keys.pop(), shape, F32, lo, hi)

    dt0 = jnp.exp(unif((N_ODD, H_C), math.log(1e-3), math.log(1e-1)))
    return {
        'x_prompt': nrm((BATCH, SEQ, D_MODEL), 1.0),
        'x_sample': nrm((DEC_BATCH, DEC_SEQ, D_MODEL), 1.0),
        'state_hgrn': nrm((N_EVEN, DEC_BATCH, H_A, DK_A, DV_A), 0.5),
        'state_ret': nrm((N_EVEN, DEC_BATCH, H_B, DK_B, DV_B), 1.0),
        'state_ssm': nrm((N_ODD, DEC_BATCH, H_C, N_C, P_C), 0.5),
        'state_conv': nrm((N_ODD, DEC_BATCH, CONV_W - 1, CONV_DIM), 1.0),
        'state_wkv': nrm((N_ODD, DEC_BATCH, H_D, P_D, P_D), 0.3),
        'state_shift': nrm((N_ODD, DEC_BATCH, SHIFT_DIM), 1.0),
        'meta_tokens': nrm((N_META, D_MODEL), 1.0),
        'ln_g': 1.0 + nrm((DEPTH, 2, D_MODEL), 0.02),
        'ln_b': nrm((DEPTH, 2, D_MODEL), 0.02),
        'even_w_in': nrm((N_EVEN, D_MODEL, EVEN_IN), D_MODEL ** -0.5),
        'hgrn_lb_logits': nrm((DEPTH + 1, H_A * DK_A), 0.5),
        'hgrn_norm_g': 1.0 + nrm((N_EVEN, DV_A), 0.02),
        'even_w_out': nrm((N_EVEN, EVEN_MIX, D_MODEL), EVEN_MIX ** -0.5 * BETA),
        'odd_w_in': nrm((N_ODD, D_MODEL, ODD_IN), D_MODEL ** -0.5),
        'conv_w': nrm((N_ODD, CONV_W, CONV_DIM), CONV_W ** -0.5),
        'conv_b': nrm((N_ODD, CONV_DIM), 0.02),
        'dt_bias': dt0 + jnp.log(-jnp.expm1(-dt0)),
        'a_log': jnp.log(unif((N_ODD, H_C), 1.0, 16.0)),
        'd_skip': 1.0 + nrm((N_ODD, H_C), 0.1),
        'ssm_norm_g': 1.0 + nrm((N_ODD, DI_C), 0.02),
        'shift_mu': unif((N_ODD, SHIFT_DIM), 0.0, 1.0),
        'rwkv_w0': -1.0 + nrm((N_ODD, DI_D), 0.5),
        'rwkv_w2': nrm((N_ODD, R_W, DI_D), 0.1),
        'rwkv_a0': nrm((N_ODD, DI_D), 0.1),
        'rwkv_a2': nrm((N_ODD, R_A, DI_D), 0.1),
        'rwkv_g2': nrm((N_ODD, R_G, DI_D), R_G ** -0.5),
        'rwkv_k_k': 0.85 + nrm((N_ODD, DI_D), 0.05),
        'rwkv_k_a': 1.0 + nrm((N_ODD, DI_D), 0.05),
        'rwkv_r_k': nrm((N_ODD, H_D, P_D), 0.1),
        'lnx_g': 1.0 + nrm((N_ODD, DI_D), 0.02),
        'lnx_b': nrm((N_ODD, DI_D), 0.02),
        'odd_w_out': nrm((N_ODD, ODD_MIX, D_MODEL), ODD_MIX ** -0.5 * BETA),
        'peer_w_query': nrm((DEPTH, D_MODEL, PEER_HEADS * PEER_QDIM), D_MODEL ** -0.5),
        'peer_sub_keys': nrm((DEPTH, PEER_HEADS, 2, PEER_KEYS, PEER_QDIM // 2), (PEER_QDIM // 2) ** -0.5),
        'peer_u': nrm((DEPTH, PEER_EXPERTS, D_MODEL), D_MODEL ** -0.5),
        'peer_v': nrm((DEPTH, PEER_EXPERTS, D_MODEL), BETA * PEER_HEADS ** -0.5),
    }


def reference(x_prompt, x_sample, state_hgrn, state_ret, state_ssm, state_conv, state_wkv, state_shift,
              meta_tokens, ln_g, ln_b, even_w_in, hgrn_lb_logits, hgrn_norm_g, even_w_out, odd_w_in, conv_w,
              conv_b, dt_bias, a_log, d_skip, ssm_norm_g, shift_mu, rwkv_w0, rwkv_w2, rwkv_a0, rwkv_a2, rwkv_g2,
              rwkv_k_k, rwkv_k_a, rwkv_r_k, lnx_g, lnx_b, odd_w_out, peer_w_query, peer_sub_keys, peer_u, peer_v):
    w = dict(ln_g=ln_g, ln_b=ln_b, even_w_in=even_w_in, hgrn_lb_logits=hgrn_lb_logits, hgrn_norm_g=hgrn_norm_g,
             even_w_out=even_w_out, odd_w_in=odd_w_in, conv_w=conv_w, conv_b=conv_b, dt_bias=dt_bias,
             a_log=a_log, d_skip=d_skip, ssm_norm_g=ssm_norm_g, shift_mu=shift_mu, rwkv_w0=rwkv_w0,
             rwkv_w2=rwkv_w2, rwkv_a0=rwkv_a0, rwkv_a2=rwkv_a2, rwkv_g2=rwkv_g2, rwkv_k_k=rwkv_k_k,
             rwkv_k_a=rwkv_k_a, rwkv_r_k=rwkv_r_k, lnx_g=lnx_g, lnx_b=lnx_b, odd_w_out=odd_w_out,
             peer_w_query=peer_w_query, peer_sub_keys=peer_sub_keys, peer_u=peer_u, peer_v=peer_v)
    bp, sp = x_prompt.shape[0], x_prompt.shape[1]
    dt = x_prompt.dtype
    xp = jnp.concatenate([jnp.broadcast_to(meta_tokens.astype(dt)[None], (bp, N_META, D_MODEL)), x_prompt], axis=1)
    pos_p = jnp.arange(N_META + sp)
    yp, hgrn_p, ret_p, ssm_p, conv_p, wkv_p, shift_p = run_trunk(
        xp, pos_p, N_META,
        jnp.zeros((N_EVEN, bp, H_A, DK_A, DV_A), dt), jnp.zeros((N_EVEN, bp, H_B, DK_B, DV_B), dt),
        jnp.zeros((N_ODD, bp, H_C, N_C, P_C), dt), jnp.zeros((N_ODD, bp, CONV_W - 1, CONV_DIM), dt),
        jnp.zeros((N_ODD, bp, H_D, P_D, P_D), dt), jnp.zeros((N_ODD, bp, SHIFT_DIM), dt), w)
    y_prompt = yp[:, N_META:]
    ds = x_sample.shape[1]
    pos_s = PAST_LEN + jnp.arange(ds)
    y_sample, hgrn_s, ret_s, ssm_s, conv_s, wkv_s, shift_s = run_trunk(
        x_sample, pos_s, ds % CHUNK, state_hgrn, state_ret, state_ssm, state_conv, state_wkv, state_shift, w)
    return (y_prompt, y_sample, hgrn_p, hgrn_s, ret_p, ret_s, ssm_p, ssm_s, conv_p, conv_s, wkv_p, wkv_s, shift_p, shift_s)
```

```python
import functools
import math

import jax
import jax.numpy as jnp
from jax import lax
from jax.experimental import pallas as pl
from jax.experimental.pallas import tpu as pltpu

D_MODEL = 2048
BATCH = 4
SEQ = 2048
DEPTH = 2
DEC_BATCH = 128
DEC_SEQ = 4
PAST_LEN = 16384
N_META = 16
CHUNK = 64

H_A, DK_A, DV_A = 8, 128, 128
H_B, DK_B, DV_B = 4, 128, 256
EVEN_IN = 4 * H_A * DK_A + 2 * H_B * DK_B + 2 * H_B * DV_B
EVEN_MIX = H_A * DV_A + H_B * DV_B

H_C, P_C, N_C, G_C, CONV_W = 16, 64, 128, 2, 4
DI_C = H_C * P_C
CONV_DIM = DI_C + 2 * G_C * N_C
H_D, P_D = 16, 64
DI_D = H_D * P_D
R_W, R_A, R_G = 64, 64, 160
SHIFT_DIM = 3 * DI_D + R_W + R_A + R_G
ODD_IN = DI_C + CONV_DIM + H_C + SHIFT_DIM
ODD_MIX = DI_C + DI_D

PEER_KEYS = 128
PEER_EXPERTS = PEER_KEYS * PEER_KEYS
PEER_HEADS = 8
PEER_TOPK = 16
PEER_QDIM = 256

ALPHA = (2.0 * DEPTH) ** 0.25
LN_EPS = 1e-5
RMS_EPS = 1e-6
RWKV_GN_EPS = 64e-5
ROPE_BASE = 10000.0

F32 = jnp.float32
BF16 = jnp.bfloat16
HIGHEST = lax.Precision.HIGHEST

P_LEAD = CHUNK - N_META
P_ROWS = CHUNK + SEQ
P_CHUNKS = P_ROWS // CHUNK
S_ROW0 = BATCH * P_ROWS
S_LEN = 8
TOK_TILE = 512
M_PAD = -(-(S_ROW0 + DEC_BATCH * DEC_SEQ) // 1024) * 1024

VMEM_LIMIT = 56 * 1024 * 1024


def _dot(a, b, dims=(((1,), (0,)), ((), ())), precision=None):
    return lax.dot_general(a, b, dims, precision=precision, preferred_element_type=F32)


def _dot_bf(a, b, dims=(((1,), (0,)), ((), ()))):
    return lax.dot_general(a.astype(BF16), b.astype(BF16), dims, preferred_element_type=F32)


NT = (((1,), (1,)), ((), ()))
TN = (((0,), (0,)), ((), ()))


def _sigmoid(x):
    return 1.0 / (1.0 + jnp.exp(-x))


def _silu(x):
    return x * _sigmoid(x)


def _mm_kernel(x_ref, w_ref, o_ref):
    o_ref[...] = jnp.dot(x_ref[...].astype(BF16), w_ref[...], preferred_element_type=F32).astype(o_ref.dtype)


def matmul(x, w, *, tm, tn, out_dtype=F32, name="matmul"):
    m, k = x.shape
    n = w.shape[1]
    return pl.pallas_call(
        _mm_kernel,
        out_shape=jax.ShapeDtypeStruct((m, n), out_dtype),
        grid=(n // tn, m // tm),
        in_specs=[pl.BlockSpec((tm, k), lambda j, i: (i, 0)),
                  pl.BlockSpec((k, tn), lambda j, i: (0, j))],
        out_specs=pl.BlockSpec((tm, tn), lambda j, i: (i, j)),
        compiler_params=pltpu.CompilerParams(dimension_semantics=("arbitrary", "arbitrary"),
                                             vmem_limit_bytes=VMEM_LIMIT),
        name=name,
    )(x, w)


def _ln_rows(v, g, b):
    mu = jnp.mean(v, -1, keepdims=True)
    c = v - mu
    var = jnp.mean(c * c, -1, keepdims=True)
    return c * lax.rsqrt(var + LN_EPS) * g + b


def _proj_ln_kernel(x_ref, m_ref, w_ref, g_ref, b_ref, o_ref, ob_ref):
    acc = jnp.dot(m_ref[...].astype(BF16), w_ref[...], preferred_element_type=F32)
    y = _ln_rows(ALPHA * x_ref[...] + acc, g_ref[...], b_ref[...])
    o_ref[...] = y
    ob_ref[...] = y.astype(BF16)


def proj_ln(x, mix, w, g, b, *, tm, name="proj_ln"):
    m, d = x.shape
    k = mix.shape[1]
    return pl.pallas_call(
        _proj_ln_kernel,
        out_shape=(jax.ShapeDtypeStruct((m, d), F32), jax.ShapeDtypeStruct((m, d), BF16)),
        grid=(m // tm,),
        in_specs=[pl.BlockSpec((tm, d), lambda i: (i, 0)),
                  pl.BlockSpec((tm, k), lambda i: (i, 0)),
                  pl.BlockSpec((k, d), lambda i: (0, 0)),
                  pl.BlockSpec((1, d), lambda i: (0, 0)),
                  pl.BlockSpec((1, d), lambda i: (0, 0))],
        out_specs=(pl.BlockSpec((tm, d), lambda i: (i, 0)), pl.BlockSpec((tm, d), lambda i: (i, 0))),
        compiler_params=pltpu.CompilerParams(dimension_semantics=("arbitrary",),
                                             vmem_limit_bytes=VMEM_LIMIT),
        name=name,
    )(x, mix, w, g.reshape(1, d), b.reshape(1, d))


def _row_info(L, lead, nvalid):
    r = lax.broadcasted_iota(jnp.int32, (L, 1), 0)
    return r, (r >= lead) & (r < lead + nvalid)


def _tril(L, strict=False):
    r = lax.broadcasted_iota(jnp.int32, (L, L), 0)
    c = lax.broadcasted_iota(jnp.int32, (L, L), 1)
    return (c < r) if strict else (c <= r)


def _hgrn_head(q, k, v, logf, s0, L):
    tril = _tril(L).astype(F32)
    b = _dot(tril, logf, precision=HIGHEST)
    ones = jnp.ones((L, DK_A), F32)
    b_end_col = _dot(logf, ones, TN, precision=HIGHEST)
    b_end_row = b[L - 1:L, :]
    o = _dot_bf(q * jnp.exp(b), s0)
    s_new = jnp.exp(b_end_col) * s0 + _dot_bf(k * jnp.exp(b_end_row - b), v, TN)
    sub = min(16, L)
    outs = []
    for i in range(L // sub):
        r0 = i * sub
        qs, bs = q[r0:r0 + sub], b[r0:r0 + sub]
        acc = o[r0:r0 + sub]
        if i > 0:
            ref = b[r0 - 1:r0, :]
            a = _dot_bf(qs * jnp.exp(bs - ref), k[0:r0] * jnp.exp(ref - b[0:r0]), NT)
            acc = acc + _dot_bf(a, v[0:r0])
        t_idx = lax.broadcasted_iota(jnp.int32, (sub, 1), 0)
        for s in range(sub):
            rs = r0 + s
            d = jnp.where(t_idx >= s, bs - b[rs:rs + 1, :], -jnp.inf)
            a_col = jnp.sum(qs * k[rs:rs + 1, :] * jnp.exp(d), axis=-1, keepdims=True)
            acc = acc + a_col * v[rs:rs + 1, :]
        outs.append(acc)
    return jnp.concatenate(outs, axis=0) if len(outs) > 1 else outs[0], s_new


def _ret_head(q, k, v, cnt_col, cnt_row, log_gamma, s0, L):
    causal = _tril(L)
    seg = jnp.where(causal, jnp.exp((cnt_col - cnt_row) * log_gamma), 0.0)
    scores = _dot_bf(q, k, NT) * seg
    o = _dot_bf(scores, v) + _dot_bf(q * jnp.exp(cnt_col * log_gamma), s0)
    cnt_end = cnt_col[L - 1:L, :]
    s_new = jnp.exp(cnt_end * log_gamma) * s0 + _dot_bf(k * jnp.exp((cnt_end - cnt_col) * log_gamma), v, TN)
    return o, s_new


def _even_kernel(*refs, L, nvalid, chunks, has_s0):
    if has_s0:
        z_ref, lb_ref, ng_ref, cos_ref, sin_ref, sh0_ref, sr0_ref, o_ref, sho_ref, sro_ref, sh_s, sr_s = refs
    else:
        z_ref, lb_ref, ng_ref, cos_ref, sin_ref, o_ref, sho_ref, sro_ref, sh_s, sr_s = refs
    c = pl.program_id(1)

    @pl.when(c == 0)
    def _():
        if has_s0:
            sh_s[...] = sh0_ref[0]
            sr_s[...] = sr0_ref[0]
        else:
            sh_s[...] = jnp.zeros_like(sh_s)
            sr_s[...] = jnp.zeros_like(sr_s)

    lead = jnp.where(c == 0, P_LEAD, 0) if chunks > 1 else 0
    r, valid = _row_info(L, lead, nvalid)
    cnt_col = jnp.clip(r + 1 - lead, 0, nvalid).astype(F32)
    rr = lax.broadcasted_iota(jnp.int32, (1, L), 1)
    cnt_row = jnp.clip(rr + 1 - lead, 0, nvalid).astype(F32)

    oa, ob = H_A * DK_A, 0
    off_q, off_f, off_i, off_g = 0, H_A * DK_A, 2 * H_A * DK_A, 2 * H_A * DK_A + H_A * DV_A
    base_b = 2 * H_A * DK_A + 2 * H_A * DV_A
    off_bq, off_bk, off_bv = base_b, base_b + H_B * DK_B, base_b + 2 * H_B * DK_B
    off_bg = off_bv + H_B * DV_B
    ng = ng_ref[...]
    for h in range(H_A):
        sl = slice(h * DK_A, (h + 1) * DK_A)
        aq = z_ref[:, off_q + h * DK_A: off_q + (h + 1) * DK_A]
        af = z_ref[:, off_f + h * DK_A: off_f + (h + 1) * DK_A]
        ai = z_ref[:, off_i + h * DV_A: off_i + (h + 1) * DV_A]
        ag = z_ref[:, off_g + h * DV_A: off_g + (h + 1) * DV_A]
        lb = lb_ref[:, sl]
        logf = jnp.where(valid, jnp.log(lb + (1.0 - lb) * _sigmoid(af)), 0.0)
        k = jnp.where(valid, (1.0 - lb) * _sigmoid(-af), 0.0)
        o, s_new = _hgrn_head(_silu(aq), k, ai, logf, sh_s[h], L)
        sh_s[h] = s_new
        o = o * lax.rsqrt(jnp.mean(o * o, -1, keepdims=True) + RMS_EPS) * ng
        o_ref[:, h * DV_A:(h + 1) * DV_A] = o * _silu(ag)
    cosv, sinv = cos_ref[...], sin_ref[...]
    for h in range(H_B):
        bq = z_ref[:, off_bq + h * DK_B: off_bq + (h + 1) * DK_B]
        bk = z_ref[:, off_bk + h * DK_B: off_bk + (h + 1) * DK_B]
        bv = z_ref[:, off_bv + h * DV_B: off_bv + (h + 1) * DV_B]
        bg = z_ref[:, off_bg + h * DV_B: off_bg + (h + 1) * DV_B]
        q = bq * cosv + pltpu.roll(bq, DK_B // 2, 1) * sinv
        k = (bk * cosv + pltpu.roll(bk, DK_B // 2, 1) * sinv) * (DK_B ** -0.5)
        k = jnp.where(valid, k, 0.0)
        log_gamma = math.log(1.0 - 2.0 ** (-5.0 - h))
        o, s_new = _ret_head(q, k, bv, cnt_col, cnt_row, log_gamma, sr_s[h], L)
        sr_s[h] = s_new
        mu = jnp.mean(o, -1, keepdims=True)
        cc = o - mu
        o = cc * lax.rsqrt(jnp.mean(cc * cc, -1, keepdims=True) + LN_EPS)
        o_ref[:, H_A * DV_A + h * DV_B: H_A * DV_A + (h + 1) * DV_B] = o * _silu(bg)

    @pl.when(c == chunks - 1)
    def _():
        sho_ref[0] = sh_s[...]
        sro_ref[0] = sr_s[...]


def even_mixer_prompt(z, lb, ng, cos, sin, *, nb=BATCH, chunks=P_CHUNKS):
    kern = functools.partial(_even_kernel, L=CHUNK, nvalid=CHUNK, chunks=chunks, has_s0=False)
    return pl.pallas_call(
        kern,
        out_shape=(jax.ShapeDtypeStruct((z.shape[0], EVEN_MIX), F32),
                   jax.ShapeDtypeStruct((nb, H_A, DK_A, DV_A), F32),
                   jax.ShapeDtypeStruct((nb, H_B, DK_B, DV_B), F32)),
        grid=(nb, chunks),
        in_specs=[pl.BlockSpec((CHUNK, EVEN_IN), lambda b, c: (b * chunks + c, 0)),
                  pl.BlockSpec((1, H_A * DK_A), lambda b, c: (0, 0)),
                  pl.BlockSpec((1, DV_A), lambda b, c: (0, 0)),
                  pl.BlockSpec((CHUNK, DK_B), lambda b, c: (c, 0)),
                  pl.BlockSpec((CHUNK, DK_B), lambda b, c: (c, 0))],
        out_specs=(pl.BlockSpec((CHUNK, EVEN_MIX), lambda b, c: (b * chunks + c, 0)),
                   pl.BlockSpec((1, H_A, DK_A, DV_A), lambda b, c: (b, 0, 0, 0)),
                   pl.BlockSpec((1, H_B, DK_B, DV_B), lambda b, c: (b, 0, 0, 0))),
        scratch_shapes=[pltpu.VMEM((H_A, DK_A, DV_A), F32), pltpu.VMEM((H_B, DK_B, DV_B), F32)],
        compiler_params=pltpu.CompilerParams(dimension_semantics=("arbitrary", "arbitrary"),
                                             vmem_limit_bytes=VMEM_LIMIT),
        name="even_mixer_prompt",
    )(z, lb, ng, cos, sin)


def even_mixer_sample(z, lb, ng, cos, sin, s_hgrn, s_ret):
    nb = z.shape[0]
    kern = functools.partial(_even_kernel, L=S_LEN, nvalid=DEC_SEQ, chunks=1, has_s0=True)
    return pl.pallas_call(
        kern,
        out_shape=(jax.ShapeDtypeStruct((nb, S_LEN, EVEN_MIX), F32),
                   jax.ShapeDtypeStruct((nb, H_A, DK_A, DV_A), F32),
                   jax.ShapeDtypeStruct((nb, H_B, DK_B, DV_B), F32)),
        grid=(nb, 1),
        in_specs=[pl.BlockSpec((None, S_LEN, EVEN_IN), lambda b, c: (b, 0, 0)),
                  pl.BlockSpec((1, H_A * DK_A), lambda b, c: (0, 0)),
                  pl.BlockSpec((1, DV_A), lambda b, c: (0, 0)),
                  pl.BlockSpec((S_LEN, DK_B), lambda b, c: (0, 0)),
                  pl.BlockSpec((S_LEN, DK_B), lambda b, c: (0, 0)),
                  pl.BlockSpec((1, H_A, DK_A, DV_A), lambda b, c: (b, 0, 0, 0)),
                  pl.BlockSpec((1, H_B, DK_B, DV_B), lambda b, c: (b, 0, 0, 0))],
        out_specs=(pl.BlockSpec((None, S_LEN, EVEN_MIX), lambda b, c: (b, 0, 0)),
                   pl.BlockSpec((1, H_A, DK_A, DV_A), lambda b, c: (b, 0, 0, 0)),
                   pl.BlockSpec((1, H_B, DK_B, DV_B), lambda b, c: (b, 0, 0, 0))),
        scratch_shapes=[pltpu.VMEM((H_A, DK_A, DV_A), F32), pltpu.VMEM((H_B, DK_B, DV_B), F32)],
        compiler_params=pltpu.CompilerParams(dimension_semantics=("arbitrary", "arbitrary"),
                                             vmem_limit_bytes=VMEM_LIMIT),
        name="even_mixer_sample",
    )(z, lb, ng, cos, sin, s_hgrn, s_ret)


def _rope_tables(pos):
    half = DK_B // 2
    inv = ROPE_BASE ** (-jnp.arange(half, dtype=F32) / half)
    ang = pos.astype(F32)[:, None] * inv
    cos, sin = jnp.cos(ang), jnp.sin(ang)
    return jnp.concatenate([cos, cos], -1), jnp.concatenate([-sin, sin], -1)


ODD_PAD = 6144
O_Z, O_X, O_R, O_K, O_V, O_T = 0, DI_C, DI_C + CONV_DIM, DI_C + CONV_DIM + DI_D, DI_C + CONV_DIM + 2 * DI_D, \
    DI_C + CONV_DIM + 3 * DI_D
T_W = 384
T_DT = R_W + R_A + R_G
SHIFT_W = 3 * DI_D + T_W
HALF = 64
RW_PREC = HIGHEST


def _softplus(x):
    return jnp.maximum(x, 0.0) + jnp.log1p(jnp.exp(-jnp.abs(x)))


def _lane_lt(n, width):
    return lax.broadcasted_iota(jnp.int32, (1, width), 1) < n


def _pair_ones():
    r = lax.broadcasted_iota(jnp.int32, (2 * HALF, 2 * HALF), 0) < HALF
    c = lax.broadcasted_iota(jnp.int32, (2 * HALF, 2 * HALF), 1) < HALF
    return r == c


def _stack_pair(x, m_a):
    return jnp.concatenate([jnp.where(m_a, x, 0.0), jnp.where(m_a, 0.0, x)], axis=0)


def _rwkv_pair(r, k, v, al, be, logw, bd, L):
    m_a = _lane_lt(HALF, 2 * HALF)
    tril = _tril(L).astype(F32)
    G = _dot(tril, logw, precision=HIGHEST)
    e_inv = jnp.exp(-G)
    al_t = al * jnp.exp(G - logw)
    r_t = r * jnp.exp(G)
    ks = _stack_pair(k * e_inv, m_a)
    bs = _stack_pair(be * e_inv, m_a)
    vs = _stack_pair(v, m_a)
    rr = lax.broadcasted_iota(jnp.int32, (L, 2 * L), 0)
    cc = lax.broadcasted_iota(jnp.int32, (L, 2 * L), 1)
    cc = jnp.where(cc >= L, cc - L, cc)
    strict, incl = cc < rr, cc <= rr
    a_ka = jnp.where(strict, _dot(al_t, ks, NT, RW_PREC), 0.0)
    a_ba = jnp.where(strict, _dot(al_t, bs, NT, RW_PREC), 0.0)
    a_kr = jnp.where(incl, _dot(r_t, ks, NT, RW_PREC), 0.0)
    a_br = jnp.where(incl, _dot(r_t, bs, NT, RW_PREC), 0.0)
    rhs = _dot(al_t, bd, NT, RW_PREC) + _dot(a_ka, vs, precision=RW_PREC)
    m_l = _lane_lt(L, 2 * L)
    n_bd = jnp.concatenate([jnp.where(m_l, a_ba, 0.0), jnp.where(m_l, 0.0, a_ba)], axis=0)
    eye = (lax.broadcasted_iota(jnp.int32, (2 * L, 2 * L), 0) ==
           lax.broadcasted_iota(jnp.int32, (2 * L, 2 * L), 1)).astype(F32)
    t_inv = eye - n_bd
    p = n_bd
    for _ in range(int(math.log2(L)) - 1):
        p = _dot(p, p, precision=RW_PREC)
        t_inv = t_inv + _dot(t_inv, p, precision=RW_PREC)
    t_st = t_inv[0:L, :] + t_inv[L:2 * L, :]
    u = _dot(t_st, _stack_pair(rhs, m_a), precision=RW_PREC)
    y = _dot(r_t, bd, NT, RW_PREC) + _dot(a_kr, vs, precision=RW_PREC) - _dot(a_br, _stack_pair(u, m_a), precision=RW_PREC)
    gl = G[L - 1:L, :]
    dec = jnp.exp(gl - G)
    upd = _dot(v, k * dec, TN, RW_PREC) - _dot(u, be * dec, TN, RW_PREC)
    bd_new = bd * jnp.exp(gl) + jnp.where(_pair_ones(), upd, 0.0)
    return y, bd_new


def _ssd_pair(xdt, cg, bg, cb, bcs, brows, sp, L):
    m_a = _lane_lt(HALF, 2 * HALF)
    causal = _tril(L)
    out = jnp.zeros((L, 2 * HALF), F32)
    ends = [bc[L - 1:L, :] for bc in bcs]
    sp_new = sp * jnp.where(m_a, jnp.exp(ends[0]), jnp.exp(ends[1]))
    for x in range(2):
        keep = m_a if x == 0 else jnp.logical_not(m_a)
        seg = jnp.where(causal, jnp.exp(bcs[x] - brows[x]), 0.0)
        xm = jnp.where(keep, xdt, 0.0)
        out = out + _dot_bf(cb * seg, xm) + _dot_bf(cg * jnp.exp(bcs[x]), jnp.where(keep, sp, 0.0))
        sp_new = sp_new + _dot_bf(bg * jnp.exp(ends[x] - bcs[x]), xm, TN)
    return out, sp_new


def _odd_kernel(*refs, L, nvalid, chunks, has_s0):
    if has_s0:
        (z_ref, vec_ref, cp_ref, tp_ref, w2_ref, a2_ref, g2_ref, ssm0_ref, wkv0_ref, conv0_ref, shift0_ref,
         o_ref, ssmo_ref, wkvo_ref, convo_ref, shifto_ref, ssm_s, wkv_s, conv_c, shift_c) = refs
    else:
        (z_ref, vec_ref, cp_ref, tp_ref, w2_ref, a2_ref, g2_ref,
         o_ref, ssmo_ref, wkvo_ref, convo_ref, shifto_ref, ssm_s, wkv_s, conv_c, shift_c) = refs
    c = pl.program_id(1)
    npair = H_C // 2

    @pl.when(c == 0)
    def _():
        if has_s0:
            zpad = jnp.zeros((HALF, HALF), F32)
            for p in range(npair):
                ssm_s[p] = jnp.concatenate([ssm0_ref[0, 2 * p], ssm0_ref[0, 2 * p + 1]], axis=1)
                wkv_s[p] = jnp.concatenate([jnp.concatenate([wkv0_ref[0, 2 * p], zpad], axis=1),
                                            jnp.concatenate([zpad, wkv0_ref[0, 2 * p + 1]], axis=1)], axis=0)
            conv_c[...] = conv0_ref[0]
            shift_c[...] = shift0_ref[0]
        else:
            ssm_s[...] = jnp.zeros_like(ssm_s)
            wkv_s[...] = jnp.zeros_like(wkv_s)
            conv_c[...] = jnp.zeros_like(conv_c)
            shift_c[...] = jnp.zeros_like(shift_c)

    lead = jnp.where(c == 0, P_LEAD, 0) if chunks > 1 else 0
    r_idx, valid = _row_info(L, lead, nvalid)
    m_a = _lane_lt(HALF, 2 * HALF)
    tril = _tril(L).astype(F32)
    triu = (lax.broadcasted_iota(jnp.int32, (L, L), 0) <= lax.broadcasted_iota(jnp.int32, (L, L), 1)).astype(F32)
    ones_blk = _pair_ones().astype(F32)

    xbc = jnp.where(valid, z_ref[:, O_X:O_X + CONV_DIM], 0.0)
    c8 = conv_c[...]
    r8 = lax.broadcasted_iota(jnp.int32, (8, 1), 0)
    conv = cp_ref[CONV_W:CONV_W + 1, :] + xbc * cp_ref[CONV_W - 1:CONV_W, :]
    for j in range(1, CONV_W):
        rolled = pltpu.roll(xbc, j, 0)
        head = jnp.where(r8 < j, pltpu.roll(c8, j, 0), rolled[0:8])
        sh = jnp.concatenate([head, rolled[8:]], axis=0) if L > 8 else head
        conv = conv + sh * cp_ref[CONV_W - 1 - j:CONV_W - j, :]
    conv_c[...] = xbc[L - 8:L]
    act = _silu(conv)
    xc, bm, cm = act[:, :DI_C], act[:, DI_C:DI_C + G_C * N_C], act[:, DI_C + G_C * N_C:]

    t_raw = z_ref[:, O_T:O_T + T_W]
    dt = jnp.where(valid, _softplus(t_raw + tp_ref[1:2, :])[:, T_DT:T_DT + H_C], 0.0)
    logf = dt * tp_ref[2:3, T_DT:T_DT + H_C]
    bc_all = _dot(tril, logf, precision=HIGHEST)
    brow_all = _dot(logf, triu, TN, precision=HIGHEST)
    per_g = npair // G_C
    for g in range(G_C):
        cg = cm[:, g * N_C:(g + 1) * N_C]
        bg = bm[:, g * N_C:(g + 1) * N_C]
        cb = _dot_bf(cg, bg, NT)
        ys, ss = [], jnp.zeros((L, 1), F32)
        for pp in range(per_g):
            p = g * per_g + pp
            sl = slice(p * 2 * HALF, (p + 1) * 2 * HALF)
            h_a, h_b = 2 * p, 2 * p + 1
            xcp = xc[:, sl]
            xdt = xcp * jnp.where(m_a, dt[:, h_a:h_a + 1], dt[:, h_b:h_b + 1])
            o, sp_new = _ssd_pair(xdt, cg, bg, cb, [bc_all[:, h_a:h_a + 1], bc_all[:, h_b:h_b + 1]],
                                  [brow_all[h_a:h_a + 1, :], brow_all[h_b:h_b + 1, :]], ssm_s[p], L)
            ssm_s[p] = sp_new
            y = (o + xcp * vec_ref[0:1, sl]) * _silu(z_ref[:, O_Z + p * 2 * HALF:O_Z + (p + 1) * 2 * HALF])
            ys.append(y)
            ss = ss + jnp.sum(y * y, axis=-1, keepdims=True)
        scale = lax.rsqrt(ss / (DI_C // G_C) + RMS_EPS)
        for pp in range(per_g):
            p = g * per_g + pp
            sl = slice(p * 2 * HALF, (p + 1) * 2 * HALF)
            o_ref[:, sl] = ys[pp] * scale * vec_ref[1:2, sl]

    def shift_mix(p_raw, carry_row, mu):
        pv = jnp.where(valid, p_raw, 0.0)
        prev = jnp.where(r_idx == 0, carry_row, pltpu.roll(pv, 1, 0))
        return pv, pv + (prev - pv) * mu

    r_raw, r = shift_mix(z_ref[:, O_R:O_R + DI_D], shift_c[7:8, 0:DI_D], vec_ref[2:3, :])
    k_raw, k = shift_mix(z_ref[:, O_K:O_K + DI_D], shift_c[7:8, DI_D:2 * DI_D], vec_ref[3:4, :])
    v_raw, v = shift_mix(z_ref[:, O_V:O_V + DI_D], shift_c[7:8, 2 * DI_D:3 * DI_D], vec_ref[4:5, :])
    t_rawm, tm = shift_mix(t_raw, shift_c[7:8, 3 * DI_D:SHIFT_W], tp_ref[0:1, :])
    shift_c[:, 0:DI_D] = r_raw[L - 8:L]
    shift_c[:, DI_D:2 * DI_D] = k_raw[L - 8:L]
    shift_c[:, 2 * DI_D:3 * DI_D] = v_raw[L - 8:L]
    shift_c[:, 3 * DI_D:SHIFT_W] = t_rawm[L - 8:L]

    w_pre = vec_ref[5:6, :] + _dot(jnp.tanh(tm), w2_ref[...], precision=HIGHEST)
    logw = jnp.where(valid, -jnp.exp(-_softplus(-w_pre) - 0.5), 0.0)
    a = _sigmoid(vec_ref[6:7, :] + _dot_bf(tm, a2_ref[...]))
    gate = _dot_bf(_sigmoid(tm), g2_ref[...])
    kkr = k * vec_ref[7:8, :]
    k2 = k * (1.0 + (a - 1.0) * vec_ref[8:9, :])
    rk = r * k2 * vec_ref[9:10, :]
    for p in range(npair):
        sl = slice(p * 2 * HALF, (p + 1) * 2 * HALF)
        kk = kkr[:, sl]
        nrm = jnp.sqrt(_dot(kk * kk, ones_blk, precision=HIGHEST))
        al = jnp.where(valid, kk / jnp.maximum(nrm, 1e-12), 0.0)
        be = al * a[:, sl]
        y, bd_new = _rwkv_pair(r[:, sl], jnp.where(valid, k2[:, sl], 0.0), v[:, sl], al, be, logw[:, sl], wkv_s[p], L)
        wkv_s[p] = bd_new
        mu = _dot(y, ones_blk, precision=HIGHEST) * (1.0 / P_D)
        yc = y - mu
        var = _dot(yc * yc, ones_blk, precision=HIGHEST) * (1.0 / P_D)
        yn = yc * lax.rsqrt(var + RWKV_GN_EPS) * vec_ref[10:11, sl] + vec_ref[11:12, sl]
        bonus = _dot(rk[:, sl], ones_blk, precision=HIGHEST) * v[:, sl]
        o_ref[:, DI_C + p * 2 * HALF:DI_C + (p + 1) * 2 * HALF] = (yn + bonus) * gate[:, sl]

    @pl.when(c == chunks - 1)
    def _():
        for p in range(npair):
            sp = ssm_s[p]
            ssmo_ref[0, 2 * p] = sp[:, 0:HALF]
            ssmo_ref[0, 2 * p + 1] = sp[:, HALF:2 * HALF]
            bd = wkv_s[p]
            wkvo_ref[0, 2 * p] = bd[0:HALF, 0:HALF]
            wkvo_ref[0, 2 * p + 1] = bd[HALF:2 * HALF, HALF:2 * HALF]
        convo_ref[0] = conv_c[...]
        shifto_ref[0] = shift_c[...]


def _odd_call(z, params, states, *, L, nvalid, nb, chunks, row_map, name):
    has_s0 = states is not None
    kern = functools.partial(_odd_kernel, L=L, nvalid=nvalid, chunks=chunks, has_s0=has_s0)
    npair = H_C // 2
    if z.ndim == 2:
        z_spec = pl.BlockSpec((L, ODD_PAD), lambda b, c: (b * chunks + c, 0))
        o_spec = pl.BlockSpec((L, ODD_MIX), lambda b, c: (b * chunks + c, 0))
        o_shape = jax.ShapeDtypeStruct((z.shape[0], ODD_MIX), F32)
    else:
        z_spec = pl.BlockSpec((None, L, ODD_PAD), lambda b, c: (b, 0, 0))
        o_spec = pl.BlockSpec((None, L, ODD_MIX), lambda b, c: (b, 0, 0))
        o_shape = jax.ShapeDtypeStruct((nb, L, ODD_MIX), F32)
    const2 = lambda b, c: (0, 0)
    per_b = lambda b, c: (b, 0, 0, 0)
    per_b3 = lambda b, c: (b, 0, 0)
    in_specs = [z_spec] + [pl.BlockSpec(p.shape, const2) for p in params]
    args = [z] + list(params)
    st_specs = [pl.BlockSpec((1, H_C, N_C, P_C), per_b), pl.BlockSpec((1, H_D, P_D, P_D), per_b),
                pl.BlockSpec((1, 8, CONV_DIM), per_b3), pl.BlockSpec((1, 8, SHIFT_W), per_b3)]
    if has_s0:
        in_specs += st_specs
        args += list(states)
    return pl.pallas_call(
        kern,
        out_shape=(o_shape,
                   jax.ShapeDtypeStruct((nb, H_C, N_C, P_C), F32), jax.ShapeDtypeStruct((nb, H_D, P_D, P_D), F32),
                   jax.ShapeDtypeStruct((nb, 8, CONV_DIM), F32), jax.ShapeDtypeStruct((nb, 8, SHIFT_W), F32)),
        grid=(nb, chunks),
        in_specs=in_specs,
        out_specs=tuple([o_spec] + st_specs),
        scratch_shapes=[pltpu.VMEM((npair, N_C, 2 * HALF), F32), pltpu.VMEM((npair, 2 * HALF, 2 * HALF), F32),
                        pltpu.VMEM((8, CONV_DIM), F32), pltpu.VMEM((8, SHIFT_W), F32)],
        compiler_params=pltpu.CompilerParams(dimension_semantics=("arbitrary", "arbitrary"),
                                             vmem_limit_bytes=VMEM_LIMIT),
        name=name,
    )(*args)


def odd_mixer_prompt(z, params, *, nb=BATCH, chunks=P_CHUNKS):
    return _odd_call(z, params, None, L=CHUNK, nvalid=CHUNK, nb=nb, chunks=chunks, row_map=None,
                     name="odd_mixer_prompt")


def odd_mixer_sample(z, params, states):
    return _odd_call(z, params, states, L=S_LEN, nvalid=DEC_SEQ, nb=z.shape[0], chunks=1, row_map=None,
                     name="odd_mixer_sample")


def _odd_params(odd_w_in, conv_w, conv_b, dt_bias, a_log, d_skip, ssm_norm_g, shift_mu, rwkv_w0, rwkv_w2, rwkv_a0,
                rwkv_a2, rwkv_g2, rwkv_k_k, rwkv_k_a, rwkv_r_k, lnx_g, lnx_b):
    o_dt = DI_C + CONV_DIM
    o_rw = o_dt + H_C
    w = jnp.concatenate([odd_w_in[:, :o_dt], odd_w_in[:, o_rw:], odd_w_in[:, o_dt:o_rw],
                         jnp.zeros((D_MODEL, ODD_PAD - ODD_IN), odd_w_in.dtype)], axis=1)
    vec = jnp.stack([jnp.repeat(d_skip, P_C), ssm_norm_g, shift_mu[:DI_D], shift_mu[DI_D:2 * DI_D],
                     shift_mu[2 * DI_D:3 * DI_D], rwkv_w0, rwkv_a0, rwkv_k_k, rwkv_k_a, rwkv_r_k.reshape(-1),
                     lnx_g, lnx_b] + [jnp.zeros((DI_D,), F32)] * 4)
    cpack = jnp.concatenate([conv_w, conv_b[None], jnp.zeros((3, CONV_DIM), F32)], axis=0)
    zt = jnp.zeros((T_W,), F32)
    tpack = jnp.stack([zt.at[:T_DT].set(shift_mu[3 * DI_D:]), zt.at[T_DT:T_DT + H_C].set(dt_bias),
                       zt.at[T_DT:T_DT + H_C].set(-jnp.exp(a_log.astype(F32)))] + [zt] * 5)
    zw = jnp.zeros((T_W, DI_D), F32)
    w2p = zw.at[:R_W].set(rwkv_w2)
    a2p = zw.at[R_W:R_W + R_A].set(rwkv_a2).astype(BF16)
    g2p = zw.at[R_W + R_A:T_DT].set(rwkv_g2).astype(BF16)
    return w.astype(BF16), (vec, cpack, tpack, w2p, a2p, g2p)


def _top16_desc(cur):
    vals = []
    for _ in range(PEER_TOPK):
        m = jnp.max(cur, axis=0, keepdims=True)
        vals.append(m)
        cur = jnp.where(cur == m, -jnp.inf, cur)
    return vals


def _peer_kernel(x_ref, xb_ref, wq_ref, sk_ref, u_ref, v_ref, g_ref, b_ref, o_ref, ob_ref,
                 s1_s, s2_s, e1_s, e2_s, tau_s, cand_s, acc_s, *, tm, te):
    e = pl.program_id(1)
    ne = pl.num_programs(1)
    nk = PEER_KEYS

    @pl.when(e == 0)
    def _route():
        q_t = lax.dot_general(wq_ref[...], xb_ref[...], NT, preferred_element_type=F32)
        for h in range(PEER_HEADS):
            tops = []
            for c in range(2):
                hc = 2 * h + c
                s = jnp.dot(sk_ref[hc], q_t[hc * nk:(hc + 1) * nk, :].astype(BF16), preferred_element_type=F32)
                (s1_s if c == 0 else s2_s)[h] = s
                tops.append(_top16_desc(s))
            t1, t2 = tops
            t2_all = jnp.concatenate(t2, axis=0)
            for a in range(PEER_TOPK):
                cand_s[a * PEER_TOPK:(a + 1) * PEER_TOPK, :] = t1[a] + t2_all
            best = _top16_desc(cand_s[...])
            mx = t1[0] + t2[0]
            z = jnp.zeros_like(mx)
            for m in best:
                z = z + jnp.exp(m - mx)
            tau_s[h:h + 1, :] = best[-1]
            e1_s[h] = jnp.exp(s1_s[h] - t1[0])
            e2_s[h] = jnp.exp(s2_s[h] - t2[0]) / z
        acc_s[...] = jnp.zeros_like(acc_s)

    h_t = lax.dot_general(u_ref[...], xb_ref[...], NT, preferred_element_type=F32)
    parts = []
    for j in range(te // nk):
        i1 = e * (te // nk) + j
        w = jnp.zeros((nk, tm), F32)
        for h in range(PEER_HEADS):
            c = s2_s[h] + s1_s[h, pl.ds(i1, 1), :]
            w = w + jnp.where(c >= tau_s[h:h + 1, :], e2_s[h] * e1_s[h, pl.ds(i1, 1), :], 0.0)
        hj = h_t[j * nk:(j + 1) * nk, :]
        act = 0.5 * hj * (1.0 + lax.erf(hj * (2.0 ** -0.5)))
        parts.append((w * act).astype(BF16))
    p_t = jnp.concatenate(parts, axis=0)
    acc_s[...] += lax.dot_general(p_t, v_ref[...], TN, preferred_element_type=F32)

    @pl.when(e == ne - 1)
    def _fin():
        y = _ln_rows(ALPHA * x_ref[...] + acc_s[...], g_ref[...], b_ref[...])
        o_ref[...] = y
        ob_ref[...] = y.astype(BF16)


def peer_ln(x, xb, wq_t, sk, u, v, g, b, *, tm=512, te=512):
    m, d = x.shape
    kern = functools.partial(_peer_kernel, tm=tm, te=te)
    const = dict(pipeline_mode=pl.Buffered(1))
    return pl.pallas_call(
        kern,
        out_shape=(jax.ShapeDtypeStruct((m, d), F32), jax.ShapeDtypeStruct((m, d), BF16)),
        grid=(m // tm, PEER_EXPERTS // te),
        in_specs=[pl.BlockSpec((tm, d), lambda i, e: (i, 0), **const),
                  pl.BlockSpec((tm, d), lambda i, e: (i, 0), **const),
                  pl.BlockSpec(wq_t.shape, lambda i, e: (0, 0), **const),
                  pl.BlockSpec(sk.shape, lambda i, e: (0, 0, 0), **const),
                  pl.BlockSpec((te, d), lambda i, e: (e, 0)),
                  pl.BlockSpec((te, d), lambda i, e: (e, 0)),
                  pl.BlockSpec((1, d), lambda i, e: (0, 0), **const),
                  pl.BlockSpec((1, d), lambda i, e: (0, 0), **const)],
        out_specs=(pl.BlockSpec((tm, d), lambda i, e: (i, 0)), pl.BlockSpec((tm, d), lambda i, e: (i, 0))),
        scratch_shapes=[pltpu.VMEM((PEER_HEADS, PEER_KEYS, tm), F32), pltpu.VMEM((PEER_HEADS, PEER_KEYS, tm), F32),
                        pltpu.VMEM((PEER_HEADS, PEER_KEYS, tm), F32), pltpu.VMEM((PEER_HEADS, PEER_KEYS, tm), F32),
                        pltpu.VMEM((PEER_HEADS, tm), F32), pltpu.VMEM((PEER_TOPK * PEER_TOPK, tm), F32),
                        pltpu.VMEM((tm, d), F32)],
        compiler_params=pltpu.CompilerParams(dimension_semantics=("arbitrary", "arbitrary"),
                                             vmem_limit_bytes=VMEM_LIMIT),
        name="peer_ln",
    )(x, xb, wq_t, sk, u, v, g.reshape(1, d), b.reshape(1, d))


def _sample_rows(z):
    f = z.shape[1]
    zs = lax.slice(z, (S_ROW0, 0), (S_ROW0 + DEC_BATCH * DEC_SEQ, f)).reshape(DEC_BATCH, DEC_SEQ, f)
    return jnp.pad(zs, ((0, 0), (0, S_LEN - DEC_SEQ), (0, 0)))


def _merge_rows(buf, sample_out):
    f = buf.shape[1]
    tail = jnp.concatenate([sample_out[:, :DEC_SEQ].reshape(DEC_BATCH * DEC_SEQ, f),
                            jnp.zeros((M_PAD - S_ROW0 - DEC_BATCH * DEC_SEQ, f), buf.dtype)], axis=0)
    return lax.dynamic_update_slice(buf, tail, (S_ROW0, 0))


def kernel(x_prompt, x_sample, state_hgrn, state_ret, state_ssm, state_conv, state_wkv, state_shift, meta_tokens, ln_g, ln_b, even_w_in, hgrn_lb_logits, hgrn_norm_g, even_w_out, odd_w_in, conv_w, conv_b, dt_bias, a_log, d_skip, ssm_norm_g, shift_mu, rwkv_w0, rwkv_w2, rwkv_a0, rwkv_a2, rwkv_g2, rwkv_k_k, rwkv_k_a, rwkv_r_k, lnx_g, lnx_b, odd_w_out, peer_w_query, peer_sub_keys, peer_u, peer_v):
    dt = x_prompt.dtype
    lead = jnp.concatenate([jnp.zeros((P_LEAD, D_MODEL), dt), meta_tokens.astype(dt)], axis=0)
    xp = jnp.concatenate([jnp.broadcast_to(lead[None], (BATCH, CHUNK, D_MODEL)), x_prompt], axis=1)
    x = jnp.concatenate([xp.reshape(S_ROW0, D_MODEL), x_sample.reshape(DEC_BATCH * DEC_SEQ, D_MODEL),
                         jnp.zeros((M_PAD - S_ROW0 - DEC_BATCH * DEC_SEQ, D_MODEL), dt)], axis=0)
    xb = x.astype(BF16)

    cos_p, sin_p = _rope_tables(jnp.arange(P_ROWS) - P_LEAD)
    cos_s, sin_s = _rope_tables(PAST_LEN + jnp.arange(S_LEN))
    lb_table = jnp.cumsum(jax.nn.softmax(hgrn_lb_logits.astype(F32), axis=0), axis=0)

    z = matmul(xb, even_w_in[0].astype(BF16), tm=TOK_TILE, tn=1024, name="even_in")
    lb = lb_table[0].reshape(1, -1)
    ng = hgrn_norm_g[0].reshape(1, -1)
    mix_p, hgrn_p, ret_p = even_mixer_prompt(z, lb, ng, cos_p, sin_p)
    mix_s, hgrn_s, ret_s = even_mixer_sample(_sample_rows(z), lb, ng, cos_s, sin_s, state_hgrn[0], state_ret[0])
    mix = _merge_rows(mix_p, mix_s)
    x, xb = proj_ln(x, mix, even_w_out[0].astype(BF16), ln_g[0, 0], ln_b[0, 0], tm=TOK_TILE, name="even_out")
    x, xb = peer_ln(x, xb, peer_w_query[0].T.astype(BF16),
                    peer_sub_keys[0].reshape(2 * PEER_HEADS, PEER_KEYS, PEER_QDIM // 2).astype(BF16),
                    peer_u[0].astype(BF16), peer_v[0].astype(BF16), ln_g[0, 1], ln_b[0, 1])

    w_in1, params = _odd_params(odd_w_in[0], conv_w[0], conv_b[0], dt_bias[0], a_log[0], d_skip[0], ssm_norm_g[0],
                                shift_mu[0], rwkv_w0[0], rwkv_w2[0], rwkv_a0[0], rwkv_a2[0], rwkv_g2[0], rwkv_k_k[0],
                                rwkv_k_a[0], rwkv_r_k[0], lnx_g[0], lnx_b[0])
    z = matmul(xb, w_in1, tm=TOK_TILE, tn=1024, name="odd_in")
    mix_p, ssm_p, wkv_p, conv_p, shift_p = odd_mixer_prompt(z, params)
    conv8 = jnp.pad(state_conv[0], ((0, 0), (8 - (CONV_W - 1), 0), (0, 0)))
    shift8 = jnp.pad(state_shift[0][:, None, :], ((0, 0), (7, 0), (0, SHIFT_W - SHIFT_DIM)))
    mix_s, ssm_s, wkv_s, conv_s, shift_s = odd_mixer_sample(_sample_rows(z), params,
                                                            (state_ssm[0], state_wkv[0], conv8, shift8))
    mix = _merge_rows(mix_p, mix_s)
    x, xb = proj_ln(x, mix, odd_w_out[0].astype(BF16), ln_g[1, 0], ln_b[1, 0], tm=TOK_TILE, name="odd_out")
    x, xb = peer_ln(x, xb, peer_w_query[1].T.astype(BF16),
                    peer_sub_keys[1].reshape(2 * PEER_HEADS, PEER_KEYS, PEER_QDIM // 2).astype(BF16),
                    peer_u[1].astype(BF16), peer_v[1].astype(BF16), ln_g[1, 1], ln_b[1, 1])

    y_prompt = x[:S_ROW0].reshape(BATCH, P_ROWS, D_MODEL)[:, CHUNK:]
    y_sample = x[S_ROW0:S_ROW0 + DEC_BATCH * DEC_SEQ].reshape(DEC_BATCH, DEC_SEQ, D_MODEL)
    nc = CONV_W - 1
    return (y_prompt, y_sample, hgrn_p[None], hgrn_s[None], ret_p[None], ret_s[None], ssm_p[None], ssm_s[None],
            conv_p[None, :, 8 - nc:], conv_s[None, :, DEC_SEQ - nc:DEC_SEQ], wkv_p[None], wkv_s[None],
            shift_p[None, :, 7, :SHIFT_DIM], shift_s[None, :, DEC_SEQ - 1, :SHIFT_DIM])
```

```python
import functools
import math

import jax
import jax.numpy as jnp
from jax import lax
from jax.experimental import pallas as pl
from jax.experimental.pallas import tpu as pltpu

D_MODEL = 2048
BATCH = 4
SEQ = 2048
DEPTH = 2
DEC_BATCH = 128
DEC_SEQ = 4
PAST_LEN = 16384
N_META = 16
CHUNK = 64

H_A, DK_A, DV_A = 8, 128, 128
H_B, DK_B, DV_B = 4, 128, 256
EVEN_IN = 4 * H_A * DK_A + 2 * H_B * DK_B + 2 * H_B * DV_B
EVEN_MIX = H_A * DV_A + H_B * DV_B

H_C, P_C, N_C, G_C, CONV_W = 16, 64, 128, 2, 4
DI_C = H_C * P_C
CONV_DIM = DI_C + 2 * G_C * N_C
H_D, P_D = 16, 64
DI_D = H_D * P_D
R_W, R_A, R_G = 64, 64, 160
SHIFT_DIM = 3 * DI_D + R_W + R_A + R_G
ODD_IN = DI_C + CONV_DIM + H_C + SHIFT_DIM
ODD_MIX = DI_C + DI_D

PEER_KEYS = 128
PEER_EXPERTS = PEER_KEYS * PEER_KEYS
PEER_HEADS = 8
PEER_TOPK = 16
PEER_QDIM = 256

ALPHA = (2.0 * DEPTH) ** 0.25
LN_EPS = 1e-5
RMS_EPS = 1e-6
RWKV_GN_EPS = 64e-5
ROPE_BASE = 10000.0

F32 = jnp.float32
BF16 = jnp.bfloat16
HIGHEST = lax.Precision.HIGHEST

P_LEAD = CHUNK - N_META
P_ROWS = CHUNK + SEQ
P_CHUNKS = P_ROWS // CHUNK
S_ROW0 = BATCH * P_ROWS
S_LEN = 8
TOK_TILE = 512
M_PAD = -(-(S_ROW0 + DEC_BATCH * DEC_SEQ) // 1024) * 1024

VMEM_LIMIT = 56 * 1024 * 1024


def _dot(a, b, dims=(((1,), (0,)), ((), ())), precision=None):
    return lax.dot_general(a, b, dims, precision=precision, preferred_element_type=F32)


def _dot_bf(a, b, dims=(((1,), (0,)), ((), ()))):
    return lax.dot_general(a.astype(BF16), b.astype(BF16), dims, preferred_element_type=F32)


def _parts(x, n):
    out, rem = [], x
    for i in range(n):
        p = rem.astype(BF16)
        out.append(p)
        if i + 1 < n:
            rem = rem - p.astype(F32)
    return out


def _dot_sel(x, sel, dims=(((1,), (0,)), ((), ())), n=3, sel_first=False):
    sel = sel.astype(BF16)
    acc = None
    for p in _parts(x, n):
        d = lax.dot_general(sel, p, dims, preferred_element_type=F32) if sel_first else \
            lax.dot_general(p, sel, dims, preferred_element_type=F32)
        acc = d if acc is None else acc + d
    return acc


NT = (((1,), (1,)), ((), ()))
TN = (((0,), (0,)), ((), ()))


def _sigmoid(x):
    return 1.0 / (1.0 + jnp.exp(-x))


def _silu(x):
    return x * _sigmoid(x)


def _mm_kernel(x_ref, w_ref, o_ref):
    o_ref[...] = jnp.dot(x_ref[...].astype(BF16), w_ref[...], preferred_element_type=F32).astype(o_ref.dtype)


def matmul(x, w, *, tm, tn, out_dtype=F32, name="matmul"):
    m, k = x.shape
    n = w.shape[1]
    return pl.pallas_call(
        _mm_kernel,
        out_shape=jax.ShapeDtypeStruct((m, n), out_dtype),
        grid=(n // tn, m // tm),
        in_specs=[pl.BlockSpec((tm, k), lambda j, i: (i, 0)),
                  pl.BlockSpec((k, tn), lambda j, i: (0, j))],
        out_specs=pl.BlockSpec((tm, tn), lambda j, i: (i, j)),
        compiler_params=pltpu.CompilerParams(dimension_semantics=("arbitrary", "arbitrary"),
                                             vmem_limit_bytes=VMEM_LIMIT),
        name=name,
    )(x, w)


def _ln_rows(v, g, b):
    mu = jnp.mean(v, -1, keepdims=True)
    c = v - mu
    var = jnp.mean(c * c, -1, keepdims=True)
    return c * lax.rsqrt(var + LN_EPS) * g + b


def _proj_ln_kernel(x_ref, m_ref, w_ref, g_ref, b_ref, o_ref, ob_ref):
    acc = jnp.dot(m_ref[...].astype(BF16), w_ref[...], preferred_element_type=F32)
    y = _ln_rows(ALPHA * x_ref[...] + acc, g_ref[...], b_ref[...])
    o_ref[...] = y
    ob_ref[...] = y.astype(BF16)


def proj_ln(x, mix, w, g, b, *, tm, name="proj_ln"):
    m, d = x.shape
    k = mix.shape[1]
    return pl.pallas_call(
        _proj_ln_kernel,
        out_shape=(jax.ShapeDtypeStruct((m, d), F32), jax.ShapeDtypeStruct((m, d), BF16)),
        grid=(m // tm,),
        in_specs=[pl.BlockSpec((tm, d), lambda i: (i, 0)),
                  pl.BlockSpec((tm, k), lambda i: (i, 0)),
                  pl.BlockSpec((k, d), lambda i: (0, 0)),
                  pl.BlockSpec((1, d), lambda i: (0, 0)),
                  pl.BlockSpec((1, d), lambda i: (0, 0))],
        out_specs=(pl.BlockSpec((tm, d), lambda i: (i, 0)), pl.BlockSpec((tm, d), lambda i: (i, 0))),
        compiler_params=pltpu.CompilerParams(dimension_semantics=("arbitrary",),
                                             vmem_limit_bytes=VMEM_LIMIT),
        name=name,
    )(x, mix, w, g.reshape(1, d), b.reshape(1, d))


def _row_info(L, lead, nvalid):
    r = lax.broadcasted_iota(jnp.int32, (L, 1), 0)
    return r, (r >= lead) & (r < lead + nvalid)


def _tril(L, strict=False):
    r = lax.broadcasted_iota(jnp.int32, (L, L), 0)
    c = lax.broadcasted_iota(jnp.int32, (L, L), 1)
    return (c < r) if strict else (c <= r)


def _hgrn_head(q, k, v, logf, s0, L):
    tril = _tril(L).astype(F32)
    b = _dot_sel(logf, tril, sel_first=True)
    ones = jnp.ones((L, DK_A), F32)
    b_end_col = _dot_sel(logf, ones, TN)
    b_end_row = b[L - 1:L, :]
    o = _dot_bf(q * jnp.exp(b), s0)
    s_new = jnp.exp(b_end_col) * s0 + _dot_bf(k * jnp.exp(b_end_row - b), v, TN)
    sub = min(16, L)
    outs = []
    for i in range(L // sub):
        r0 = i * sub
        qs, bs = q[r0:r0 + sub], b[r0:r0 + sub]
        acc = o[r0:r0 + sub]
        if i > 0:
            ref = b[r0 - 1:r0, :]
            a = _dot_bf(qs * jnp.exp(bs - ref), k[0:r0] * jnp.exp(ref - b[0:r0]), NT)
            acc = acc + _dot_bf(a, v[0:r0])
        t_idx = lax.broadcasted_iota(jnp.int32, (sub, 1), 0)
        for s in range(sub):
            rs = r0 + s
            d = jnp.where(t_idx >= s, bs - b[rs:rs + 1, :], -jnp.inf)
            a_col = jnp.sum(qs * k[rs:rs + 1, :] * jnp.exp(d), axis=-1, keepdims=True)
            acc = acc + a_col * v[rs:rs + 1, :]
        outs.append(acc)
    return jnp.concatenate(outs, axis=0) if len(outs) > 1 else outs[0], s_new


def _ret_head(q, k, v, cnt_col, cnt_row, log_gamma, s0, L):
    causal = _tril(L)
    seg = jnp.where(causal, jnp.exp((cnt_col - cnt_row) * log_gamma), 0.0)
    scores = _dot_bf(q, k, NT) * seg
    o = _dot_bf(scores, v) + _dot_bf(q * jnp.exp(cnt_col * log_gamma), s0)
    cnt_end = cnt_col[L - 1:L, :]
    s_new = jnp.exp(cnt_end * log_gamma) * s0 + _dot_bf(k * jnp.exp((cnt_end - cnt_col) * log_gamma), v, TN)
    return o, s_new


def _even_kernel(*refs, L, nvalid, chunks, has_s0):
    if has_s0:
        z_ref, lb_ref, ng_ref, cos_ref, sin_ref, sh0_ref, sr0_ref, o_ref, sho_ref, sro_ref, sh_s, sr_s = refs
    else:
        z_ref, lb_ref, ng_ref, cos_ref, sin_ref, o_ref, sho_ref, sro_ref, sh_s, sr_s = refs
    c = pl.program_id(1)

    @pl.when(c == 0)
    def _():
        if has_s0:
            sh_s[...] = sh0_ref[0]
            sr_s[...] = sr0_ref[0]
        else:
            sh_s[...] = jnp.zeros_like(sh_s)
            sr_s[...] = jnp.zeros_like(sr_s)

    lead = jnp.where(c == 0, P_LEAD, 0) if chunks > 1 else 0
    r, valid = _row_info(L, lead, nvalid)
    cnt_col = jnp.clip(r + 1 - lead, 0, nvalid).astype(F32)
    rr = lax.broadcasted_iota(jnp.int32, (1, L), 1)
    cnt_row = jnp.clip(rr + 1 - lead, 0, nvalid).astype(F32)

    oa, ob = H_A * DK_A, 0
    off_q, off_f, off_i, off_g = 0, H_A * DK_A, 2 * H_A * DK_A, 2 * H_A * DK_A + H_A * DV_A
    base_b = 2 * H_A * DK_A + 2 * H_A * DV_A
    off_bq, off_bk, off_bv = base_b, base_b + H_B * DK_B, base_b + 2 * H_B * DK_B
    off_bg = off_bv + H_B * DV_B
    ng = ng_ref[...]
    for h in range(H_A):
        sl = slice(h * DK_A, (h + 1) * DK_A)
        aq = z_ref[:, off_q + h * DK_A: off_q + (h + 1) * DK_A]
        af = z_ref[:, off_f + h * DK_A: off_f + (h + 1) * DK_A]
        ai = z_ref[:, off_i + h * DV_A: off_i + (h + 1) * DV_A]
        ag = z_ref[:, off_g + h * DV_A: off_g + (h + 1) * DV_A]
        lb = lb_ref[:, sl]
        logf = jnp.where(valid, jnp.log(lb + (1.0 - lb) * _sigmoid(af)), 0.0)
        k = jnp.where(valid, (1.0 - lb) * _sigmoid(-af), 0.0)
        o, s_new = _hgrn_head(_silu(aq), k, ai, logf, sh_s[h], L)
        sh_s[h] = s_new
        o = o * lax.rsqrt(jnp.mean(o * o, -1, keepdims=True) + RMS_EPS) * ng
        o_ref[:, h * DV_A:(h + 1) * DV_A] = o * _silu(ag)
    cosv, sinv = cos_ref[...], sin_ref[...]
    for h in range(H_B):
        bq = z_ref[:, off_bq + h * DK_B: off_bq + (h + 1) * DK_B]
        bk = z_ref[:, off_bk + h * DK_B: off_bk + (h + 1) * DK_B]
        bv = z_ref[:, off_bv + h * DV_B: off_bv + (h + 1) * DV_B]
        bg = z_ref[:, off_bg + h * DV_B: off_bg + (h + 1) * DV_B]
        q = bq * cosv + pltpu.roll(bq, DK_B // 2, 1) * sinv
        k = (bk * cosv + pltpu.roll(bk, DK_B // 2, 1) * sinv) * (DK_B ** -0.5)
        k = jnp.where(valid, k, 0.0)
        log_gamma = math.log(1.0 - 2.0 ** (-5.0 - h))
        o, s_new = _ret_head(q, k, bv, cnt_col, cnt_row, log_gamma, sr_s[h], L)
        sr_s[h] = s_new
        mu = jnp.mean(o, -1, keepdims=True)
        cc = o - mu
        o = cc * lax.rsqrt(jnp.mean(cc * cc, -1, keepdims=True) + LN_EPS)
        o_ref[:, H_A * DV_A + h * DV_B: H_A * DV_A + (h + 1) * DV_B] = o * _silu(bg)

    @pl.when(c == chunks - 1)
    def _():
        sho_ref[0] = sh_s[...]
        sro_ref[0] = sr_s[...]


def even_mixer_prompt(z, lb, ng, cos, sin, *, nb=BATCH, chunks=P_CHUNKS):
    kern = functools.partial(_even_kernel, L=CHUNK, nvalid=CHUNK, chunks=chunks, has_s0=False)
    return pl.pallas_call(
        kern,
        out_shape=(jax.ShapeDtypeStruct((z.shape[0], EVEN_MIX), F32),
                   jax.ShapeDtypeStruct((nb, H_A, DK_A, DV_A), F32),
                   jax.ShapeDtypeStruct((nb, H_B, DK_B, DV_B), F32)),
        grid=(nb, chunks),
        in_specs=[pl.BlockSpec((CHUNK, EVEN_IN), lambda b, c: (b * chunks + c, 0)),
                  pl.BlockSpec((1, H_A * DK_A), lambda b, c: (0, 0)),
                  pl.BlockSpec((1, DV_A), lambda b, c: (0, 0)),
                  pl.BlockSpec((CHUNK, DK_B), lambda b, c: (c, 0)),
                  pl.BlockSpec((CHUNK, DK_B), lambda b, c: (c, 0))],
        out_specs=(pl.BlockSpec((CHUNK, EVEN_MIX), lambda b, c: (b * chunks + c, 0)),
                   pl.BlockSpec((1, H_A, DK_A, DV_A), lambda b, c: (b, 0, 0, 0)),
                   pl.BlockSpec((1, H_B, DK_B, DV_B), lambda b, c: (b, 0, 0, 0))),
        scratch_shapes=[pltpu.VMEM((H_A, DK_A, DV_A), F32), pltpu.VMEM((H_B, DK_B, DV_B), F32)],
        compiler_params=pltpu.CompilerParams(dimension_semantics=("arbitrary", "arbitrary"),
                                             vmem_limit_bytes=VMEM_LIMIT),
        name="even_mixer_prompt",
    )(z, lb, ng, cos, sin)


def even_mixer_sample(z, lb, ng, cos, sin, s_hgrn, s_ret):
    nb = z.shape[0]
    kern = functools.partial(_even_kernel, L=S_LEN, nvalid=DEC_SEQ, chunks=1, has_s0=True)
    return pl.pallas_call(
        kern,
        out_shape=(jax.ShapeDtypeStruct((nb, S_LEN, EVEN_MIX), F32),
                   jax.ShapeDtypeStruct((nb, H_A, DK_A, DV_A), F32),
                   jax.ShapeDtypeStruct((nb, H_B, DK_B, DV_B), F32)),
        grid=(nb, 1),
        in_specs=[pl.BlockSpec((None, S_LEN, EVEN_IN), lambda b, c: (b, 0, 0)),
                  pl.BlockSpec((1, H_A * DK_A), lambda b, c: (0, 0)),
                  pl.BlockSpec((1, DV_A), lambda b, c: (0, 0)),
                  pl.BlockSpec((S_LEN, DK_B), lambda b, c: (0, 0)),
                  pl.BlockSpec((S_LEN, DK_B), lambda b, c: (0, 0)),
                  pl.BlockSpec((1, H_A, DK_A, DV_A), lambda b, c: (b, 0, 0, 0)),
                  pl.BlockSpec((1, H_B, DK_B, DV_B), lambda b, c: (b, 0, 0, 0))],
        out_specs=(pl.BlockSpec((None, S_LEN, EVEN_MIX), lambda b, c: (b, 0, 0)),
                   pl.BlockSpec((1, H_A, DK_A, DV_A), lambda b, c: (b, 0, 0, 0)),
                   pl.BlockSpec((1, H_B, DK_B, DV_B), lambda b, c: (b, 0, 0, 0))),
        scratch_shapes=[pltpu.VMEM((H_A, DK_A, DV_A), F32), pltpu.VMEM((H_B, DK_B, DV_B), F32)],
        compiler_params=pltpu.CompilerParams(dimension_semantics=("arbitrary", "arbitrary"),
                                             vmem_limit_bytes=VMEM_LIMIT),
        name="even_mixer_sample",
    )(z, lb, ng, cos, sin, s_hgrn, s_ret)


def _rope_tables(pos):
    half = DK_B // 2
    inv = ROPE_BASE ** (-jnp.arange(half, dtype=F32) / half)
    ang = pos.astype(F32)[:, None] * inv
    cos, sin = jnp.cos(ang), jnp.sin(ang)
    return jnp.concatenate([cos, cos], -1), jnp.concatenate([-sin, sin], -1)


ODD_PAD = 6144
O_Z, O_X, O_R, O_K, O_V, O_T = 0, DI_C, DI_C + CONV_DIM, DI_C + CONV_DIM + DI_D, DI_C + CONV_DIM + 2 * DI_D, \
    DI_C + CONV_DIM + 3 * DI_D
T_W = 384
T_DT = R_W + R_A + R_G
SHIFT_W = 3 * DI_D + T_W
HALF = 64


def _softplus(x):
    return jnp.maximum(x, 0.0) + jnp.log1p(jnp.exp(-jnp.abs(x)))


def _lane_lt(n, width):
    return lax.broadcasted_iota(jnp.int32, (1, width), 1) < n


def _pair_ones():
    r = lax.broadcasted_iota(jnp.int32, (2 * HALF, 2 * HALF), 0) < HALF
    c = lax.broadcasted_iota(jnp.int32, (2 * HALF, 2 * HALF), 1) < HALF
    return r == c


def _stack_pair(x, m_a):
    return jnp.concatenate([jnp.where(m_a, x, 0.0), jnp.where(m_a, 0.0, x)], axis=0)


def _rwkv_pair(r, k, v, al, be, logw, G, bd, L):
    m_a = _lane_lt(HALF, 2 * HALF)
    e_inv = jnp.exp(-G)
    lhs = jnp.concatenate([al * jnp.exp(G - logw), r * jnp.exp(G)], axis=0).astype(BF16)
    ks = _stack_pair(k * e_inv, m_a)
    bs = _stack_pair(be * e_inv, m_a)
    vs = _stack_pair(v, m_a)
    rr = lax.broadcasted_iota(jnp.int32, (2 * L, 2 * L), 0)
    cc = lax.broadcasted_iota(jnp.int32, (2 * L, 2 * L), 1)
    cc = jnp.where(cc >= L, cc - L, cc)
    keep = cc < jnp.where(rr >= L, rr - L + 1, rr)
    a_k = jnp.where(keep, _dot_bf(lhs, ks, NT), 0.0)
    a_b = jnp.where(keep, _dot_bf(lhs, bs, NT), 0.0)
    base = _dot_bf(lhs, bd, NT) + _dot_bf(a_k, vs)
    m_l = _lane_lt(L, 2 * L)
    a_ba = a_b[0:L]
    n_bd = jnp.concatenate([jnp.where(m_l, a_ba, 0.0), jnp.where(m_l, 0.0, a_ba)], axis=0)
    eye = (lax.broadcasted_iota(jnp.int32, (2 * L, 2 * L), 0) ==
           lax.broadcasted_iota(jnp.int32, (2 * L, 2 * L), 1)).astype(F32)
    ri = lax.broadcasted_iota(jnp.int32, (2 * L, 2 * L), 0)
    ci = lax.broadcasted_iota(jnp.int32, (2 * L, 2 * L), 1)
    n4 = jnp.where((ri // 4) == (ci // 4), n_bd, 0.0)
    t_inv = eye - n4
    t_inv = t_inv + _dot_bf(t_inv, _dot_bf(n4, n4))
    blk = 4
    while blk < L:
        off = ((ri // (2 * blk)) == (ci // (2 * blk))) & ((ri % (2 * blk)) >= blk) & ((ci % (2 * blk)) < blk)
        t_inv = t_inv - _dot_bf(t_inv, _dot_bf(jnp.where(off, n_bd, 0.0), t_inv))
        blk *= 2
    t_st = t_inv[0:L, :] + t_inv[L:2 * L, :]
    u = _dot_bf(t_st, _stack_pair(base[0:L], m_a))
    y = base[L:2 * L] - _dot_bf(a_b[L:2 * L], _stack_pair(u, m_a))
    gl = G[L - 1:L, :]
    dec = jnp.exp(gl - G)
    upd = _dot_bf(jnp.concatenate([v, u], axis=0), jnp.concatenate([k * dec, -(be * dec)], axis=0), TN)
    bd_new = bd * jnp.exp(gl) + jnp.where(_pair_ones(), upd, 0.0)
    return y, bd_new


def _ssd_pair(xdt, cg, bg, cb, bcs, brows, sp, L):
    m_a = _lane_lt(HALF, 2 * HALF)
    causal = _tril(L)
    out = jnp.zeros((L, 2 * HALF), F32)
    ends = [bc[L - 1:L, :] for bc in bcs]
    sp_new = sp * jnp.where(m_a, jnp.exp(ends[0]), jnp.exp(ends[1]))
    for x in range(2):
        keep = m_a if x == 0 else jnp.logical_not(m_a)
        seg = jnp.where(causal, jnp.exp(bcs[x] - brows[x]), 0.0)
        xm = jnp.where(keep, xdt, 0.0)
        out = out + _dot_bf(cb * seg, xm) + _dot_bf(cg * jnp.exp(bcs[x]), jnp.where(keep, sp, 0.0))
        sp_new = sp_new + _dot_bf(bg * jnp.exp(ends[x] - bcs[x]), xm, TN)
    return out, sp_new


def _odd_kernel(*refs, L, nvalid, chunks, has_s0):
    if has_s0:
        (z_ref, vec_ref, cp_ref, tp_ref, w2_ref, a2_ref, g2_ref, ssm0_ref, wkv0_ref, conv0_ref, shift0_ref,
         o_ref, ssmo_ref, wkvo_ref, convo_ref, shifto_ref, ssm_s, wkv_s, conv_c, shift_c) = refs
    else:
        (z_ref, vec_ref, cp_ref, tp_ref, w2_ref, a2_ref, g2_ref,
         o_ref, ssmo_ref, wkvo_ref, convo_ref, shifto_ref, ssm_s, wkv_s, conv_c, shift_c) = refs
    c = pl.program_id(1)
    npair = H_C // 2

    @pl.when(c == 0)
    def _():
        if has_s0:
            zpad = jnp.zeros((HALF, HALF), F32)
            for p in range(npair):
                ssm_s[p] = jnp.concatenate([ssm0_ref[0, 2 * p], ssm0_ref[0, 2 * p + 1]], axis=1)
                wkv_s[p] = jnp.concatenate([jnp.concatenate([wkv0_ref[0, 2 * p], zpad], axis=1),
                                            jnp.concatenate([zpad, wkv0_ref[0, 2 * p + 1]], axis=1)], axis=0)
            conv_c[...] = conv0_ref[0]
            shift_c[...] = shift0_ref[0]
        else:
            ssm_s[...] = jnp.zeros_like(ssm_s)
            wkv_s[...] = jnp.zeros_like(wkv_s)
            conv_c[...] = jnp.zeros_like(conv_c)
            shift_c[...] = jnp.zeros_like(shift_c)

    lead = jnp.where(c == 0, P_LEAD, 0) if chunks > 1 else 0
    r_idx, valid = _row_info(L, lead, nvalid)
    m_a = _lane_lt(HALF, 2 * HALF)
    tril = _tril(L).astype(F32)
    triu = (lax.broadcasted_iota(jnp.int32, (L, L), 0) <= lax.broadcasted_iota(jnp.int32, (L, L), 1)).astype(F32)
    ones_blk = _pair_ones().astype(F32)

    xbc = jnp.where(valid, z_ref[:, O_X:O_X + CONV_DIM], 0.0)
    c8 = conv_c[...]
    r8 = lax.broadcasted_iota(jnp.int32, (8, 1), 0)
    conv = cp_ref[CONV_W:CONV_W + 1, :] + xbc * cp_ref[CONV_W - 1:CONV_W, :]
    for j in range(1, CONV_W):
        rolled = pltpu.roll(xbc, j, 0)
        head = jnp.where(r8 < j, pltpu.roll(c8, j, 0), rolled[0:8])
        sh = jnp.concatenate([head, rolled[8:]], axis=0) if L > 8 else head
        conv = conv + sh * cp_ref[CONV_W - 1 - j:CONV_W - j, :]
    conv_c[...] = xbc[L - 8:L]
    act = _silu(conv)
    xc, bm, cm = act[:, :DI_C], act[:, DI_C:DI_C + G_C * N_C], act[:, DI_C + G_C * N_C:]

    t_raw = z_ref[:, O_T:O_T + T_W]
    dt = jnp.where(valid, _softplus(t_raw + tp_ref[1:2, :])[:, T_DT:T_DT + H_C], 0.0)
    logf = dt * tp_ref[2:3, T_DT:T_DT + H_C]
    bc_all = _dot_sel(logf, tril, sel_first=True)
    brow_all = _dot_sel(logf, triu, TN)
    per_g = npair // G_C
    for g in range(G_C):
        cg = cm[:, g * N_C:(g + 1) * N_C]
        bg = bm[:, g * N_C:(g + 1) * N_C]
        cb = _dot_bf(cg, bg, NT)
        ys, ss = [], jnp.zeros((L, 1), F32)
        for pp in range(per_g):
            p = g * per_g + pp
            sl = slice(p * 2 * HALF, (p + 1) * 2 * HALF)
            h_a, h_b = 2 * p, 2 * p + 1
            xcp = xc[:, sl]
            xdt = xcp * jnp.where(m_a, dt[:, h_a:h_a + 1], dt[:, h_b:h_b + 1])
            o, sp_new = _ssd_pair(xdt, cg, bg, cb, [bc_all[:, h_a:h_a + 1], bc_all[:, h_b:h_b + 1]],
                                  [brow_all[h_a:h_a + 1, :], brow_all[h_b:h_b + 1, :]], ssm_s[p], L)
            ssm_s[p] = sp_new
            y = (o + xcp * vec_ref[0:1, sl]) * _silu(z_ref[:, O_Z + p * 2 * HALF:O_Z + (p + 1) * 2 * HALF])
            ys.append(y)
            ss = ss + jnp.sum(y * y, axis=-1, keepdims=True)
        scale = lax.rsqrt(ss / (DI_C // G_C) + RMS_EPS)
        for pp in range(per_g):
            p = g * per_g + pp
            sl = slice(p * 2 * HALF, (p + 1) * 2 * HALF)
            o_ref[:, sl] = ys[pp] * scale * vec_ref[1:2, sl]

    def shift_mix(p_raw, carry_row, mu):
        pv = jnp.where(valid, p_raw, 0.0)
        prev = jnp.where(r_idx == 0, carry_row, pltpu.roll(pv, 1, 0))
        return pv, pv + (prev - pv) * mu

    r_raw, r = shift_mix(z_ref[:, O_R:O_R + DI_D], shift_c[7:8, 0:DI_D], vec_ref[2:3, :])
    k_raw, k = shift_mix(z_ref[:, O_K:O_K + DI_D], shift_c[7:8, DI_D:2 * DI_D], vec_ref[3:4, :])
    v_raw, v = shift_mix(z_ref[:, O_V:O_V + DI_D], shift_c[7:8, 2 * DI_D:3 * DI_D], vec_ref[4:5, :])
    t_rawm, tm = shift_mix(t_raw, shift_c[7:8, 3 * DI_D:SHIFT_W], tp_ref[0:1, :])
    shift_c[:, 0:DI_D] = r_raw[L - 8:L]
    shift_c[:, DI_D:2 * DI_D] = k_raw[L - 8:L]
    shift_c[:, 2 * DI_D:3 * DI_D] = v_raw[L - 8:L]
    shift_c[:, 3 * DI_D:SHIFT_W] = t_rawm[L - 8:L]

    th_hi, th_lo = _parts(jnp.tanh(tm), 2)
    w2_hi, w2_lo = _parts(w2_ref[...], 2)
    w_pre = vec_ref[5:6, :] + (jnp.dot(th_hi, w2_hi, preferred_element_type=F32) +
                               jnp.dot(th_lo, w2_hi, preferred_element_type=F32) +
                               jnp.dot(th_hi, w2_lo, preferred_element_type=F32))
    logw = jnp.where(valid, -jnp.exp(-_softplus(-w_pre) - 0.5), 0.0)
    g_cum = _dot_sel(logw, tril, sel_first=True)
    a = _sigmoid(vec_ref[6:7, :] + _dot_bf(tm, a2_ref[...]))
    gate = _dot_bf(_sigmoid(tm), g2_ref[...])
    kkr = k * vec_ref[7:8, :]
    k2 = k * (1.0 + (a - 1.0) * vec_ref[8:9, :])
    rk = r * k2 * vec_ref[9:10, :]
    for p in range(npair):
        sl = slice(p * 2 * HALF, (p + 1) * 2 * HALF)
        kk = kkr[:, sl]
        nrm = jnp.sqrt(_dot_sel(kk * kk, ones_blk, n=2))
        al = jnp.where(valid, kk / jnp.maximum(nrm, 1e-12), 0.0)
        be = al * a[:, sl]
        y, bd_new = _rwkv_pair(r[:, sl], jnp.where(valid, k2[:, sl], 0.0), v[:, sl], al, be, logw[:, sl], g_cum[:, sl],
                               wkv_s[p], L)
        wkv_s[p] = bd_new
        mu = _dot_sel(y, ones_blk, n=2) * (1.0 / P_D)
        yc = y - mu
        var = _dot_sel(yc * yc, ones_blk, n=2) * (1.0 / P_D)
        yn = yc * lax.rsqrt(var + RWKV_GN_EPS) * vec_ref[10:11, sl] + vec_ref[11:12, sl]
        bonus = _dot_sel(rk[:, sl], ones_blk, n=2) * v[:, sl]
        o_ref[:, DI_C + p * 2 * HALF:DI_C + (p + 1) * 2 * HALF] = (yn + bonus) * gate[:, sl]

    @pl.when(c == chunks - 1)
    def _():
        for p in range(npair):
            sp = ssm_s[p]
            ssmo_ref[0, 2 * p] = sp[:, 0:HALF]
            ssmo_ref[0, 2 * p + 1] = sp[:, HALF:2 * HALF]
            bd = wkv_s[p]
            wkvo_ref[0, 2 * p] = bd[0:HALF, 0:HALF]
            wkvo_ref[0, 2 * p + 1] = bd[HALF:2 * HALF, HALF:2 * HALF]
        convo_ref[0] = conv_c[...]
        shifto_ref[0] = shift_c[...]


def _odd_call(z, params, states, *, L, nvalid, nb, chunks, row_map, name):
    has_s0 = states is not None
    kern = functools.partial(_odd_kernel, L=L, nvalid=nvalid, chunks=chunks, has_s0=has_s0)
    npair = H_C // 2
    if z.ndim == 2:
        z_spec = pl.BlockSpec((L, ODD_PAD), lambda b, c: (b * chunks + c, 0))
        o_spec = pl.BlockSpec((L, ODD_MIX), lambda b, c: (b * chunks + c, 0))
        o_shape = jax.ShapeDtypeStruct((z.shape[0], ODD_MIX), F32)
    else:
        z_spec = pl.BlockSpec((None, L, ODD_PAD), lambda b, c: (b, 0, 0))
        o_spec = pl.BlockSpec((None, L, ODD_MIX), lambda b, c: (b, 0, 0))
        o_shape = jax.ShapeDtypeStruct((nb, L, ODD_MIX), F32)
    const2 = lambda b, c: (0, 0)
    per_b = lambda b, c: (b, 0, 0, 0)
    per_b3 = lambda b, c: (b, 0, 0)
    in_specs = [z_spec] + [pl.BlockSpec(p.shape, const2) for p in params]
    args = [z] + list(params)
    st_specs = [pl.BlockSpec((1, H_C, N_C, P_C), per_b), pl.BlockSpec((1, H_D, P_D, P_D), per_b),
                pl.BlockSpec((1, 8, CONV_DIM), per_b3), pl.BlockSpec((1, 8, SHIFT_W), per_b3)]
    if has_s0:
        in_specs += st_specs
        args += list(states)
    return pl.pallas_call(
        kern,
        out_shape=(o_shape,
                   jax.ShapeDtypeStruct((nb, H_C, N_C, P_C), F32), jax.ShapeDtypeStruct((nb, H_D, P_D, P_D), F32),
                   jax.ShapeDtypeStruct((nb, 8, CONV_DIM), F32), jax.ShapeDtypeStruct((nb, 8, SHIFT_W), F32)),
        grid=(nb, chunks),
        in_specs=in_specs,
        out_specs=tuple([o_spec] + st_specs),
        scratch_shapes=[pltpu.VMEM((npair, N_C, 2 * HALF), F32), pltpu.VMEM((npair, 2 * HALF, 2 * HALF), F32),
                        pltpu.VMEM((8, CONV_DIM), F32), pltpu.VMEM((8, SHIFT_W), F32)],
        compiler_params=pltpu.CompilerParams(dimension_semantics=("arbitrary", "arbitrary"),
                                             vmem_limit_bytes=VMEM_LIMIT),
        name=name,
    )(*args)


def odd_mixer_prompt(z, params, *, nb=BATCH, chunks=P_CHUNKS):
    return _odd_call(z, params, None, L=CHUNK, nvalid=CHUNK, nb=nb, chunks=chunks, row_map=None,
                     name="odd_mixer_prompt")


def odd_mixer_sample(z, params, states):
    return _odd_call(z, params, states, L=S_LEN, nvalid=DEC_SEQ, nb=z.shape[0], chunks=1, row_map=None,
                     name="odd_mixer_sample")


def _odd_params(odd_w_in, conv_w, conv_b, dt_bias, a_log, d_skip, ssm_norm_g, shift_mu, rwkv_w0, rwkv_w2, rwkv_a0,
                rwkv_a2, rwkv_g2, rwkv_k_k, rwkv_k_a, rwkv_r_k, lnx_g, lnx_b):
    o_dt = DI_C + CONV_DIM
    o_rw = o_dt + H_C
    w = jnp.concatenate([odd_w_in[:, :o_dt], odd_w_in[:, o_rw:], odd_w_in[:, o_dt:o_rw],
                         jnp.zeros((D_MODEL, ODD_PAD - ODD_IN), odd_w_in.dtype)], axis=1)
    vec = jnp.stack([jnp.repeat(d_skip, P_C), ssm_norm_g, shift_mu[:DI_D], shift_mu[DI_D:2 * DI_D],
                     shift_mu[2 * DI_D:3 * DI_D], rwkv_w0, rwkv_a0, rwkv_k_k, rwkv_k_a, rwkv_r_k.reshape(-1),
                     lnx_g, lnx_b] + [jnp.zeros((DI_D,), F32)] * 4)
    cpack = jnp.concatenate([conv_w, conv_b[None], jnp.zeros((3, CONV_DIM), F32)], axis=0)
    zt = jnp.zeros((T_W,), F32)
    tpack = jnp.stack([zt.at[:T_DT].set(shift_mu[3 * DI_D:]), zt.at[T_DT:T_DT + H_C].set(dt_bias),
                       zt.at[T_DT:T_DT + H_C].set(-jnp.exp(a_log.astype(F32)))] + [zt] * 5)
    zw = jnp.zeros((T_W, DI_D), F32)
    w2p = zw.at[:R_W].set(rwkv_w2)
    a2p = zw.at[R_W:R_W + R_A].set(rwkv_a2).astype(BF16)
    g2p = zw.at[R_W + R_A:T_DT].set(rwkv_g2).astype(BF16)
    return w.astype(BF16), (vec, cpack, tpack, w2p, a2p, g2p)


def _top16_desc(cur):
    vals = []
    for _ in range(PEER_TOPK):
        m = jnp.max(cur, axis=0, keepdims=True)
        vals.append(m)
        cur = jnp.where(cur == m, -jnp.inf, cur)
    return vals


def _peer_kernel(x_ref, xb_ref, wq_ref, sk_ref, u_ref, v_ref, g_ref, b_ref, o_ref, ob_ref,
                 s1_s, s2_s, e1_s, e2_s, tau_s, cand_s, acc_s, *, tm, te):
    e = pl.program_id(1)
    ne = pl.num_programs(1)
    nk = PEER_KEYS

    @pl.when(e == 0)
    def _route():
        q_t = lax.dot_general(wq_ref[...], xb_ref[...], NT, preferred_element_type=F32)
        for h in range(PEER_HEADS):
            tops = []
            for c in range(2):
                hc = 2 * h + c
                s = jnp.dot(sk_ref[hc], q_t[hc * nk:(hc + 1) * nk, :].astype(BF16), preferred_element_type=F32)
                (s1_s if c == 0 else s2_s)[h] = s
                tops.append(_top16_desc(s))
            t1, t2 = tops
            t2_all = jnp.concatenate(t2, axis=0)
            for a in range(PEER_TOPK):
                cand_s[a * PEER_TOPK:(a + 1) * PEER_TOPK, :] = t1[a] + t2_all
            best = _top16_desc(cand_s[...])
            mx = t1[0] + t2[0]
            z = jnp.zeros_like(mx)
            for m in best:
                z = z + jnp.exp(m - mx)
            tau_s[h:h + 1, :] = best[-1]
            e1_s[h] = jnp.exp(s1_s[h] - t1[0])
            e2_s[h] = jnp.exp(s2_s[h] - t2[0]) / z
        acc_s[...] = jnp.zeros_like(acc_s)

    h_t = lax.dot_general(u_ref[...], xb_ref[...], NT, preferred_element_type=F32)
    parts = []
    for j in range(te // nk):
        i1 = e * (te // nk) + j
        w = jnp.zeros((nk, tm), F32)
        for h in range(PEER_HEADS):
            c = s2_s[h] + s1_s[h, pl.ds(i1, 1), :]
            w = w + jnp.where(c >= tau_s[h:h + 1, :], e2_s[h] * e1_s[h, pl.ds(i1, 1), :], 0.0)
        hj = h_t[j * nk:(j + 1) * nk, :]
        act = 0.5 * hj * (1.0 + lax.erf(hj * (2.0 ** -0.5)))
        parts.append((w * act).astype(BF16))
    p_t = jnp.concatenate(parts, axis=0)
    acc_s[...] += lax.dot_general(p_t, v_ref[...], TN, preferred_element_type=F32)

    @pl.when(e == ne - 1)
    def _fin():
        y = _ln_rows(ALPHA * x_ref[...] + acc_s[...], g_ref[...], b_ref[...])
        o_ref[...] = y
        ob_ref[...] = y.astype(BF16)


def peer_ln(x, xb, wq_t, sk, u, v, g, b, *, tm=512, te=512):
    m, d = x.shape
    kern = functools.partial(_peer_kernel, tm=tm, te=te)
    const = dict(pipeline_mode=pl.Buffered(1))
    return pl.pallas_call(
        kern,
        out_shape=(jax.ShapeDtypeStruct((m, d), F32), jax.ShapeDtypeStruct((m, d), BF16)),
        grid=(m // tm, PEER_EXPERTS // te),
        in_specs=[pl.BlockSpec((tm, d), lambda i, e: (i, 0), **const),
                  pl.BlockSpec((tm, d), lambda i, e: (i, 0), **const),
                  pl.BlockSpec(wq_t.shape, lambda i, e: (0, 0), **const),
                  pl.BlockSpec(sk.shape, lambda i, e: (0, 0, 0), **const),
                  pl.BlockSpec((te, d), lambda i, e: (e, 0)),
                  pl.BlockSpec((te, d), lambda i, e: (e, 0)),
                  pl.BlockSpec((1, d), lambda i, e: (0, 0), **const),
                  pl.BlockSpec((1, d), lambda i, e: (0, 0), **const)],
        out_specs=(pl.BlockSpec((tm, d), lambda i, e: (i, 0)), pl.BlockSpec((tm, d), lambda i, e: (i, 0))),
        scratch_shapes=[pltpu.VMEM((PEER_HEADS, PEER_KEYS, tm), F32), pltpu.VMEM((PEER_HEADS, PEER_KEYS, tm), F32),
                        pltpu.VMEM((PEER_HEADS, PEER_KEYS, tm), F32), pltpu.VMEM((PEER_HEADS, PEER_KEYS, tm), F32),
                        pltpu.VMEM((PEER_HEADS, tm), F32), pltpu.VMEM((PEER_TOPK * PEER_TOPK, tm), F32),
                        pltpu.VMEM((tm, d), F32)],
        compiler_params=pltpu.CompilerParams(dimension_semantics=("arbitrary", "arbitrary"),
                                             vmem_limit_bytes=VMEM_LIMIT),
        name="peer_ln",
    )(x, xb, wq_t, sk, u, v, g.reshape(1, d), b.reshape(1, d))


def _sample_rows(z):
    f = z.shape[1]
    zs = lax.slice(z, (S_ROW0, 0), (S_ROW0 + DEC_BATCH * DEC_SEQ, f)).reshape(DEC_BATCH, DEC_SEQ, f)
    return jnp.pad(zs, ((0, 0), (0, S_LEN - DEC_SEQ), (0, 0)))


def _merge_rows(buf, sample_out):
    f = buf.shape[1]
    tail = jnp.concatenate([sample_out[:, :DEC_SEQ].reshape(DEC_BATCH * DEC_SEQ, f),
                            jnp.zeros((M_PAD - S_ROW0 - DEC_BATCH * DEC_SEQ, f), buf.dtype)], axis=0)
    return lax.dynamic_update_slice(buf, tail, (S_ROW0, 0))


def kernel(x_prompt, x_sample, state_hgrn, state_ret, state_ssm, state_conv, state_wkv, state_shift, meta_tokens, ln_g, ln_b, even_w_in, hgrn_lb_logits, hgrn_norm_g, even_w_out, odd_w_in, conv_w, conv_b, dt_bias, a_log, d_skip, ssm_norm_g, shift_mu, rwkv_w0, rwkv_w2, rwkv_a0, rwkv_a2, rwkv_g2, rwkv_k_k, rwkv_k_a, rwkv_r_k, lnx_g, lnx_b, odd_w_out, peer_w_query, peer_sub_keys, peer_u, peer_v):
    dt = x_prompt.dtype
    lead = jnp.concatenate([jnp.zeros((P_LEAD, D_MODEL), dt), meta_tokens.astype(dt)], axis=0)
    xp = jnp.concatenate([jnp.broadcast_to(lead[None], (BATCH, CHUNK, D_MODEL)), x_prompt], axis=1)
    x = jnp.concatenate([xp.reshape(S_ROW0, D_MODEL), x_sample.reshape(DEC_BATCH * DEC_SEQ, D_MODEL),
                         jnp.zeros((M_PAD - S_ROW0 - DEC_BATCH * DEC_SEQ, D_MODEL), dt)], axis=0)
    xb = x.astype(BF16)

    cos_p, sin_p = _rope_tables(jnp.arange(P_ROWS) - P_LEAD)
    cos_s, sin_s = _rope_tables(PAST_LEN + jnp.arange(S_LEN))
    lb_table = jnp.cumsum(jax.nn.softmax(hgrn_lb_logits.astype(F32), axis=0), axis=0)

    z = matmul(xb, even_w_in[0].astype(BF16), tm=TOK_TILE, tn=1024, name="even_in")
    lb = lb_table[0].reshape(1, -1)
    ng = hgrn_norm_g[0].reshape(1, -1)
    mix_p, hgrn_p, ret_p = even_mixer_prompt(z, lb, ng, cos_p, sin_p)
    mix_s, hgrn_s, ret_s = even_mixer_sample(_sample_rows(z), lb, ng, cos_s, sin_s, state_hgrn[0], state_ret[0])
    mix = _merge_rows(mix_p, mix_s)
    x, xb = proj_ln(x, mix, even_w_out[0].astype(BF16), ln_g[0, 0], ln_b[0, 0], tm=TOK_TILE, name="even_out")
    x, xb = peer_ln(x, xb, peer_w_query[0].T.astype(BF16),
                    peer_sub_keys[0].reshape(2 * PEER_HEADS, PEER_KEYS, PEER_QDIM // 2).astype(BF16),
                    peer_u[0].astype(BF16), peer_v[0].astype(BF16), ln_g[0, 1], ln_b[0, 1])

    w_in1, params = _odd_params(odd_w_in[0], conv_w[0], conv_b[0], dt_bias[0], a_log[0], d_skip[0], ssm_norm_g[0],
                                shift_mu[0], rwkv_w0[0], rwkv_w2[0], rwkv_a0[0], rwkv_a2[0], rwkv_g2[0], rwkv_k_k[0],
                                rwkv_k_a[0], rwkv_r_k[0], lnx_g[0], lnx_b[0])
    z = matmul(xb, w_in1, tm=TOK_TILE, tn=1024, name="odd_in")
    mix_p, ssm_p, wkv_p, conv_p, shift_p = odd_mixer_prompt(z, params)
    conv8 = jnp.pad(state_conv[0], ((0, 0), (8 - (CONV_W - 1), 0), (0, 0)))
    shift8 = jnp.pad(state_shift[0][:, None, :], ((0, 0), (7, 0), (0, SHIFT_W - SHIFT_DIM)))
    mix_s, ssm_s, wkv_s, conv_s, shift_s = odd_mixer_sample(_sample_rows(z), params,
                                                            (state_ssm[0], state_wkv[0], conv8, shift8))
    mix = _merge_rows(mix_p, mix_s)
    x, xb = proj_ln(x, mix, odd_w_out[0].astype(BF16), ln_g[1, 0], ln_b[1, 0], tm=TOK_TILE, name="odd_out")
    x, xb = peer_ln(x, xb, peer_w_query[1].T.astype(BF16),
                    peer_sub_keys[1].reshape(2 * PEER_HEADS, PEER_KEYS, PEER_QDIM // 2).astype(BF16),
                    peer_u[1].astype(BF16), peer_v[1].astype(BF16), ln_g[1, 1], ln_b[1, 1])

    y_prompt = x[:S_ROW0].reshape(BATCH, P_ROWS, D_MODEL)[:, CHUNK:]
    y_sample = x[S_ROW0:S_ROW0 + DEC_BATCH * DEC_SEQ].reshape(DEC_BATCH, DEC_SEQ, D_MODEL)
    nc = CONV_W - 1
    return (y_prompt, y_sample, hgrn_p[None], hgrn_s[None], ret_p[None], ret_s[None], ssm_p[None], ssm_s[None],
            conv_p[None, :, 8 - nc:], conv_s[None, :, DEC_SEQ - nc:DEC_SEQ], wkv_p[None], wkv_s[None],
            shift_p[None, :, 7, :SHIFT_DIM], shift_s[None, :, DEC_SEQ - 1, :SHIFT_DIM])
```

```python
import functools
import math

import jax
import jax.numpy as jnp
from jax import lax
from jax.experimental import pallas as pl
from jax.experimental.pallas import tpu as pltpu

D_MODEL = 2048
BATCH = 4
SEQ = 2048
DEPTH = 2
DEC_BATCH = 128
DEC_SEQ = 4
PAST_LEN = 16384
N_META = 16
CHUNK = 64

H_A, DK_A, DV_A = 8, 128, 128
H_B, DK_B, DV_B = 4, 128, 256
EVEN_IN = 4 * H_A * DK_A + 2 * H_B * DK_B + 2 * H_B * DV_B
EVEN_MIX = H_A * DV_A + H_B * DV_B

H_C, P_C, N_C, G_C, CONV_W = 16, 64, 128, 2, 4
DI_C = H_C * P_C
CONV_DIM = DI_C + 2 * G_C * N_C
H_D, P_D = 16, 64
DI_D = H_D * P_D
R_W, R_A, R_G = 64, 64, 160
SHIFT_DIM = 3 * DI_D + R_W + R_A + R_G
ODD_IN = DI_C + CONV_DIM + H_C + SHIFT_DIM
ODD_MIX = DI_C + DI_D

PEER_KEYS = 128
PEER_EXPERTS = PEER_KEYS * PEER_KEYS
PEER_HEADS = 8
PEER_TOPK = 16
PEER_QDIM = 256

ALPHA = (2.0 * DEPTH) ** 0.25
LN_EPS = 1e-5
RMS_EPS = 1e-6
RWKV_GN_EPS = 64e-5
ROPE_BASE = 10000.0

F32 = jnp.float32
BF16 = jnp.bfloat16
HIGHEST = lax.Precision.HIGHEST

P_LEAD = CHUNK - N_META
P_ROWS = CHUNK + SEQ
P_CHUNKS = P_ROWS // CHUNK
S_ROW0 = BATCH * P_ROWS
S_LEN = 8
TOK_TILE = 512
M_PAD = -(-(S_ROW0 + DEC_BATCH * DEC_SEQ) // 1024) * 1024

VMEM_LIMIT = 56 * 1024 * 1024


def _dot(a, b, dims=(((1,), (0,)), ((), ())), precision=None):
    return lax.dot_general(a, b, dims, precision=precision, preferred_element_type=F32)


def _dot_bf(a, b, dims=(((1,), (0,)), ((), ()))):
    return lax.dot_general(a.astype(BF16), b.astype(BF16), dims, preferred_element_type=F32)


def _parts(x, n):
    out, rem = [], x
    for i in range(n):
        p = rem.astype(BF16)
        out.append(p)
        if i + 1 < n:
            rem = rem - p.astype(F32)
    return out


def _dot_sel(x, sel, dims=(((1,), (0,)), ((), ())), n=3, sel_first=False):
    sel = sel.astype(BF16)
    acc = None
    for p in _parts(x, n):
        d = lax.dot_general(sel, p, dims, preferred_element_type=F32) if sel_first else \
            lax.dot_general(p, sel, dims, preferred_element_type=F32)
        acc = d if acc is None else acc + d
    return acc


NT = (((1,), (1,)), ((), ()))
TN = (((0,), (0,)), ((), ()))


def _sigmoid(x):
    return 1.0 / (1.0 + jnp.exp(-x))


def _silu(x):
    return x * _sigmoid(x)


def _mm_kernel(x_ref, w_ref, o_ref):
    o_ref[...] = jnp.dot(x_ref[...].astype(BF16), w_ref[...], preferred_element_type=F32).astype(o_ref.dtype)


def matmul(x, w, *, tm, tn, out_dtype=F32, name="matmul"):
    m, k = x.shape
    n = w.shape[1]
    return pl.pallas_call(
        _mm_kernel,
        out_shape=jax.ShapeDtypeStruct((m, n), out_dtype),
        grid=(n // tn, m // tm),
        in_specs=[pl.BlockSpec((tm, k), lambda j, i: (i, 0)),
                  pl.BlockSpec((k, tn), lambda j, i: (0, j))],
        out_specs=pl.BlockSpec((tm, tn), lambda j, i: (i, j)),
        compiler_params=pltpu.CompilerParams(dimension_semantics=("arbitrary", "arbitrary"),
                                             vmem_limit_bytes=VMEM_LIMIT),
        name=name,
    )(x, w)


def _ln_rows(v, g, b):
    mu = jnp.mean(v, -1, keepdims=True)
    c = v - mu
    var = jnp.mean(c * c, -1, keepdims=True)
    return c * lax.rsqrt(var + LN_EPS) * g + b


def _proj_ln_kernel(x_ref, m_ref, w_ref, g_ref, b_ref, o_ref, ob_ref):
    acc = jnp.dot(m_ref[...].astype(BF16), w_ref[...], preferred_element_type=F32)
    y = _ln_rows(ALPHA * x_ref[...] + acc, g_ref[...], b_ref[...])
    o_ref[...] = y
    ob_ref[...] = y.astype(BF16)


def proj_ln(x, mix, w, g, b, *, tm, name="proj_ln"):
    m, d = x.shape
    k = mix.shape[1]
    return pl.pallas_call(
        _proj_ln_kernel,
        out_shape=(jax.ShapeDtypeStruct((m, d), F32), jax.ShapeDtypeStruct((m, d), BF16)),
        grid=(m // tm,),
        in_specs=[pl.BlockSpec((tm, d), lambda i: (i, 0)),
                  pl.BlockSpec((tm, k), lambda i: (i, 0)),
                  pl.BlockSpec((k, d), lambda i: (0, 0)),
                  pl.BlockSpec((1, d), lambda i: (0, 0)),
                  pl.BlockSpec((1, d), lambda i: (0, 0))],
        out_specs=(pl.BlockSpec((tm, d), lambda i: (i, 0)), pl.BlockSpec((tm, d), lambda i: (i, 0))),
        compiler_params=pltpu.CompilerParams(dimension_semantics=("arbitrary",),
                                             vmem_limit_bytes=VMEM_LIMIT),
        name=name,
    )(x, mix, w, g.reshape(1, d), b.reshape(1, d))


def _row_info(L, lead, nvalid):
    r = lax.broadcasted_iota(jnp.int32, (L, 1), 0)
    return r, (r >= lead) & (r < lead + nvalid)


def _tril(L, strict=False):
    r = lax.broadcasted_iota(jnp.int32, (L, L), 0)
    c = lax.broadcasted_iota(jnp.int32, (L, L), 1)
    return (c < r) if strict else (c <= r)


def _hgrn_head(q, k, v, logf, s0, L):
    tril = _tril(L).astype(F32)
    b = _dot_sel(logf, tril, sel_first=True)
    ones = jnp.ones((L, DK_A), F32)
    b_end_col = _dot_sel(logf, ones, TN)
    b_end_row = b[L - 1:L, :]
    o = _dot_bf(q * jnp.exp(b), s0)
    s_new = jnp.exp(b_end_col) * s0 + _dot_bf(k * jnp.exp(b_end_row - b), v, TN)
    sub = min(16, L)
    outs = []
    for i in range(L // sub):
        r0 = i * sub
        qs, bs = q[r0:r0 + sub], b[r0:r0 + sub]
        acc = o[r0:r0 + sub]
        if i > 0:
            ref = b[r0 - 1:r0, :]
            a = _dot_bf(qs * jnp.exp(bs - ref), k[0:r0] * jnp.exp(ref - b[0:r0]), NT)
            acc = acc + _dot_bf(a, v[0:r0])
        t_idx = lax.broadcasted_iota(jnp.int32, (sub, 1), 0)
        for s in range(sub):
            rs = r0 + s
            d = jnp.where(t_idx >= s, bs - b[rs:rs + 1, :], -jnp.inf)
            a_col = jnp.sum(qs * k[rs:rs + 1, :] * jnp.exp(d), axis=-1, keepdims=True)
            acc = acc + a_col * v[rs:rs + 1, :]
        outs.append(acc)
    return jnp.concatenate(outs, axis=0) if len(outs) > 1 else outs[0], s_new


def _ret_head(q, k, v, cnt_col, cnt_row, log_gamma, s0, L):
    causal = _tril(L)
    seg = jnp.where(causal, jnp.exp((cnt_col - cnt_row) * log_gamma), 0.0)
    scores = _dot_bf(q, k, NT) * seg
    o = _dot_bf(scores, v) + _dot_bf(q * jnp.exp(cnt_col * log_gamma), s0)
    cnt_end = cnt_col[L - 1:L, :]
    s_new = jnp.exp(cnt_end * log_gamma) * s0 + _dot_bf(k * jnp.exp((cnt_end - cnt_col) * log_gamma), v, TN)
    return o, s_new


def _even_kernel(*refs, L, nvalid, chunks, has_s0):
    if has_s0:
        z_ref, lb_ref, ng_ref, cos_ref, sin_ref, sh0_ref, sr0_ref, o_ref, sho_ref, sro_ref, sh_s, sr_s = refs
    else:
        z_ref, lb_ref, ng_ref, cos_ref, sin_ref, o_ref, sho_ref, sro_ref, sh_s, sr_s = refs
    c = pl.program_id(1)

    @pl.when(c == 0)
    def _():
        if has_s0:
            sh_s[...] = sh0_ref[0]
            sr_s[...] = sr0_ref[0]
        else:
            sh_s[...] = jnp.zeros_like(sh_s)
            sr_s[...] = jnp.zeros_like(sr_s)

    lead = jnp.where(c == 0, P_LEAD, 0) if chunks > 1 else 0
    r, valid = _row_info(L, lead, nvalid)
    cnt_col = jnp.clip(r + 1 - lead, 0, nvalid).astype(F32)
    rr = lax.broadcasted_iota(jnp.int32, (1, L), 1)
    cnt_row = jnp.clip(rr + 1 - lead, 0, nvalid).astype(F32)

    oa, ob = H_A * DK_A, 0
    off_q, off_f, off_i, off_g = 0, H_A * DK_A, 2 * H_A * DK_A, 2 * H_A * DK_A + H_A * DV_A
    base_b = 2 * H_A * DK_A + 2 * H_A * DV_A
    off_bq, off_bk, off_bv = base_b, base_b + H_B * DK_B, base_b + 2 * H_B * DK_B
    off_bg = off_bv + H_B * DV_B
    ng = ng_ref[...]
    for h in range(H_A):
        sl = slice(h * DK_A, (h + 1) * DK_A)
        aq = z_ref[:, off_q + h * DK_A: off_q + (h + 1) * DK_A]
        af = z_ref[:, off_f + h * DK_A: off_f + (h + 1) * DK_A]
        ai = z_ref[:, off_i + h * DV_A: off_i + (h + 1) * DV_A]
        ag = z_ref[:, off_g + h * DV_A: off_g + (h + 1) * DV_A]
        lb = lb_ref[:, sl]
        logf = jnp.where(valid, jnp.log(lb + (1.0 - lb) * _sigmoid(af)), 0.0)
        k = jnp.where(valid, (1.0 - lb) * _sigmoid(-af), 0.0)
        o, s_new = _hgrn_head(_silu(aq), k, ai, logf, sh_s[h], L)
        sh_s[h] = s_new
        o = o * lax.rsqrt(jnp.mean(o * o, -1, keepdims=True) + RMS_EPS) * ng
        o_ref[:, h * DV_A:(h + 1) * DV_A] = o * _silu(ag)
    cosv, sinv = cos_ref[...], sin_ref[...]
    for h in range(H_B):
        bq = z_ref[:, off_bq + h * DK_B: off_bq + (h + 1) * DK_B]
        bk = z_ref[:, off_bk + h * DK_B: off_bk + (h + 1) * DK_B]
        bv = z_ref[:, off_bv + h * DV_B: off_bv + (h + 1) * DV_B]
        bg = z_ref[:, off_bg + h * DV_B: off_bg + (h + 1) * DV_B]
        q = bq * cosv + pltpu.roll(bq, DK_B // 2, 1) * sinv
        k = (bk * cosv + pltpu.roll(bk, DK_B // 2, 1) * sinv) * (DK_B ** -0.5)
        k = jnp.where(valid, k, 0.0)
        log_gamma = math.log(1.0 - 2.0 ** (-5.0 - h))
        o, s_new = _ret_head(q, k, bv, cnt_col, cnt_row, log_gamma, sr_s[h], L)
        sr_s[h] = s_new
        mu = jnp.mean(o, -1, keepdims=True)
        cc = o - mu
        o = cc * lax.rsqrt(jnp.mean(cc * cc, -1, keepdims=True) + LN_EPS)
        o_ref[:, H_A * DV_A + h * DV_B: H_A * DV_A + (h + 1) * DV_B] = o * _silu(bg)

    @pl.when(c == chunks - 1)
    def _():
        sho_ref[0] = sh_s[...]
        sro_ref[0] = sr_s[...]


def even_mixer_prompt(z, lb, ng, cos, sin, *, nb=BATCH, chunks=P_CHUNKS):
    kern = functools.partial(_even_kernel, L=CHUNK, nvalid=CHUNK, chunks=chunks, has_s0=False)
    return pl.pallas_call(
        kern,
        out_shape=(jax.ShapeDtypeStruct((z.shape[0], EVEN_MIX), F32),
                   jax.ShapeDtypeStruct((nb, H_A, DK_A, DV_A), F32),
                   jax.ShapeDtypeStruct((nb, H_B, DK_B, DV_B), F32)),
        grid=(nb, chunks),
        in_specs=[pl.BlockSpec((CHUNK, EVEN_IN), lambda b, c: (b * chunks + c, 0)),
                  pl.BlockSpec((1, H_A * DK_A), lambda b, c: (0, 0)),
                  pl.BlockSpec((1, DV_A), lambda b, c: (0, 0)),
                  pl.BlockSpec((CHUNK, DK_B), lambda b, c: (c, 0)),
                  pl.BlockSpec((CHUNK, DK_B), lambda b, c: (c, 0))],
        out_specs=(pl.BlockSpec((CHUNK, EVEN_MIX), lambda b, c: (b * chunks + c, 0)),
                   pl.BlockSpec((1, H_A, DK_A, DV_A), lambda b, c: (b, 0, 0, 0)),
                   pl.BlockSpec((1, H_B, DK_B, DV_B), lambda b, c: (b, 0, 0, 0))),
        scratch_shapes=[pltpu.VMEM((H_A, DK_A, DV_A), F32), pltpu.VMEM((H_B, DK_B, DV_B), F32)],
        compiler_params=pltpu.CompilerParams(dimension_semantics=("arbitrary", "arbitrary"),
                                             vmem_limit_bytes=VMEM_LIMIT),
        name="even_mixer_prompt",
    )(z, lb, ng, cos, sin)


def even_mixer_sample(z, lb, ng, cos, sin, s_hgrn, s_ret):
    nb = z.shape[0]
    kern = functools.partial(_even_kernel, L=S_LEN, nvalid=DEC_SEQ, chunks=1, has_s0=True)
    return pl.pallas_call(
        kern,
        out_shape=(jax.ShapeDtypeStruct((nb, S_LEN, EVEN_MIX), F32),
                   jax.ShapeDtypeStruct((nb, H_A, DK_A, DV_A), F32),
                   jax.ShapeDtypeStruct((nb, H_B, DK_B, DV_B), F32)),
        grid=(nb, 1),
        in_specs=[pl.BlockSpec((None, S_LEN, EVEN_IN), lambda b, c: (b, 0, 0)),
                  pl.BlockSpec((1, H_A * DK_A), lambda b, c: (0, 0)),
                  pl.BlockSpec((1, DV_A), lambda b, c: (0, 0)),
                  pl.BlockSpec((S_LEN, DK_B), lambda b, c: (0, 0)),
                  pl.BlockSpec((S_LEN, DK_B), lambda b, c: (0, 0)),
                  pl.BlockSpec((1, H_A, DK_A, DV_A), lambda b, c: (b, 0, 0, 0)),
                  pl.BlockSpec((1, H_B, DK_B, DV_B), lambda b, c: (b, 0, 0, 0))],
        out_specs=(pl.BlockSpec((None, S_LEN, EVEN_MIX), lambda b, c: (b, 0, 0)),
                   pl.BlockSpec((1, H_A, DK_A, DV_A), lambda b, c: (b, 0, 0, 0)),
                   pl.BlockSpec((1, H_B, DK_B, DV_B), lambda b, c: (b, 0, 0, 0))),
        scratch_shapes=[pltpu.VMEM((H_A, DK_A, DV_A), F32), pltpu.VMEM((H_B, DK_B, DV_B), F32)],
        compiler_params=pltpu.CompilerParams(dimension_semantics=("arbitrary", "arbitrary"),
                                             vmem_limit_bytes=VMEM_LIMIT),
        name="even_mixer_sample",
    )(z, lb, ng, cos, sin, s_hgrn, s_ret)


def _rope_tables(pos):
    half = DK_B // 2
    inv = ROPE_BASE ** (-jnp.arange(half, dtype=F32) / half)
    ang = pos.astype(F32)[:, None] * inv
    cos, sin = jnp.cos(ang), jnp.sin(ang)
    return jnp.concatenate([cos, cos], -1), jnp.concatenate([-sin, sin], -1)


ODD_PAD = 6144
O_Z, O_X, O_R, O_K, O_V, O_T = 0, DI_C, DI_C + CONV_DIM, DI_C + CONV_DIM + DI_D, DI_C + CONV_DIM + 2 * DI_D, \
    DI_C + CONV_DIM + 3 * DI_D
T_W = 384
T_DT = R_W + R_A + R_G
SHIFT_W = 3 * DI_D + T_W
HALF = 64


def _softplus(x):
    return jnp.maximum(x, 0.0) + jnp.log1p(jnp.exp(-jnp.abs(x)))


def _lane_lt(n, width):
    return lax.broadcasted_iota(jnp.int32, (1, width), 1) < n


def _pair_ones():
    r = lax.broadcasted_iota(jnp.int32, (2 * HALF, 2 * HALF), 0) < HALF
    c = lax.broadcasted_iota(jnp.int32, (2 * HALF, 2 * HALF), 1) < HALF
    return r == c


def _stack_pair(x, m_a):
    return jnp.concatenate([jnp.where(m_a, x, 0.0), jnp.where(m_a, 0.0, x)], axis=0)


def _rwkv_pair(r, k, v, al, be, logw, G, bd, L):
    ys, bds = _rwkv_pairs([r], [k], [v], [al], [be], [logw], [G], [bd], L)
    return ys[0], bds[0]


def _rwkv_pairs(rs, ks_, vs_, als, bes, logws, Gs, bds, L):
    n = range(len(rs))
    m_a = _lane_lt(HALF, 2 * HALF)
    m_l = _lane_lt(L, 2 * L)
    ri = lax.broadcasted_iota(jnp.int32, (2 * L, 2 * L), 0)
    ci = lax.broadcasted_iota(jnp.int32, (2 * L, 2 * L), 1)
    cm = jnp.where(ci >= L, ci - L, ci)
    keep = cm < jnp.where(ri >= L, ri - L + 1, ri)
    eye = (ri == ci).astype(F32)
    blk4 = (ri // 4) == (ci // 4)
    pair_blk = _pair_ones()

    e_inv = [jnp.exp(-Gs[i]) for i in n]
    lhs = [jnp.concatenate([als[i] * jnp.exp(Gs[i] - logws[i]), rs[i] * jnp.exp(Gs[i])], axis=0).astype(BF16) for i in n]
    ks = [_stack_pair(ks_[i] * e_inv[i], m_a).astype(BF16) for i in n]
    bs = [_stack_pair(bes[i] * e_inv[i], m_a).astype(BF16) for i in n]
    vs = [_stack_pair(vs_[i], m_a).astype(BF16) for i in n]
    a_k = [jnp.where(keep, _dot_bf(lhs[i], ks[i], NT), 0.0) for i in n]
    a_b = [jnp.where(keep, _dot_bf(lhs[i], bs[i], NT), 0.0) for i in n]
    sb = [_dot_bf(lhs[i], bds[i], NT) for i in n]
    base = [sb[i] + _dot_bf(a_k[i], vs[i]) for i in n]
    n_bd = [jnp.concatenate([jnp.where(m_l, a_b[i][0:L], 0.0), jnp.where(m_l, 0.0, a_b[i][0:L])], axis=0) for i in n]
    n4 = [jnp.where(blk4, n_bd[i], 0.0) for i in n]
    n4sq = [_dot_bf(n4[i], n4[i]) for i in n]
    t_inv = [eye - n4[i] for i in n]
    t_inv = [t_inv[i] + _dot_bf(t_inv[i], n4sq[i]) for i in n]
    blk = 4
    while blk < L:
        off = ((ri // (2 * blk)) == (ci // (2 * blk))) & ((ri % (2 * blk)) >= blk) & ((ci % (2 * blk)) < blk)
        ct = [_dot_bf(jnp.where(off, n_bd[i], 0.0), t_inv[i]) for i in n]
        t_inv = [t_inv[i] - _dot_bf(t_inv[i], ct[i]) for i in n]
        blk *= 2
    u = [_dot_bf(t_inv[i][0:L, :] + t_inv[i][L:2 * L, :], _stack_pair(base[i][0:L], m_a)) for i in n]
    y = [base[i][L:2 * L] - _dot_bf(a_b[i][L:2 * L], _stack_pair(u[i], m_a)) for i in n]
    gl = [Gs[i][L - 1:L, :] for i in n]
    dec = [jnp.exp(gl[i] - Gs[i]) for i in n]
    upd = [_dot_bf(jnp.concatenate([vs_[i], u[i]], axis=0),
                   jnp.concatenate([ks_[i] * dec[i], -(bes[i] * dec[i])], axis=0), TN) for i in n]
    bd_new = [bds[i] * jnp.exp(gl[i]) + jnp.where(pair_blk, upd[i], 0.0) for i in n]
    return y, bd_new


def _ssd_pair(xdt, cg, bg, cb, bcs, brows, sp, L):
    m_a = _lane_lt(HALF, 2 * HALF)
    causal = _tril(L)
    out = jnp.zeros((L, 2 * HALF), F32)
    ends = [bc[L - 1:L, :] for bc in bcs]
    sp_new = sp * jnp.where(m_a, jnp.exp(ends[0]), jnp.exp(ends[1]))
    for x in range(2):
        keep = m_a if x == 0 else jnp.logical_not(m_a)
        seg = jnp.where(causal, jnp.exp(bcs[x] - brows[x]), 0.0)
        xm = jnp.where(keep, xdt, 0.0)
        out = out + _dot_bf(cb * seg, xm) + _dot_bf(cg * jnp.exp(bcs[x]), jnp.where(keep, sp, 0.0))
        sp_new = sp_new + _dot_bf(bg * jnp.exp(ends[x] - bcs[x]), xm, TN)
    return out, sp_new


def _odd_kernel(*refs, L, nvalid, chunks, has_s0):
    if has_s0:
        (z_ref, vec_ref, cp_ref, tp_ref, w2_ref, a2_ref, g2_ref, ssm0_ref, wkv0_ref, conv0_ref, shift0_ref,
         o_ref, ssmo_ref, wkvo_ref, convo_ref, shifto_ref, ssm_s, wkv_s, conv_c, shift_c) = refs
    else:
        (z_ref, vec_ref, cp_ref, tp_ref, w2_ref, a2_ref, g2_ref,
         o_ref, ssmo_ref, wkvo_ref, convo_ref, shifto_ref, ssm_s, wkv_s, conv_c, shift_c) = refs
    c = pl.program_id(1)
    npair = H_C // 2

    @pl.when(c == 0)
    def _():
        if has_s0:
            zpad = jnp.zeros((HALF, HALF), F32)
            for p in range(npair):
                ssm_s[p] = jnp.concatenate([ssm0_ref[0, 2 * p], ssm0_ref[0, 2 * p + 1]], axis=1)
                wkv_s[p] = jnp.concatenate([jnp.concatenate([wkv0_ref[0, 2 * p], zpad], axis=1),
                                            jnp.concatenate([zpad, wkv0_ref[0, 2 * p + 1]], axis=1)], axis=0)
            conv_c[...] = conv0_ref[0]
            shift_c[...] = shift0_ref[0]
        else:
            ssm_s[...] = jnp.zeros_like(ssm_s)
            wkv_s[...] = jnp.zeros_like(wkv_s)
            conv_c[...] = jnp.zeros_like(conv_c)
            shift_c[...] = jnp.zeros_like(shift_c)

    lead = jnp.where(c == 0, P_LEAD, 0) if chunks > 1 else 0
    r_idx, valid = _row_info(L, lead, nvalid)
    m_a = _lane_lt(HALF, 2 * HALF)
    tril = _tril(L).astype(F32)
    triu = (lax.broadcasted_iota(jnp.int32, (L, L), 0) <= lax.broadcasted_iota(jnp.int32, (L, L), 1)).astype(F32)
    ones_blk = _pair_ones().astype(F32)

    xbc = jnp.where(valid, z_ref[:, O_X:O_X + CONV_DIM], 0.0)
    c8 = conv_c[...]
    r8 = lax.broadcasted_iota(jnp.int32, (8, 1), 0)
    conv = cp_ref[CONV_W:CONV_W + 1, :] + xbc * cp_ref[CONV_W - 1:CONV_W, :]
    for j in range(1, CONV_W):
        rolled = pltpu.roll(xbc, j, 0)
        head = jnp.where(r8 < j, pltpu.roll(c8, j, 0), rolled[0:8])
        sh = jnp.concatenate([head, rolled[8:]], axis=0) if L > 8 else head
        conv = conv + sh * cp_ref[CONV_W - 1 - j:CONV_W - j, :]
    conv_c[...] = xbc[L - 8:L]
    act = _silu(conv)
    xc, bm, cm = act[:, :DI_C], act[:, DI_C:DI_C + G_C * N_C], act[:, DI_C + G_C * N_C:]

    t_raw = z_ref[:, O_T:O_T + T_W]
    dt = jnp.where(valid, _softplus(t_raw + tp_ref[1:2, :])[:, T_DT:T_DT + H_C], 0.0)
    logf = dt * tp_ref[2:3, T_DT:T_DT + H_C]
    bc_all = _dot_sel(logf, tril, sel_first=True)
    brow_all = _dot_sel(logf, triu, TN)
    per_g = npair // G_C
    for g in range(G_C):
        cg = cm[:, g * N_C:(g + 1) * N_C]
        bg = bm[:, g * N_C:(g + 1) * N_C]
        cb = _dot_bf(cg, bg, NT)
        ys, ss = [], jnp.zeros((L, 1), F32)
        for pp in range(per_g):
            p = g * per_g + pp
            sl = slice(p * 2 * HALF, (p + 1) * 2 * HALF)
            h_a, h_b = 2 * p, 2 * p + 1
            xcp = xc[:, sl]
            xdt = xcp * jnp.where(m_a, dt[:, h_a:h_a + 1], dt[:, h_b:h_b + 1])
            o, sp_new = _ssd_pair(xdt, cg, bg, cb, [bc_all[:, h_a:h_a + 1], bc_all[:, h_b:h_b + 1]],
                                  [brow_all[h_a:h_a + 1, :], brow_all[h_b:h_b + 1, :]], ssm_s[p], L)
            ssm_s[p] = sp_new
            y = (o + xcp * vec_ref[0:1, sl]) * _silu(z_ref[:, O_Z + p * 2 * HALF:O_Z + (p + 1) * 2 * HALF])
            ys.append(y)
            ss = ss + jnp.sum(y * y, axis=-1, keepdims=True)
        scale = lax.rsqrt(ss / (DI_C // G_C) + RMS_EPS)
        for pp in range(per_g):
            p = g * per_g + pp
            sl = slice(p * 2 * HALF, (p + 1) * 2 * HALF)
            o_ref[:, sl] = ys[pp] * scale * vec_ref[1:2, sl]

    def shift_mix(p_raw, carry_row, mu):
        pv = jnp.where(valid, p_raw, 0.0)
        prev = jnp.where(r_idx == 0, carry_row, pltpu.roll(pv, 1, 0))
        return pv, pv + (prev - pv) * mu

    r_raw, r = shift_mix(z_ref[:, O_R:O_R + DI_D], shift_c[7:8, 0:DI_D], vec_ref[2:3, :])
    k_raw, k = shift_mix(z_ref[:, O_K:O_K + DI_D], shift_c[7:8, DI_D:2 * DI_D], vec_ref[3:4, :])
    v_raw, v = shift_mix(z_ref[:, O_V:O_V + DI_D], shift_c[7:8, 2 * DI_D:3 * DI_D], vec_ref[4:5, :])
    t_rawm, tm = shift_mix(t_raw, shift_c[7:8, 3 * DI_D:SHIFT_W], tp_ref[0:1, :])
    shift_c[:, 0:DI_D] = r_raw[L - 8:L]
    shift_c[:, DI_D:2 * DI_D] = k_raw[L - 8:L]
    shift_c[:, 2 * DI_D:3 * DI_D] = v_raw[L - 8:L]
    shift_c[:, 3 * DI_D:SHIFT_W] = t_rawm[L - 8:L]

    th_hi, th_lo = _parts(jnp.tanh(tm), 2)
    w2_hi, w2_lo = _parts(w2_ref[...], 2)
    w_pre = vec_ref[5:6, :] + (jnp.dot(th_hi, w2_hi, preferred_element_type=F32) +
                               jnp.dot(th_lo, w2_hi, preferred_element_type=F32) +
                               jnp.dot(th_hi, w2_lo, preferred_element_type=F32))
    logw = jnp.where(valid, -jnp.exp(-_softplus(-w_pre) - 0.5), 0.0)
    g_cum = _dot_sel(logw, tril, sel_first=True)
    a = _sigmoid(vec_ref[6:7, :] + _dot_bf(tm, a2_ref[...]))
    gate = _dot_bf(_sigmoid(tm), g2_ref[...])
    kkr = k * vec_ref[7:8, :]
    k2 = k * (1.0 + (a - 1.0) * vec_ref[8:9, :])
    rk = r * k2 * vec_ref[9:10, :]
    pairs = range(npair)
    sls = [slice(p * 2 * HALF, (p + 1) * 2 * HALF) for p in pairs]
    nrm = [jnp.sqrt(_dot_sel(kkr[:, sl] * kkr[:, sl], ones_blk, n=2)) for sl in sls]
    al = [jnp.where(valid, kkr[:, sls[p]] / jnp.maximum(nrm[p], 1e-12), 0.0) for p in pairs]
    be = [al[p] * a[:, sls[p]] for p in pairs]
    ys, bd_new = _rwkv_pairs([r[:, sl] for sl in sls], [jnp.where(valid, k2[:, sl], 0.0) for sl in sls],
                             [v[:, sl] for sl in sls], al, be, [logw[:, sl] for sl in sls],
                             [g_cum[:, sl] for sl in sls], [wkv_s[p] for p in pairs], L)
    for p in pairs:
        wkv_s[p] = bd_new[p]
    mu = [_dot_sel(ys[p], ones_blk, n=2) * (1.0 / P_D) for p in pairs]
    yc = [ys[p] - mu[p] for p in pairs]
    var = [_dot_sel(yc[p] * yc[p], ones_blk, n=2) * (1.0 / P_D) for p in pairs]
    bonus = [_dot_sel(rk[:, sl], ones_blk, n=2) for sl in sls]
    for p in pairs:
        sl = sls[p]
        yn = yc[p] * lax.rsqrt(var[p] + RWKV_GN_EPS) * vec_ref[10:11, sl] + vec_ref[11:12, sl]
        o_ref[:, DI_C + p * 2 * HALF:DI_C + (p + 1) * 2 * HALF] = (yn + bonus[p] * v[:, sl]) * gate[:, sl]

    @pl.when(c == chunks - 1)
    def _():
        for p in range(npair):
            sp = ssm_s[p]
            ssmo_ref[0, 2 * p] = sp[:, 0:HALF]
            ssmo_ref[0, 2 * p + 1] = sp[:, HALF:2 * HALF]
            bd = wkv_s[p]
            wkvo_ref[0, 2 * p] = bd[0:HALF, 0:HALF]
            wkvo_ref[0, 2 * p + 1] = bd[HALF:2 * HALF, HALF:2 * HALF]
        convo_ref[0] = conv_c[...]
        shifto_ref[0] = shift_c[...]


def _odd_call(z, params, states, *, L, nvalid, nb, chunks, row_map, name):
    has_s0 = states is not None
    kern = functools.partial(_odd_kernel, L=L, nvalid=nvalid, chunks=chunks, has_s0=has_s0)
    npair = H_C // 2
    if z.ndim == 2:
        z_spec = pl.BlockSpec((L, ODD_PAD), lambda b, c: (b * chunks + c, 0))
        o_spec = pl.BlockSpec((L, ODD_MIX), lambda b, c: (b * chunks + c, 0))
        o_shape = jax.ShapeDtypeStruct((z.shape[0], ODD_MIX), F32)
    else:
        z_spec = pl.BlockSpec((None, L, ODD_PAD), lambda b, c: (b, 0, 0))
        o_spec = pl.BlockSpec((None, L, ODD_MIX), lambda b, c: (b, 0, 0))
        o_shape = jax.ShapeDtypeStruct((nb, L, ODD_MIX), F32)
    const2 = lambda b, c: (0, 0)
    per_b = lambda b, c: (b, 0, 0, 0)
    per_b3 = lambda b, c: (b, 0, 0)
    in_specs = [z_spec] + [pl.BlockSpec(p.shape, const2) for p in params]
    args = [z] + list(params)
    st_specs = [pl.BlockSpec((1, H_C, N_C, P_C), per_b), pl.BlockSpec((1, H_D, P_D, P_D), per_b),
                pl.BlockSpec((1, 8, CONV_DIM), per_b3), pl.BlockSpec((1, 8, SHIFT_W), per_b3)]
    if has_s0:
        in_specs += st_specs
        args += list(states)
    return pl.pallas_call(
        kern,
        out_shape=(o_shape,
                   jax.ShapeDtypeStruct((nb, H_C, N_C, P_C), F32), jax.ShapeDtypeStruct((nb, H_D, P_D, P_D), F32),
                   jax.ShapeDtypeStruct((nb, 8, CONV_DIM), F32), jax.ShapeDtypeStruct((nb, 8, SHIFT_W), F32)),
        grid=(nb, chunks),
        in_specs=in_specs,
        out_specs=tuple([o_spec] + st_specs),
        scratch_shapes=[pltpu.VMEM((npair, N_C, 2 * HALF), F32), pltpu.VMEM((npair, 2 * HALF, 2 * HALF), F32),
                        pltpu.VMEM((8, CONV_DIM), F32), pltpu.VMEM((8, SHIFT_W), F32)],
        compiler_params=pltpu.CompilerParams(dimension_semantics=("arbitrary", "arbitrary"),
                                             vmem_limit_bytes=VMEM_LIMIT),
        name=name,
    )(*args)


def odd_mixer_prompt(z, params, *, nb=BATCH, chunks=P_CHUNKS):
    return _odd_call(z, params, None, L=CHUNK, nvalid=CHUNK, nb=nb, chunks=chunks, row_map=None,
                     name="odd_mixer_prompt")


def odd_mixer_sample(z, params, states):
    return _odd_call(z, params, states, L=S_LEN, nvalid=DEC_SEQ, nb=z.shape[0], chunks=1, row_map=None,
                     name="odd_mixer_sample")


def _odd_params(odd_w_in, conv_w, conv_b, dt_bias, a_log, d_skip, ssm_norm_g, shift_mu, rwkv_w0, rwkv_w2, rwkv_a0,
                rwkv_a2, rwkv_g2, rwkv_k_k, rwkv_k_a, rwkv_r_k, lnx_g, lnx_b):
    o_dt = DI_C + CONV_DIM
    o_rw = o_dt + H_C
    w = jnp.concatenate([odd_w_in[:, :o_dt], odd_w_in[:, o_rw:], odd_w_in[:, o_dt:o_rw],
                         jnp.zeros((D_MODEL, ODD_PAD - ODD_IN), odd_w_in.dtype)], axis=1)
    vec = jnp.stack([jnp.repeat(d_skip, P_C), ssm_norm_g, shift_mu[:DI_D], shift_mu[DI_D:2 * DI_D],
                     shift_mu[2 * DI_D:3 * DI_D], rwkv_w0, rwkv_a0, rwkv_k_k, rwkv_k_a, rwkv_r_k.reshape(-1),
                     lnx_g, lnx_b] + [jnp.zeros((DI_D,), F32)] * 4)
    cpack = jnp.concatenate([conv_w, conv_b[None], jnp.zeros((3, CONV_DIM), F32)], axis=0)
    zt = jnp.zeros((T_W,), F32)
    tpack = jnp.stack([zt.at[:T_DT].set(shift_mu[3 * DI_D:]), zt.at[T_DT:T_DT + H_C].set(dt_bias),
                       zt.at[T_DT:T_DT + H_C].set(-jnp.exp(a_log.astype(F32)))] + [zt] * 5)
    zw = jnp.zeros((T_W, DI_D), F32)
    w2p = zw.at[:R_W].set(rwkv_w2)
    a2p = zw.at[R_W:R_W + R_A].set(rwkv_a2).astype(BF16)
    g2p = zw.at[R_W + R_A:T_DT].set(rwkv_g2).astype(BF16)
    return w.astype(BF16), (vec, cpack, tpack, w2p, a2p, g2p)


def _top16_desc(cur):
    vals = []
    for _ in range(PEER_TOPK):
        m = jnp.max(cur, axis=0, keepdims=True)
        vals.append(m)
        cur = jnp.where(cur == m, -jnp.inf, cur)
    return vals


def _peer_kernel(x_ref, xb_ref, wq_ref, sk_ref, u_ref, v_ref, g_ref, b_ref, o_ref, ob_ref,
                 s1_s, s2_s, e1_s, e2_s, tau_s, cand_s, acc_s, *, tm, te):
    e = pl.program_id(1)
    ne = pl.num_programs(1)
    nk = PEER_KEYS

    @pl.when(e == 0)
    def _route():
        q_t = lax.dot_general(wq_ref[...], xb_ref[...], NT, preferred_element_type=F32)
        for h in range(PEER_HEADS):
            tops = []
            for c in range(2):
                hc = 2 * h + c
                s = jnp.dot(sk_ref[hc], q_t[hc * nk:(hc + 1) * nk, :].astype(BF16), preferred_element_type=F32)
                (s1_s if c == 0 else s2_s)[h] = s
                tops.append(_top16_desc(s))
            t1, t2 = tops
            t2_all = jnp.concatenate(t2, axis=0)
            for a in range(PEER_TOPK):
                cand_s[a * PEER_TOPK:(a + 1) * PEER_TOPK, :] = t1[a] + t2_all
            best = _top16_desc(cand_s[...])
            mx = t1[0] + t2[0]
            z = jnp.zeros_like(mx)
            for m in best:
                z = z + jnp.exp(m - mx)
            tau_s[h:h + 1, :] = best[-1]
            e1_s[h] = jnp.exp(s1_s[h] - t1[0])
            e2_s[h] = jnp.exp(s2_s[h] - t2[0]) / z
        acc_s[...] = jnp.zeros_like(acc_s)

    h_t = lax.dot_general(u_ref[...], xb_ref[...], NT, preferred_element_type=F32)
    parts = []
    for j in range(te // nk):
        i1 = e * (te // nk) + j
        w = jnp.zeros((nk, tm), F32)
        for h in range(PEER_HEADS):
            c = s2_s[h] + s1_s[h, pl.ds(i1, 1), :]
            w = w + jnp.where(c >= tau_s[h:h + 1, :], e2_s[h] * e1_s[h, pl.ds(i1, 1), :], 0.0)
        hj = h_t[j * nk:(j + 1) * nk, :]
        act = 0.5 * hj * (1.0 + lax.erf(hj * (2.0 ** -0.5)))
        parts.append((w * act).astype(BF16))
    p_t = jnp.concatenate(parts, axis=0)
    acc_s[...] += lax.dot_general(p_t, v_ref[...], TN, preferred_element_type=F32)

    @pl.when(e == ne - 1)
    def _fin():
        y = _ln_rows(ALPHA * x_ref[...] + acc_s[...], g_ref[...], b_ref[...])
        o_ref[...] = y
        ob_ref[...] = y.astype(BF16)


def peer_ln(x, xb, wq_t, sk, u, v, g, b, *, tm=512, te=512):
    m, d = x.shape
    kern = functools.partial(_peer_kernel, tm=tm, te=te)
    const = dict(pipeline_mode=pl.Buffered(1))
    return pl.pallas_call(
        kern,
        out_shape=(jax.ShapeDtypeStruct((m, d), F32), jax.ShapeDtypeStruct((m, d), BF16)),
        grid=(m // tm, PEER_EXPERTS // te),
        in_specs=[pl.BlockSpec((tm, d), lambda i, e: (i, 0), **const),
                  pl.BlockSpec((tm, d), lambda i, e: (i, 0), **const),
                  pl.BlockSpec(wq_t.shape, lambda i, e: (0, 0), **const),
                  pl.BlockSpec(sk.shape, lambda i, e: (0, 0, 0), **const),
                  pl.BlockSpec((te, d), lambda i, e: (e, 0)),
                  pl.BlockSpec((te, d), lambda i, e: (e, 0)),
                  pl.BlockSpec((1, d), lambda i, e: (0, 0), **const),
                  pl.BlockSpec((1, d), lambda i, e: (0, 0), **const)],
        out_specs=(pl.BlockSpec((tm, d), lambda i, e: (i, 0)), pl.BlockSpec((tm, d), lambda i, e: (i, 0))),
        scratch_shapes=[pltpu.VMEM((PEER_HEADS, PEER_KEYS, tm), F32), pltpu.VMEM((PEER_HEADS, PEER_KEYS, tm), F32),
                        pltpu.VMEM((PEER_HEADS, PEER_KEYS, tm), F32), pltpu.VMEM((PEER_HEADS, PEER_KEYS, tm), F32),
                        pltpu.VMEM((PEER_HEADS, tm), F32), pltpu.VMEM((PEER_TOPK * PEER_TOPK, tm), F32),
                        pltpu.VMEM((tm, d), F32)],
        compiler_params=pltpu.CompilerParams(dimension_semantics=("arbitrary", "arbitrary"),
                                             vmem_limit_bytes=VMEM_LIMIT),
        name="peer_ln",
    )(x, xb, wq_t, sk, u, v, g.reshape(1, d), b.reshape(1, d))


def _sample_rows(z):
    f = z.shape[1]
    zs = lax.slice(z, (S_ROW0, 0), (S_ROW0 + DEC_BATCH * DEC_SEQ, f)).reshape(DEC_BATCH, DEC_SEQ, f)
    return jnp.pad(zs, ((0, 0), (0, S_LEN - DEC_SEQ), (0, 0)))


def _merge_rows(buf, sample_out):
    f = buf.shape[1]
    tail = jnp.concatenate([sample_out[:, :DEC_SEQ].reshape(DEC_BATCH * DEC_SEQ, f),
                            jnp.zeros((M_PAD - S_ROW0 - DEC_BATCH * DEC_SEQ, f), buf.dtype)], axis=0)
    return lax.dynamic_update_slice(buf, tail, (S_ROW0, 0))


def kernel(x_prompt, x_sample, state_hgrn, state_ret, state_ssm, state_conv, state_wkv, state_shift, meta_tokens, ln_g, ln_b, even_w_in, hgrn_lb_logits, hgrn_norm_g, even_w_out, odd_w_in, conv_w, conv_b, dt_bias, a_log, d_skip, ssm_norm_g, shift_mu, rwkv_w0, rwkv_w2, rwkv_a0, rwkv_a2, rwkv_g2, rwkv_k_k, rwkv_k_a, rwkv_r_k, lnx_g, lnx_b, odd_w_out, peer_w_query, peer_sub_keys, peer_u, peer_v):
    dt = x_prompt.dtype
    lead = jnp.concatenate([jnp.zeros((P_LEAD, D_MODEL), dt), meta_tokens.astype(dt)], axis=0)
    xp = jnp.concatenate([jnp.broadcast_to(lead[None], (BATCH, CHUNK, D_MODEL)), x_prompt], axis=1)
    x = jnp.concatenate([xp.reshape(S_ROW0, D_MODEL), x_sample.reshape(DEC_BATCH * DEC_SEQ, D_MODEL),
                         jnp.zeros((M_PAD - S_ROW0 - DEC_BATCH * DEC_SEQ, D_MODEL), dt)], axis=0)
    xb = x.astype(BF16)

    cos_p, sin_p = _rope_tables(jnp.arange(P_ROWS) - P_LEAD)
    cos_s, sin_s = _rope_tables(PAST_LEN + jnp.arange(S_LEN))
    lb_table = jnp.cumsum(jax.nn.softmax(hgrn_lb_logits.astype(F32), axis=0), axis=0)

    z = matmul(xb, even_w_in[0].astype(BF16), tm=TOK_TILE, tn=1024, name="even_in")
    lb = lb_table[0].reshape(1, -1)
    ng = hgrn_norm_g[0].reshape(1, -1)
    mix_p, hgrn_p, ret_p = even_mixer_prompt(z, lb, ng, cos_p, sin_p)
    mix_s, hgrn_s, ret_s = even_mixer_sample(_sample_rows(z), lb, ng, cos_s, sin_s, state_hgrn[0], state_ret[0])
    mix = _merge_rows(mix_p, mix_s)
    x, xb = proj_ln(x, mix, even_w_out[0].astype(BF16), ln_g[0, 0], ln_b[0, 0], tm=TOK_TILE, name="even_out")
    x, xb = peer_ln(x, xb, peer_w_query[0].T.astype(BF16),
                    peer_sub_keys[0].reshape(2 * PEER_HEADS, PEER_KEYS, PEER_QDIM // 2).astype(BF16),
                    peer_u[0].astype(BF16), peer_v[0].astype(BF16), ln_g[0, 1], ln_b[0, 1])

    w_in1, params = _odd_params(odd_w_in[0], conv_w[0], conv_b[0], dt_bias[0], a_log[0], d_skip[0], ssm_norm_g[0],
                                shift_mu[0], rwkv_w0[0], rwkv_w2[0], rwkv_a0[0], rwkv_a2[0], rwkv_g2[0], rwkv_k_k[0],
                                rwkv_k_a[0], rwkv_r_k[0], lnx_g[0], lnx_b[0])
    z = matmul(xb, w_in1, tm=TOK_TILE, tn=1024, name="odd_in")
    mix_p, ssm_p, wkv_p, conv_p, shift_p = odd_mixer_prompt(z, params)
    conv8 = jnp.pad(state_conv[0], ((0, 0), (8 - (CONV_W - 1), 0), (0, 0)))
    shift8 = jnp.pad(state_shift[0][:, None, :], ((0, 0), (7, 0), (0, SHIFT_W - SHIFT_DIM)))
    mix_s, ssm_s, wkv_s, conv_s, shift_s = odd_mixer_sample(_sample_rows(z), params,
                                                            (state_ssm[0], state_wkv[0], conv8, shift8))
    mix = _merge_rows(mix_p, mix_s)
    x, xb = proj_ln(x, mix, odd_w_out[0].astype(BF16), ln_g[1, 0], ln_b[1, 0], tm=TOK_TILE, name="odd_out")
    x, xb = peer_ln(x, xb, peer_w_query[1].T.astype(BF16),
                    peer_sub_keys[1].reshape(2 * PEER_HEADS, PEER_KEYS, PEER_QDIM // 2).astype(BF16),
                    peer_u[1].astype(BF16), peer_v[1].astype(BF16), ln_g[1, 1], ln_b[1, 1])

    y_prompt = x[:S_ROW0].reshape(BATCH, P_ROWS, D_MODEL)[:, CHUNK:]
    y_sample = x[S_ROW0:S_ROW0 + DEC_BATCH * DEC_SEQ].reshape(DEC_BATCH, DEC_SEQ, D_MODEL)
    nc = CONV_W - 1
    return (y_prompt, y_sample, hgrn_p[None], hgrn_s[None], ret_p[None], ret_s[None], ssm_p[None], ssm_s[None],
            conv_p[None, :, 8 - nc:], conv_s[None, :, DEC_SEQ - nc:DEC_SEQ], wkv_p[None], wkv_s[None],
            shift_p[None, :, 7, :SHIFT_DIM], shift_s[None, :, DEC_SEQ - 1, :SHIFT_DIM])
```

```python
import functools
import math

import jax
import jax.numpy as jnp
from jax import lax
from jax.experimental import pallas as pl
from jax.experimental.pallas import tpu as pltpu

D_MODEL = 2048
BATCH = 4
SEQ = 2048
DEPTH = 2
DEC_BATCH = 128
DEC_SEQ = 4
PAST_LEN = 16384
N_META = 16
CHUNK = 64

H_A, DK_A, DV_A = 8, 128, 128
H_B, DK_B, DV_B = 4, 128, 256
EVEN_IN = 4 * H_A * DK_A + 2 * H_B * DK_B + 2 * H_B * DV_B
EVEN_MIX = H_A * DV_A + H_B * DV_B

H_C, P_C, N_C, G_C, CONV_W = 16, 64, 128, 2, 4
DI_C = H_C * P_C
CONV_DIM = DI_C + 2 * G_C * N_C
H_D, P_D = 16, 64
DI_D = H_D * P_D
R_W, R_A, R_G = 64, 64, 160
SHIFT_DIM = 3 * DI_D + R_W + R_A + R_G
ODD_IN = DI_C + CONV_DIM + H_C + SHIFT_DIM
ODD_MIX = DI_C + DI_D

PEER_KEYS = 128
PEER_EXPERTS = PEER_KEYS * PEER_KEYS
PEER_HEADS = 8
PEER_TOPK = 16
PEER_QDIM = 256

ALPHA = (2.0 * DEPTH) ** 0.25
LN_EPS = 1e-5
RMS_EPS = 1e-6
RWKV_GN_EPS = 64e-5
ROPE_BASE = 10000.0

F32 = jnp.float32
BF16 = jnp.bfloat16
HIGHEST = lax.Precision.HIGHEST

P_LEAD = CHUNK - N_META
P_ROWS = CHUNK + SEQ
P_CHUNKS = P_ROWS // CHUNK
S_ROW0 = BATCH * P_ROWS
S_LEN = 8
TOK_TILE = 512
M_PAD = -(-(S_ROW0 + DEC_BATCH * DEC_SEQ) // 1024) * 1024

VMEM_LIMIT = 56 * 1024 * 1024


def _dot(a, b, dims=(((1,), (0,)), ((), ())), precision=None):
    return lax.dot_general(a, b, dims, precision=precision, preferred_element_type=F32)


def _dot_bf(a, b, dims=(((1,), (0,)), ((), ()))):
    return lax.dot_general(a.astype(BF16), b.astype(BF16), dims, preferred_element_type=F32)


def _parts(x, n):
    out, rem = [], x
    for i in range(n):
        p = rem.astype(BF16)
        out.append(p)
        if i + 1 < n:
            rem = rem - p.astype(F32)
    return out


def _dot_sel(x, sel, dims=(((1,), (0,)), ((), ())), n=3, sel_first=False):
    sel = sel.astype(BF16)
    acc = None
    for p in _parts(x, n):
        d = lax.dot_general(sel, p, dims, preferred_element_type=F32) if sel_first else \
            lax.dot_general(p, sel, dims, preferred_element_type=F32)
        acc = d if acc is None else acc + d
    return acc


NT = (((1,), (1,)), ((), ()))
TN = (((0,), (0,)), ((), ()))


def _sigmoid(x):
    return 1.0 / (1.0 + jnp.exp(-x))


def _silu(x):
    return x * _sigmoid(x)


def _mm_kernel(x_ref, w_ref, o_ref):
    o_ref[...] = jnp.dot(x_ref[...].astype(BF16), w_ref[...], preferred_element_type=F32).astype(o_ref.dtype)


def matmul(x, w, *, tm, tn, out_dtype=F32, name="matmul"):
    m, k = x.shape
    n = w.shape[1]
    return pl.pallas_call(
        _mm_kernel,
        out_shape=jax.ShapeDtypeStruct((m, n), out_dtype),
        grid=(n // tn, m // tm),
        in_specs=[pl.BlockSpec((tm, k), lambda j, i: (i, 0)),
                  pl.BlockSpec((k, tn), lambda j, i: (0, j))],
        out_specs=pl.BlockSpec((tm, tn), lambda j, i: (i, j)),
        compiler_params=pltpu.CompilerParams(dimension_semantics=("arbitrary", "arbitrary"),
                                             vmem_limit_bytes=VMEM_LIMIT),
        name=name,
    )(x, w)


def _ln_rows(v, g, b):
    mu = jnp.mean(v, -1, keepdims=True)
    c = v - mu
    var = jnp.mean(c * c, -1, keepdims=True)
    return c * lax.rsqrt(var + LN_EPS) * g + b


def _proj_ln_kernel(x_ref, m_ref, w_ref, g_ref, b_ref, o_ref, ob_ref):
    acc = jnp.dot(m_ref[...].astype(BF16), w_ref[...], preferred_element_type=F32)
    y = _ln_rows(ALPHA * x_ref[...] + acc, g_ref[...], b_ref[...])
    o_ref[...] = y
    ob_ref[...] = y.astype(BF16)


def proj_ln(x, mix, w, g, b, *, tm, name="proj_ln"):
    m, d = x.shape
    k = mix.shape[1]
    return pl.pallas_call(
        _proj_ln_kernel,
        out_shape=(jax.ShapeDtypeStruct((m, d), F32), jax.ShapeDtypeStruct((m, d), BF16)),
        grid=(m // tm,),
        in_specs=[pl.BlockSpec((tm, d), lambda i: (i, 0)),
                  pl.BlockSpec((tm, k), lambda i: (i, 0)),
                  pl.BlockSpec((k, d), lambda i: (0, 0)),
                  pl.BlockSpec((1, d), lambda i: (0, 0)),
                  pl.BlockSpec((1, d), lambda i: (0, 0))],
        out_specs=(pl.BlockSpec((tm, d), lambda i: (i, 0)), pl.BlockSpec((tm, d), lambda i: (i, 0))),
        compiler_params=pltpu.CompilerParams(dimension_semantics=("arbitrary",),
                                             vmem_limit_bytes=VMEM_LIMIT),
        name=name,
    )(x, mix, w, g.reshape(1, d), b.reshape(1, d))


def _row_info(L, lead, nvalid):
    r = lax.broadcasted_iota(jnp.int32, (L, 1), 0)
    return r, (r >= lead) & (r < lead + nvalid)


def _tril(L, strict=False):
    r = lax.broadcasted_iota(jnp.int32, (L, L), 0)
    c = lax.broadcasted_iota(jnp.int32, (L, L), 1)
    return (c < r) if strict else (c <= r)


def _hgrn_heads(qs, ks, vs, logfs, s0s, L):
    n = range(len(qs))
    tril = _tril(L).astype(F32)
    ones_l = jnp.ones((L, DK_A), F32)
    ones_k = jnp.ones((DK_A, DK_A), BF16)
    b = [_dot_sel(logfs[i], tril, sel_first=True) for i in n]
    b_end_col = [_dot_sel(logfs[i], ones_l, TN) for i in n]
    o = [_dot_bf(qs[i] * jnp.exp(b[i]), s0s[i]) for i in n]
    upd = [_dot_bf(ks[i] * jnp.exp(b[i][L - 1:L, :] - b[i]), vs[i], TN) for i in n]
    s_new = [jnp.exp(b_end_col[i]) * s0s[i] + upd[i] for i in n]
    sub = min(16, L)
    t_idx = lax.broadcasted_iota(jnp.int32, (sub, 1), 0)
    outs = [[] for _ in n]
    for blk in range(L // sub):
        r0 = blk * sub
        acc = [o[i][r0:r0 + sub] for i in n]
        if blk > 0:
            a = [_dot_bf(qs[i][r0:r0 + sub] * jnp.exp(b[i][r0:r0 + sub] - b[i][r0 - 1:r0, :]),
                         ks[i][0:r0] * jnp.exp(b[i][r0 - 1:r0, :] - b[i][0:r0]), NT) for i in n]
            acc = [acc[i] + _dot_bf(a[i], vs[i][0:r0]) for i in n]
        prod = [jnp.concatenate(
            [qs[i][r0:r0 + sub] * ks[i][r0 + s:r0 + s + 1, :] *
             jnp.exp(jnp.where(t_idx >= s, b[i][r0:r0 + sub] - b[i][r0 + s:r0 + s + 1, :], -jnp.inf))
             for s in range(sub)], axis=0).astype(BF16) for i in n]
        score = [jnp.dot(prod[i], ones_k, preferred_element_type=F32) for i in n]
        for i in n:
            t = acc[i]
            for s in range(sub):
                t = t + score[i][s * sub:(s + 1) * sub, :] * vs[i][r0 + s:r0 + s + 1, :]
            outs[i].append(t)
    return [jnp.concatenate(x, axis=0) if len(x) > 1 else x[0] for x in outs], s_new


def _even_kernel(*refs, L, nvalid, chunks, has_s0):
    if has_s0:
        z_ref, lb_ref, ng_ref, cos_ref, sin_ref, sh0_ref, sr0_ref, o_ref, sho_ref, sro_ref, sh_s, sr_s = refs
    else:
        z_ref, lb_ref, ng_ref, cos_ref, sin_ref, o_ref, sho_ref, sro_ref, sh_s, sr_s = refs
    c = pl.program_id(1)

    @pl.when(c == 0)
    def _():
        if has_s0:
            sh_s[...] = sh0_ref[0]
            sr_s[...] = sr0_ref[0]
        else:
            sh_s[...] = jnp.zeros_like(sh_s)
            sr_s[...] = jnp.zeros_like(sr_s)

    lead = jnp.where(c == 0, P_LEAD, 0) if chunks > 1 else 0
    r, valid = _row_info(L, lead, nvalid)
    cnt_col = jnp.clip(r + 1 - lead, 0, nvalid).astype(F32)
    rr = lax.broadcasted_iota(jnp.int32, (1, L), 1)
    cnt_row = jnp.clip(rr + 1 - lead, 0, nvalid).astype(F32)

    oa, ob = H_A * DK_A, 0
    off_q, off_f, off_i, off_g = 0, H_A * DK_A, 2 * H_A * DK_A, 2 * H_A * DK_A + H_A * DV_A
    base_b = 2 * H_A * DK_A + 2 * H_A * DV_A
    off_bq, off_bk, off_bv = base_b, base_b + H_B * DK_B, base_b + 2 * H_B * DK_B
    off_bg = off_bv + H_B * DV_B
    ng = ng_ref[...]
    ha, hb = range(H_A), range(H_B)
    af = [z_ref[:, off_f + h * DK_A: off_f + (h + 1) * DK_A] for h in ha]
    lbs = [lb_ref[:, h * DK_A:(h + 1) * DK_A] for h in ha]
    logf = [jnp.where(valid, jnp.log(lbs[h] + (1.0 - lbs[h]) * _sigmoid(af[h])), 0.0) for h in ha]
    ka = [jnp.where(valid, (1.0 - lbs[h]) * _sigmoid(-af[h]), 0.0) for h in ha]
    qa = [_silu(z_ref[:, off_q + h * DK_A: off_q + (h + 1) * DK_A]) for h in ha]
    va = [z_ref[:, off_i + h * DV_A: off_i + (h + 1) * DV_A] for h in ha]
    cosv, sinv = cos_ref[...], sin_ref[...]
    bq = [z_ref[:, off_bq + h * DK_B: off_bq + (h + 1) * DK_B] for h in hb]
    bk = [z_ref[:, off_bk + h * DK_B: off_bk + (h + 1) * DK_B] for h in hb]
    vb = [z_ref[:, off_bv + h * DV_B: off_bv + (h + 1) * DV_B] for h in hb]
    qb = [bq[h] * cosv + pltpu.roll(bq[h], DK_B // 2, 1) * sinv for h in hb]
    kb = [jnp.where(valid, (bk[h] * cosv + pltpu.roll(bk[h], DK_B // 2, 1) * sinv) * (DK_B ** -0.5), 0.0) for h in hb]
    lg = [math.log(1.0 - 2.0 ** (-5.0 - h)) for h in hb]

    oa, sa_new = _hgrn_heads(qa, ka, va, logf, [sh_s[h] for h in ha], L)
    causal = _tril(L)
    cnt_end = cnt_col[L - 1:L, :]
    scores = [_dot_bf(qb[h], kb[h], NT) * jnp.where(causal, jnp.exp((cnt_col - cnt_row) * lg[h]), 0.0) for h in hb]
    inter = [_dot_bf(qb[h] * jnp.exp(cnt_col * lg[h]), sr_s[h]) for h in hb]
    ob = [_dot_bf(scores[h], vb[h]) + inter[h] for h in hb]
    updb = [_dot_bf(kb[h] * jnp.exp((cnt_end - cnt_col) * lg[h]), vb[h], TN) for h in hb]
    for h in ha:
        sh_s[h] = sa_new[h]
        o = oa[h] * lax.rsqrt(jnp.mean(oa[h] * oa[h], -1, keepdims=True) + RMS_EPS) * ng
        o_ref[:, h * DV_A:(h + 1) * DV_A] = o * _silu(z_ref[:, off_g + h * DV_A: off_g + (h + 1) * DV_A])
    for h in hb:
        sr_s[h] = jnp.exp(cnt_end * lg[h]) * sr_s[h] + updb[h]
        mu = jnp.mean(ob[h], -1, keepdims=True)
        cc = ob[h] - mu
        o = cc * lax.rsqrt(jnp.mean(cc * cc, -1, keepdims=True) + LN_EPS)
        o_ref[:, H_A * DV_A + h * DV_B: H_A * DV_A + (h + 1) * DV_B] = \
            o * _silu(z_ref[:, off_bg + h * DV_B: off_bg + (h + 1) * DV_B])

    @pl.when(c == chunks - 1)
    def _():
        sho_ref[0] = sh_s[...]
        sro_ref[0] = sr_s[...]


def even_mixer_prompt(z, lb, ng, cos, sin, *, nb=BATCH, chunks=P_CHUNKS):
    kern = functools.partial(_even_kernel, L=CHUNK, nvalid=CHUNK, chunks=chunks, has_s0=False)
    return pl.pallas_call(
        kern,
        out_shape=(jax.ShapeDtypeStruct((z.shape[0], EVEN_MIX), F32),
                   jax.ShapeDtypeStruct((nb, H_A, DK_A, DV_A), F32),
                   jax.ShapeDtypeStruct((nb, H_B, DK_B, DV_B), F32)),
        grid=(nb, chunks),
        in_specs=[pl.BlockSpec((CHUNK, EVEN_IN), lambda b, c: (b * chunks + c, 0)),
                  pl.BlockSpec((1, H_A * DK_A), lambda b, c: (0, 0)),
                  pl.BlockSpec((1, DV_A), lambda b, c: (0, 0)),
                  pl.BlockSpec((CHUNK, DK_B), lambda b, c: (c, 0)),
                  pl.BlockSpec((CHUNK, DK_B), lambda b, c: (c, 0))],
        out_specs=(pl.BlockSpec((CHUNK, EVEN_MIX), lambda b, c: (b * chunks + c, 0)),
                   pl.BlockSpec((1, H_A, DK_A, DV_A), lambda b, c: (b, 0, 0, 0)),
                   pl.BlockSpec((1, H_B, DK_B, DV_B), lambda b, c: (b, 0, 0, 0))),
        scratch_shapes=[pltpu.VMEM((H_A, DK_A, DV_A), F32), pltpu.VMEM((H_B, DK_B, DV_B), F32)],
        compiler_params=pltpu.CompilerParams(dimension_semantics=("arbitrary", "arbitrary"),
                                             vmem_limit_bytes=VMEM_LIMIT),
        name="even_mixer_prompt",
    )(z, lb, ng, cos, sin)


def even_mixer_sample(z, lb, ng, cos, sin, s_hgrn, s_ret):
    nb = z.shape[0]
    kern = functools.partial(_even_kernel, L=S_LEN, nvalid=DEC_SEQ, chunks=1, has_s0=True)
    return pl.pallas_call(
        kern,
        out_shape=(jax.ShapeDtypeStruct((nb, S_LEN, EVEN_MIX), F32),
                   jax.ShapeDtypeStruct((nb, H_A, DK_A, DV_A), F32),
                   jax.ShapeDtypeStruct((nb, H_B, DK_B, DV_B), F32)),
        grid=(nb, 1),
        in_specs=[pl.BlockSpec((None, S_LEN, EVEN_IN), lambda b, c: (b, 0, 0)),
                  pl.BlockSpec((1, H_A * DK_A), lambda b, c: (0, 0)),
                  pl.BlockSpec((1, DV_A), lambda b, c: (0, 0)),
                  pl.BlockSpec((S_LEN, DK_B), lambda b, c: (0, 0)),
                  pl.BlockSpec((S_LEN, DK_B), lambda b, c: (0, 0)),
                  pl.BlockSpec((1, H_A, DK_A, DV_A), lambda b, c: (b, 0, 0, 0)),
                  pl.BlockSpec((1, H_B, DK_B, DV_B), lambda b, c: (b, 0, 0, 0))],
        out_specs=(pl.BlockSpec((None, S_LEN, EVEN_MIX), lambda b, c: (b, 0, 0)),
                   pl.BlockSpec((1, H_A, DK_A, DV_A), lambda b, c: (b, 0, 0, 0)),
                   pl.BlockSpec((1, H_B, DK_B, DV_B), lambda b, c: (b, 0, 0, 0))),
        scratch_shapes=[pltpu.VMEM((H_A, DK_A, DV_A), F32), pltpu.VMEM((H_B, DK_B, DV_B), F32)],
        compiler_params=pltpu.CompilerParams(dimension_semantics=("arbitrary", "arbitrary"),
                                             vmem_limit_bytes=VMEM_LIMIT),
        name="even_mixer_sample",
    )(z, lb, ng, cos, sin, s_hgrn, s_ret)


def _rope_tables(pos):
    half = DK_B // 2
    inv = ROPE_BASE ** (-jnp.arange(half, dtype=F32) / half)
    ang = pos.astype(F32)[:, None] * inv
    cos, sin = jnp.cos(ang), jnp.sin(ang)
    return jnp.concatenate([cos, cos], -1), jnp.concatenate([-sin, sin], -1)


ODD_PAD = 6144
O_Z, O_X, O_R, O_K, O_V, O_T = 0, DI_C, DI_C + CONV_DIM, DI_C + CONV_DIM + DI_D, DI_C + CONV_DIM + 2 * DI_D, \
    DI_C + CONV_DIM + 3 * DI_D
T_W = 384
T_DT = R_W + R_A + R_G
SHIFT_W = 3 * DI_D + T_W
HALF = 64


def _softplus(x):
    return jnp.maximum(x, 0.0) + jnp.log1p(jnp.exp(-jnp.abs(x)))


def _lane_lt(n, width):
    return lax.broadcasted_iota(jnp.int32, (1, width), 1) < n


def _pair_ones():
    r = lax.broadcasted_iota(jnp.int32, (2 * HALF, 2 * HALF), 0) < HALF
    c = lax.broadcasted_iota(jnp.int32, (2 * HALF, 2 * HALF), 1) < HALF
    return r == c


def _stack_pair(x, m_a):
    return jnp.concatenate([jnp.where(m_a, x, 0.0), jnp.where(m_a, 0.0, x)], axis=0)


def _rwkv_pairs(rs, ks_, vs_, als, bes, logws, Gs, bds, L):
    n = range(len(rs))
    m_a = _lane_lt(HALF, 2 * HALF)
    m_l = _lane_lt(L, 2 * L)
    ri = lax.broadcasted_iota(jnp.int32, (2 * L, 2 * L), 0)
    ci = lax.broadcasted_iota(jnp.int32, (2 * L, 2 * L), 1)
    cm = jnp.where(ci >= L, ci - L, ci)
    keep = cm < jnp.where(ri >= L, ri - L + 1, ri)
    eye = (ri == ci).astype(F32)
    blk4 = (ri // 4) == (ci // 4)
    pair_blk = _pair_ones()

    e_inv = [jnp.exp(-Gs[i]) for i in n]
    lhs = [jnp.concatenate([als[i] * jnp.exp(Gs[i] - logws[i]), rs[i] * jnp.exp(Gs[i])], axis=0).astype(BF16) for i in n]
    ks = [_stack_pair(ks_[i] * e_inv[i], m_a).astype(BF16) for i in n]
    bs = [_stack_pair(bes[i] * e_inv[i], m_a).astype(BF16) for i in n]
    vs = [_stack_pair(vs_[i], m_a).astype(BF16) for i in n]
    a_k = [jnp.where(keep, _dot_bf(lhs[i], ks[i], NT), 0.0) for i in n]
    a_b = [jnp.where(keep, _dot_bf(lhs[i], bs[i], NT), 0.0) for i in n]
    sb = [_dot_bf(lhs[i], bds[i], NT) for i in n]
    base = [sb[i] + _dot_bf(a_k[i], vs[i]) for i in n]
    n_bd = [jnp.concatenate([jnp.where(m_l, a_b[i][0:L], 0.0), jnp.where(m_l, 0.0, a_b[i][0:L])], axis=0) for i in n]
    n4 = [jnp.where(blk4, n_bd[i], 0.0) for i in n]
    n4sq = [_dot_bf(n4[i], n4[i]) for i in n]
    t_inv = [eye - n4[i] for i in n]
    t_inv = [t_inv[i] + _dot_bf(t_inv[i], n4sq[i]) for i in n]
    blk = 4
    while blk < L:
        off = ((ri // (2 * blk)) == (ci // (2 * blk))) & ((ri % (2 * blk)) >= blk) & ((ci % (2 * blk)) < blk)
        ct = [_dot_bf(jnp.where(off, n_bd[i], 0.0), t_inv[i]) for i in n]
        t_inv = [t_inv[i] - _dot_bf(t_inv[i], ct[i]) for i in n]
        blk *= 2
    u = [_dot_bf(t_inv[i][0:L, :] + t_inv[i][L:2 * L, :], _stack_pair(base[i][0:L], m_a)) for i in n]
    y = [base[i][L:2 * L] - _dot_bf(a_b[i][L:2 * L], _stack_pair(u[i], m_a)) for i in n]
    gl = [Gs[i][L - 1:L, :] for i in n]
    dec = [jnp.exp(gl[i] - Gs[i]) for i in n]
    upd = [_dot_bf(jnp.concatenate([vs_[i], u[i]], axis=0),
                   jnp.concatenate([ks_[i] * dec[i], -(bes[i] * dec[i])], axis=0), TN) for i in n]
    bd_new = [bds[i] * jnp.exp(gl[i]) + jnp.where(pair_blk, upd[i], 0.0) for i in n]
    return y, bd_new


def _ssd_pair(xdt, cg, bg, cb, bcs, brows, sp, L):
    m_a = _lane_lt(HALF, 2 * HALF)
    causal = _tril(L)
    out = jnp.zeros((L, 2 * HALF), F32)
    ends = [bc[L - 1:L, :] for bc in bcs]
    sp_new = sp * jnp.where(m_a, jnp.exp(ends[0]), jnp.exp(ends[1]))
    for x in range(2):
        keep = m_a if x == 0 else jnp.logical_not(m_a)
        seg = jnp.where(causal, jnp.exp(bcs[x] - brows[x]), 0.0)
        xm = jnp.where(keep, xdt, 0.0)
        out = out + _dot_bf(cb * seg, xm) + _dot_bf(cg * jnp.exp(bcs[x]), jnp.where(keep, sp, 0.0))
        sp_new = sp_new + _dot_bf(bg * jnp.exp(ends[x] - bcs[x]), xm, TN)
    return out, sp_new


def _odd_kernel(*refs, L, nvalid, chunks, has_s0):
    if has_s0:
        (z_ref, vec_ref, cp_ref, tp_ref, w2_ref, a2_ref, g2_ref, ssm0_ref, wkv0_ref, conv0_ref, shift0_ref,
         o_ref, ssmo_ref, wkvo_ref, convo_ref, shifto_ref, ssm_s, wkv_s, conv_c, shift_c) = refs
    else:
        (z_ref, vec_ref, cp_ref, tp_ref, w2_ref, a2_ref, g2_ref,
         o_ref, ssmo_ref, wkvo_ref, convo_ref, shifto_ref, ssm_s, wkv_s, conv_c, shift_c) = refs
    c = pl.program_id(1)
    npair = H_C // 2

    @pl.when(c == 0)
    def _():
        if has_s0:
            zpad = jnp.zeros((HALF, HALF), F32)
            for p in range(npair):
                ssm_s[p] = jnp.concatenate([ssm0_ref[0, 2 * p], ssm0_ref[0, 2 * p + 1]], axis=1)
                wkv_s[p] = jnp.concatenate([jnp.concatenate([wkv0_ref[0, 2 * p], zpad], axis=1),
                                            jnp.concatenate([zpad, wkv0_ref[0, 2 * p + 1]], axis=1)], axis=0)
            conv_c[...] = conv0_ref[0]
            shift_c[...] = shift0_ref[0]
        else:
            ssm_s[...] = jnp.zeros_like(ssm_s)
            wkv_s[...] = jnp.zeros_like(wkv_s)
            conv_c[...] = jnp.zeros_like(conv_c)
            shift_c[...] = jnp.zeros_like(shift_c)

    lead = jnp.where(c == 0, P_LEAD, 0) if chunks > 1 else 0
    r_idx, valid = _row_info(L, lead, nvalid)
    m_a = _lane_lt(HALF, 2 * HALF)
    tril = _tril(L).astype(F32)
    triu = (lax.broadcasted_iota(jnp.int32, (L, L), 0) <= lax.broadcasted_iota(jnp.int32, (L, L), 1)).astype(F32)
    ones_blk = _pair_ones().astype(F32)

    xbc = jnp.where(valid, z_ref[:, O_X:O_X + CONV_DIM], 0.0)
    c8 = conv_c[...]
    r8 = lax.broadcasted_iota(jnp.int32, (8, 1), 0)
    conv = cp_ref[CONV_W:CONV_W + 1, :] + xbc * cp_ref[CONV_W - 1:CONV_W, :]
    for j in range(1, CONV_W):
        rolled = pltpu.roll(xbc, j, 0)
        head = jnp.where(r8 < j, pltpu.roll(c8, j, 0), rolled[0:8])
        sh = jnp.concatenate([head, rolled[8:]], axis=0) if L > 8 else head
        conv = conv + sh * cp_ref[CONV_W - 1 - j:CONV_W - j, :]
    conv_c[...] = xbc[L - 8:L]
    act = _silu(conv)
    xc, bm, cm = act[:, :DI_C], act[:, DI_C:DI_C + G_C * N_C], act[:, DI_C + G_C * N_C:]

    t_raw = z_ref[:, O_T:O_T + T_W]
    dt = jnp.where(valid, _softplus(t_raw + tp_ref[1:2, :])[:, T_DT:T_DT + H_C], 0.0)
    logf = dt * tp_ref[2:3, T_DT:T_DT + H_C]
    bc_all = _dot_sel(logf, tril, sel_first=True)
    brow_all = _dot_sel(logf, triu, TN)
    per_g = npair // G_C
    for g in range(G_C):
        cg = cm[:, g * N_C:(g + 1) * N_C]
        bg = bm[:, g * N_C:(g + 1) * N_C]
        cb = _dot_bf(cg, bg, NT)
        ys, ss = [], jnp.zeros((L, 1), F32)
        for pp in range(per_g):
            p = g * per_g + pp
            sl = slice(p * 2 * HALF, (p + 1) * 2 * HALF)
            h_a, h_b = 2 * p, 2 * p + 1
            xcp = xc[:, sl]
            xdt = xcp * jnp.where(m_a, dt[:, h_a:h_a + 1], dt[:, h_b:h_b + 1])
            o, sp_new = _ssd_pair(xdt, cg, bg, cb, [bc_all[:, h_a:h_a + 1], bc_all[:, h_b:h_b + 1]],
                                  [brow_all[h_a:h_a + 1, :], brow_all[h_b:h_b + 1, :]], ssm_s[p], L)
            ssm_s[p] = sp_new
            y = (o + xcp * vec_ref[0:1, sl]) * _silu(z_ref[:, O_Z + p * 2 * HALF:O_Z + (p + 1) * 2 * HALF])
            ys.append(y)
            ss = ss + jnp.sum(y * y, axis=-1, keepdims=True)
        scale = lax.rsqrt(ss / (DI_C // G_C) + RMS_EPS)
        for pp in range(per_g):
            p = g * per_g + pp
            sl = slice(p * 2 * HALF, (p + 1) * 2 * HALF)
            o_ref[:, sl] = ys[pp] * scale * vec_ref[1:2, sl]

    def shift_mix(p_raw, carry_row, mu):
        pv = jnp.where(valid, p_raw, 0.0)
        prev = jnp.where(r_idx == 0, carry_row, pltpu.roll(pv, 1, 0))
        return pv, pv + (prev - pv) * mu

    r_raw, r = shift_mix(z_ref[:, O_R:O_R + DI_D], shift_c[7:8, 0:DI_D], vec_ref[2:3, :])
    k_raw, k = shift_mix(z_ref[:, O_K:O_K + DI_D], shift_c[7:8, DI_D:2 * DI_D], vec_ref[3:4, :])
    v_raw, v = shift_mix(z_ref[:, O_V:O_V + DI_D], shift_c[7:8, 2 * DI_D:3 * DI_D], vec_ref[4:5, :])
    t_rawm, tm = shift_mix(t_raw, shift_c[7:8, 3 * DI_D:SHIFT_W], tp_ref[0:1, :])
    shift_c[:, 0:DI_D] = r_raw[L - 8:L]
    shift_c[:, DI_D:2 * DI_D] = k_raw[L - 8:L]
    shift_c[:, 2 * DI_D:3 * DI_D] = v_raw[L - 8:L]
    shift_c[:, 3 * DI_D:SHIFT_W] = t_rawm[L - 8:L]

    th_hi, th_lo = _parts(jnp.tanh(tm), 2)
    w2_hi, w2_lo = _parts(w2_ref[...], 2)
    w_pre = vec_ref[5:6, :] + (jnp.dot(th_hi, w2_hi, preferred_element_type=F32) +
                               jnp.dot(th_lo, w2_hi, preferred_element_type=F32) +
                               jnp.dot(th_hi, w2_lo, preferred_element_type=F32))
    logw = jnp.where(valid, -jnp.exp(-_softplus(-w_pre) - 0.5), 0.0)
    g_cum = _dot_sel(logw, tril, sel_first=True)
    a = _sigmoid(vec_ref[6:7, :] + _dot_bf(tm, a2_ref[...]))
    gate = _dot_bf(_sigmoid(tm), g2_ref[...])
    kkr = k * vec_ref[7:8, :]
    k2 = k * (1.0 + (a - 1.0) * vec_ref[8:9, :])
    rk = r * k2 * vec_ref[9:10, :]
    pairs = range(npair)
    sls = [slice(p * 2 * HALF, (p + 1) * 2 * HALF) for p in pairs]
    nrm = [jnp.sqrt(_dot_sel(kkr[:, sl] * kkr[:, sl], ones_blk, n=2)) for sl in sls]
    al = [jnp.where(valid, kkr[:, sls[p]] / jnp.maximum(nrm[p], 1e-12), 0.0) for p in pairs]
    be = [al[p] * a[:, sls[p]] for p in pairs]
    ys, bd_new = _rwkv_pairs([r[:, sl] for sl in sls], [jnp.where(valid, k2[:, sl], 0.0) for sl in sls],
                             [v[:, sl] for sl in sls], al, be, [logw[:, sl] for sl in sls],
                             [g_cum[:, sl] for sl in sls], [wkv_s[p] for p in pairs], L)
    for p in pairs:
        wkv_s[p] = bd_new[p]
    mu = [_dot_sel(ys[p], ones_blk, n=2) * (1.0 / P_D) for p in pairs]
    yc = [ys[p] - mu[p] for p in pairs]
    var = [_dot_sel(yc[p] * yc[p], ones_blk, n=2) * (1.0 / P_D) for p in pairs]
    bonus = [_dot_sel(rk[:, sl], ones_blk, n=2) for sl in sls]
    for p in pairs:
        sl = sls[p]
        yn = yc[p] * lax.rsqrt(var[p] + RWKV_GN_EPS) * vec_ref[10:11, sl] + vec_ref[11:12, sl]
        o_ref[:, DI_C + p * 2 * HALF:DI_C + (p + 1) * 2 * HALF] = (yn + bonus[p] * v[:, sl]) * gate[:, sl]

    @pl.when(c == chunks - 1)
    def _():
        for p in range(npair):
            sp = ssm_s[p]
            ssmo_ref[0, 2 * p] = sp[:, 0:HALF]
            ssmo_ref[0, 2 * p + 1] = sp[:, HALF:2 * HALF]
            bd = wkv_s[p]
            wkvo_ref[0, 2 * p] = bd[0:HALF, 0:HALF]
            wkvo_ref[0, 2 * p + 1] = bd[HALF:2 * HALF, HALF:2 * HALF]
        convo_ref[0] = conv_c[...]
        shifto_ref[0] = shift_c[...]


def _odd_call(z, params, states, *, L, nvalid, nb, chunks, row_map, name):
    has_s0 = states is not None
    kern = functools.partial(_odd_kernel, L=L, nvalid=nvalid, chunks=chunks, has_s0=has_s0)
    npair = H_C // 2
    if z.ndim == 2:
        z_spec = pl.BlockSpec((L, ODD_PAD), lambda b, c: (b * chunks + c, 0))
        o_spec = pl.BlockSpec((L, ODD_MIX), lambda b, c: (b * chunks + c, 0))
        o_shape = jax.ShapeDtypeStruct((z.shape[0], ODD_MIX), F32)
    else:
        z_spec = pl.BlockSpec((None, L, ODD_PAD), lambda b, c: (b, 0, 0))
        o_spec = pl.BlockSpec((None, L, ODD_MIX), lambda b, c: (b, 0, 0))
        o_shape = jax.ShapeDtypeStruct((nb, L, ODD_MIX), F32)
    const2 = lambda b, c: (0, 0)
    per_b = lambda b, c: (b, 0, 0, 0)
    per_b3 = lambda b, c: (b, 0, 0)
    in_specs = [z_spec] + [pl.BlockSpec(p.shape, const2) for p in params]
    args = [z] + list(params)
    st_specs = [pl.BlockSpec((1, H_C, N_C, P_C), per_b), pl.BlockSpec((1, H_D, P_D, P_D), per_b),
                pl.BlockSpec((1, 8, CONV_DIM), per_b3), pl.BlockSpec((1, 8, SHIFT_W), per_b3)]
    if has_s0:
        in_specs += st_specs
        args += list(states)
    return pl.pallas_call(
        kern,
        out_shape=(o_shape,
                   jax.ShapeDtypeStruct((nb, H_C, N_C, P_C), F32), jax.ShapeDtypeStruct((nb, H_D, P_D, P_D), F32),
                   jax.ShapeDtypeStruct((nb, 8, CONV_DIM), F32), jax.ShapeDtypeStruct((nb, 8, SHIFT_W), F32)),
        grid=(nb, chunks),
        in_specs=in_specs,
        out_specs=tuple([o_spec] + st_specs),
        scratch_shapes=[pltpu.VMEM((npair, N_C, 2 * HALF), F32), pltpu.VMEM((npair, 2 * HALF, 2 * HALF), F32),
                        pltpu.VMEM((8, CONV_DIM), F32), pltpu.VMEM((8, SHIFT_W), F32)],
        compiler_params=pltpu.CompilerParams(dimension_semantics=("arbitrary", "arbitrary"),
                                             vmem_limit_bytes=VMEM_LIMIT),
        name=name,
    )(*args)


def odd_mixer_prompt(z, params, *, nb=BATCH, chunks=P_CHUNKS):
    return _odd_call(z, params, None, L=CHUNK, nvalid=CHUNK, nb=nb, chunks=chunks, row_map=None,
                     name="odd_mixer_prompt")


def odd_mixer_sample(z, params, states):
    return _odd_call(z, params, states, L=S_LEN, nvalid=DEC_SEQ, nb=z.shape[0], chunks=1, row_map=None,
                     name="odd_mixer_sample")


def _odd_params(odd_w_in, conv_w, conv_b, dt_bias, a_log, d_skip, ssm_norm_g, shift_mu, rwkv_w0, rwkv_w2, rwkv_a0,
                rwkv_a2, rwkv_g2, rwkv_k_k, rwkv_k_a, rwkv_r_k, lnx_g, lnx_b):
    o_dt = DI_C + CONV_DIM
    o_rw = o_dt + H_C
    w = jnp.concatenate([odd_w_in[:, :o_dt], odd_w_in[:, o_rw:], odd_w_in[:, o_dt:o_rw],
                         jnp.zeros((D_MODEL, ODD_PAD - ODD_IN), odd_w_in.dtype)], axis=1)
    vec = jnp.stack([jnp.repeat(d_skip, P_C), ssm_norm_g, shift_mu[:DI_D], shift_mu[DI_D:2 * DI_D],
                     shift_mu[2 * DI_D:3 * DI_D], rwkv_w0, rwkv_a0, rwkv_k_k, rwkv_k_a, rwkv_r_k.reshape(-1),
                     lnx_g, lnx_b] + [jnp.zeros((DI_D,), F32)] * 4)
    cpack = jnp.concatenate([conv_w, conv_b[None], jnp.zeros((3, CONV_DIM), F32)], axis=0)
    zt = jnp.zeros((T_W,), F32)
    tpack = jnp.stack([zt.at[:T_DT].set(shift_mu[3 * DI_D:]), zt.at[T_DT:T_DT + H_C].set(dt_bias),
                       zt.at[T_DT:T_DT + H_C].set(-jnp.exp(a_log.astype(F32)))] + [zt] * 5)
    zw = jnp.zeros((T_W, DI_D), F32)
    w2p = zw.at[:R_W].set(rwkv_w2)
    a2p = zw.at[R_W:R_W + R_A].set(rwkv_a2).astype(BF16)
    g2p = zw.at[R_W + R_A:T_DT].set(rwkv_g2).astype(BF16)
    return w.astype(BF16), (vec, cpack, tpack, w2p, a2p, g2p)


def _top16_desc(cur):
    vals = []
    for _ in range(PEER_TOPK):
        m = jnp.max(cur, axis=0, keepdims=True)
        vals.append(m)
        cur = jnp.where(cur == m, -jnp.inf, cur)
    return vals


PEER_CAND = 112


def _peer_kernel(x_ref, xb_ref, wq_ref, sk_ref, u_ref, v_ref, g_ref, b_ref, o_ref, ob_ref,
                 q_s, s1_s, s2_s, e1_s, e2_s, tau_s, cand_s, w_s, h_s, p_s, pn_s, *, tm, te):
    assert te == 4 * PEER_KEYS
    s = pl.program_id(1)
    ns = pl.num_programs(1)
    nk = PEER_KEYS
    neg = -jnp.inf

    @pl.when(s == 0)
    def _route():
        q_s[...] = lax.dot_general(wq_ref[...], xb_ref[...], NT, preferred_element_type=F32)
        a_idx = lax.broadcasted_iota(jnp.int32, (PEER_TOPK, 1), 0)

        def head(h, carry):
            tops = []
            for c in range(2):
                row0 = pl.multiple_of((2 * h + c) * nk, nk)
                sc = jnp.dot(sk_ref[2 * h + c], q_s[pl.ds(row0, nk), :].astype(BF16), preferred_element_type=F32)
                if c == 0:
                    s1_s[h] = sc
                else:
                    s2_s[h] = sc
                tops.append(_top16_desc(sc))
            t1, t2 = tops
            t1_all = jnp.concatenate(t1, axis=0)
            t2_all = jnp.concatenate(t2, axis=0)
            for b in range(4):
                cand_s[b * PEER_TOPK:(b + 1) * PEER_TOPK, :] = jnp.where(a_idx < PEER_TOPK // (b + 1), t1_all + t2[b], neg)
            for a in range(3):
                ok = (a_idx >= 4) & (a_idx < PEER_TOPK // (a + 1))
                cand_s[(4 + a) * PEER_TOPK:(5 + a) * PEER_TOPK, :] = jnp.where(ok, t1[a] + t2_all, neg)
            best = _top16_desc(cand_s[...])
            mx = t1[0] + t2[0]
            z = jnp.zeros_like(mx)
            for m in best:
                z = z + jnp.exp(m - mx)
            tau_s[pl.ds(h, 1), :] = best[-1]
            e1_s[h] = jnp.exp(s1_s[h] - t1[0])
            e2_s[h] = jnp.exp(s2_s[h] - t2[0]) / z
            return carry

        lax.fori_loop(0, PEER_HEADS, head, 0)
        o_ref[...] = jnp.zeros_like(o_ref)
        p_s[...] = jnp.zeros_like(p_s)

    blk = jnp.minimum(s, ns - 2)
    nj = te // nk
    d = o_ref.shape[1]
    kp, cp = d // 4, d // 8

    def mm1(k):
        part = lax.dot_general(u_ref[:, k * kp:(k + 1) * kp], xb_ref[:, k * kp:(k + 1) * kp], NT,
                               preferred_element_type=F32)
        if k == 0:
            h_s[...] = part
        else:
            h_s[...] += part

    def wbuild(j):
        i1 = blk * nj + j
        w = jnp.zeros((nk, tm), F32)
        for h in range(PEER_HEADS):
            c = s2_s[h] + s1_s[h, pl.ds(i1, 1), :]
            w = w + jnp.where(c >= tau_s[h:h + 1, :], e2_s[h] * e1_s[h, pl.ds(i1, 1), :], 0.0)
        w_s[j * nk:(j + 1) * nk, :] = w

    def act(j):
        hj = h_s[j * nk:(j + 1) * nk, :]
        g = 0.5 * hj * (1.0 + lax.erf(hj * (2.0 ** -0.5)))
        pn_s[:, j * nk:(j + 1) * nk] = jnp.transpose(w_s[j * nk:(j + 1) * nk, :] * g).astype(BF16)

    def mm2(n):
        o_ref[:, n * cp:(n + 1) * cp] += jnp.dot(p_s[...], v_ref[:, n * cp:(n + 1) * cp], preferred_element_type=F32)

    mm1(0); wbuild(0); mm1(1); wbuild(1); mm1(2); wbuild(2); mm1(3)
    mm2(0); wbuild(3); mm2(1); act(0); mm2(2); act(1); mm2(3); act(2); mm2(4); act(3); mm2(5); mm2(6); mm2(7)
    p_s[...] = pn_s[...]

    @pl.when(s == ns - 1)
    def _fin():
        y = _ln_rows(ALPHA * x_ref[...] + o_ref[...], g_ref[...], b_ref[...])
        o_ref[...] = y
        ob_ref[...] = y.astype(BF16)


def peer_ln(x, xb, wq_t, sk, u, v, g, b, *, tm=512, te=512):
    m, d = x.shape
    kern = functools.partial(_peer_kernel, tm=tm, te=te)
    const = dict(pipeline_mode=pl.Buffered(1))
    ne = PEER_EXPERTS // te
    route = pltpu.VMEM((PEER_HEADS, PEER_KEYS, tm), F32)
    return pl.pallas_call(
        kern,
        out_shape=(jax.ShapeDtypeStruct((m, d), F32), jax.ShapeDtypeStruct((m, d), BF16)),
        grid=(m // tm, ne + 1),
        in_specs=[pl.BlockSpec((tm, d), lambda i, s: (i, 0), **const),
                  pl.BlockSpec((tm, d), lambda i, s: (i, 0), **const),
                  pl.BlockSpec(wq_t.shape, lambda i, s: (0, 0), **const),
                  pl.BlockSpec(sk.shape, lambda i, s: (0, 0, 0), **const),
                  pl.BlockSpec((te, d), lambda i, s: (jnp.minimum(s, ne - 1), 0)),
                  pl.BlockSpec((te, d), lambda i, s: (jnp.maximum(s - 1, 0), 0)),
                  pl.BlockSpec((1, d), lambda i, s: (0, 0), **const),
                  pl.BlockSpec((1, d), lambda i, s: (0, 0), **const)],
        out_specs=(pl.BlockSpec((tm, d), lambda i, s: (i, 0)), pl.BlockSpec((tm, d), lambda i, s: (i, 0))),
        scratch_shapes=[pltpu.VMEM((PEER_HEADS * PEER_QDIM, tm), F32), route, route, route, route,
                        pltpu.VMEM((PEER_HEADS, tm), F32), pltpu.VMEM((PEER_CAND, tm), F32),
                        pltpu.VMEM((te, tm), F32), pltpu.VMEM((te, tm), F32),
                        pltpu.VMEM((tm, te), BF16), pltpu.VMEM((tm, te), BF16)],
        compiler_params=pltpu.CompilerParams(dimension_semantics=("arbitrary", "arbitrary"),
                                             vmem_limit_bytes=60 * 1024 * 1024),
        name="peer_ln",
    )(x, xb, wq_t, sk, u, v, g.reshape(1, d), b.reshape(1, d))


def _sample_rows(z):
    f = z.shape[1]
    zs = lax.slice(z, (S_ROW0, 0), (S_ROW0 + DEC_BATCH * DEC_SEQ, f)).reshape(DEC_BATCH, DEC_SEQ, f)
    return jnp.pad(zs, ((0, 0), (0, S_LEN - DEC_SEQ), (0, 0)))


def _merge_rows(buf, sample_out):
    f = buf.shape[1]
    tail = jnp.concatenate([sample_out[:, :DEC_SEQ].reshape(DEC_BATCH * DEC_SEQ, f),
                            jnp.zeros((M_PAD - S_ROW0 - DEC_BATCH * DEC_SEQ, f), buf.dtype)], axis=0)
    return lax.dynamic_update_slice(buf, tail, (S_ROW0, 0))


def kernel(x_prompt, x_sample, state_hgrn, state_ret, state_ssm, state_conv, state_wkv, state_shift, meta_tokens, ln_g, ln_b, even_w_in, hgrn_lb_logits, hgrn_norm_g, even_w_out, odd_w_in, conv_w, conv_b, dt_bias, a_log, d_skip, ssm_norm_g, shift_mu, rwkv_w0, rwkv_w2, rwkv_a0, rwkv_a2, rwkv_g2, rwkv_k_k, rwkv_k_a, rwkv_r_k, lnx_g, lnx_b, odd_w_out, peer_w_query, peer_sub_keys, peer_u, peer_v):
    dt = x_prompt.dtype
    lead = jnp.concatenate([jnp.zeros((P_LEAD, D_MODEL), dt), meta_tokens.astype(dt)], axis=0)
    xp = jnp.concatenate([jnp.broadcast_to(lead[None], (BATCH, CHUNK, D_MODEL)), x_prompt], axis=1)
    x = jnp.concatenate([xp.reshape(S_ROW0, D_MODEL), x_sample.reshape(DEC_BATCH * DEC_SEQ, D_MODEL),
                         jnp.zeros((M_PAD - S_ROW0 - DEC_BATCH * DEC_SEQ, D_MODEL), dt)], axis=0)
    xb = x.astype(BF16)

    cos_p, sin_p = _rope_tables(jnp.arange(P_ROWS) - P_LEAD)
    cos_s, sin_s = _rope_tables(PAST_LEN + jnp.arange(S_LEN))
    lb_table = jnp.cumsum(jax.nn.softmax(hgrn_lb_logits.astype(F32), axis=0), axis=0)

    z = matmul(xb, even_w_in[0].astype(BF16), tm=TOK_TILE, tn=1024, name="even_in")
    lb = lb_table[0].reshape(1, -1)
    ng = hgrn_norm_g[0].reshape(1, -1)
    mix_p, hgrn_p, ret_p = even_mixer_prompt(z, lb, ng, cos_p, sin_p)
    mix_s, hgrn_s, ret_s = even_mixer_sample(_sample_rows(z), lb, ng, cos_s, sin_s, state_hgrn[0], state_ret[0])
    mix = _merge_rows(mix_p, mix_s)
    x, xb = proj_ln(x, mix, even_w_out[0].astype(BF16), ln_g[0, 0], ln_b[0, 0], tm=TOK_TILE, name="even_out")
    x, xb = peer_ln(x, xb, peer_w_query[0].T.astype(BF16),
                    peer_sub_keys[0].reshape(2 * PEER_HEADS, PEER_KEYS, PEER_QDIM // 2).astype(BF16),
                    peer_u[0].astype(BF16), peer_v[0].astype(BF16), ln_g[0, 1], ln_b[0, 1])

    w_in1, params = _odd_params(odd_w_in[0], conv_w[0], conv_b[0], dt_bias[0], a_log[0], d_skip[0], ssm_norm_g[0],
                                shift_mu[0], rwkv_w0[0], rwkv_w2[0], rwkv_a0[0], rwkv_a2[0], rwkv_g2[0], rwkv_k_k[0],
                                rwkv_k_a[0], rwkv_r_k[0], lnx_g[0], lnx_b[0])
    z = matmul(xb, w_in1, tm=TOK_TILE, tn=1024, name="odd_in")
    mix_p, ssm_p, wkv_p, conv_p, shift_p = odd_mixer_prompt(z, params)
    conv8 = jnp.pad(state_conv[0], ((0, 0), (8 - (CONV_W - 1), 0), (0, 0)))
    shift8 = jnp.pad(state_shift[0][:, None, :], ((0, 0), (7, 0), (0, SHIFT_W - SHIFT_DIM)))
    mix_s, ssm_s, wkv_s, conv_s, shift_s = odd_mixer_sample(_sample_rows(z), params,
                                                            (state_ssm[0], state_wkv[0], conv8, shift8))
    mix = _merge_rows(mix_p, mix_s)
    x, xb = proj_ln(x, mix, odd_w_out[0].astype(BF16), ln_g[1, 0], ln_b[1, 0], tm=TOK_TILE, name="odd_out")
    x, xb = peer_ln(x, xb, peer_w_query[1].T.astype(BF16),
                    peer_sub_keys[1].reshape(2 * PEER_HEADS, PEER_KEYS, PEER_QDIM // 2).astype(BF16),
                    peer_u[1].astype(BF16), peer_v[1].astype(BF16), ln_g[1, 1], ln_b[1, 1])

    y_prompt = x[:S_ROW0].reshape(BATCH, P_ROWS, D_MODEL)[:, CHUNK:]
    y_sample = x[S_ROW0:S_ROW0 + DEC_BATCH * DEC_SEQ].reshape(DEC_BATCH, DEC_SEQ, D_MODEL)
    nc = CONV_W - 1
    return (y_prompt, y_sample, hgrn_p[None], hgrn_s[None], ret_p[None], ret_s[None], ssm_p[None], ssm_s[None],
            conv_p[None, :, 8 - nc:], conv_s[None, :, DEC_SEQ - nc:DEC_SEQ], wkv_p[None], wkv_s[None],
            shift_p[None, :, 7, :SHIFT_DIM], shift_s[None, :, DEC_SEQ - 1, :SHIFT_DIM])
```

```python
import functools
import math

import jax
import jax.numpy as jnp
from jax import lax
from jax.experimental import pallas as pl
from jax.experimental.pallas import tpu as pltpu

D_MODEL = 2048
BATCH = 4
SEQ = 2048
DEPTH = 2
DEC_BATCH = 128
DEC_SEQ = 4
PAST_LEN = 16384
N_META = 16
CHUNK = 64

H_A, DK_A, DV_A = 8, 128, 128
H_B, DK_B, DV_B = 4, 128, 256
EVEN_IN = 4 * H_A * DK_A + 2 * H_B * DK_B + 2 * H_B * DV_B
EVEN_MIX = H_A * DV_A + H_B * DV_B

H_C, P_C, N_C, G_C, CONV_W = 16, 64, 128, 2, 4
DI_C = H_C * P_C
CONV_DIM = DI_C + 2 * G_C * N_C
H_D, P_D = 16, 64
DI_D = H_D * P_D
R_W, R_A, R_G = 64, 64, 160
SHIFT_DIM = 3 * DI_D + R_W + R_A + R_G
ODD_IN = DI_C + CONV_DIM + H_C + SHIFT_DIM
ODD_MIX = DI_C + DI_D

PEER_KEYS = 128
PEER_EXPERTS = PEER_KEYS * PEER_KEYS
PEER_HEADS = 8
PEER_TOPK = 16
PEER_QDIM = 256

ALPHA = (2.0 * DEPTH) ** 0.25
LN_EPS = 1e-5
RMS_EPS = 1e-6
RWKV_GN_EPS = 64e-5
ROPE_BASE = 10000.0

F32 = jnp.float32
BF16 = jnp.bfloat16

P_LEAD = CHUNK - N_META
P_ROWS = CHUNK + SEQ
P_CHUNKS = P_ROWS // CHUNK
S_ROW0 = BATCH * SEQ
L_ROW0 = S_ROW0 + DEC_BATCH * DEC_SEQ
S_LEN = 8
TOK_TILE = 512
M_PAD = -(-(L_ROW0 + BATCH * CHUNK) // 1024) * 1024

VMEM_LIMIT = 56 * 1024 * 1024
PEER_VMEM_LIMIT = 60 * 1024 * 1024


def _dot_bf(a, b, dims=(((1,), (0,)), ((), ()))):
    return lax.dot_general(a.astype(BF16), b.astype(BF16), dims, preferred_element_type=F32)


def _parts(x, n):
    out, rem = [], x
    for i in range(n):
        p = rem.astype(BF16)
        out.append(p)
        if i + 1 < n:
            rem = rem - p.astype(F32)
    return out


def _dot_sel(x, sel, dims=(((1,), (0,)), ((), ())), n=3, sel_first=False):
    sel = sel.astype(BF16)
    acc = None
    for p in _parts(x, n):
        d = lax.dot_general(sel, p, dims, preferred_element_type=F32) if sel_first else \
            lax.dot_general(p, sel, dims, preferred_element_type=F32)
        acc = d if acc is None else acc + d
    return acc


NT = (((1,), (1,)), ((), ()))
TN = (((0,), (0,)), ((), ()))


def _sigmoid(x):
    return 1.0 / (1.0 + jnp.exp(-x))


def _silu(x):
    return x * _sigmoid(x)


def _mm_kernel(x_ref, w_ref, o_ref):
    o_ref[...] = jnp.dot(x_ref[...].astype(BF16), w_ref[...], preferred_element_type=F32).astype(o_ref.dtype)


def matmul(x, w, *, tm, tn, out_dtype=F32, name="matmul"):
    m, k = x.shape
    n = w.shape[1]
    return pl.pallas_call(
        _mm_kernel,
        out_shape=jax.ShapeDtypeStruct((m, n), out_dtype),
        grid=(n // tn, m // tm),
        in_specs=[pl.BlockSpec((tm, k), lambda j, i: (i, 0)),
                  pl.BlockSpec((k, tn), lambda j, i: (0, j))],
        out_specs=pl.BlockSpec((tm, tn), lambda j, i: (i, j)),
        compiler_params=pltpu.CompilerParams(dimension_semantics=("arbitrary", "arbitrary"),
                                             vmem_limit_bytes=VMEM_LIMIT),
        name=name,
    )(x, w)


def _ln_rows(v, g, b):
    mu = jnp.mean(v, -1, keepdims=True)
    c = v - mu
    var = jnp.mean(c * c, -1, keepdims=True)
    return c * lax.rsqrt(var + LN_EPS) * g + b


def _proj_ln_kernel(x_ref, m_ref, w_ref, g_ref, b_ref, o_ref, ob_ref):
    acc = jnp.dot(m_ref[...].astype(BF16), w_ref[...], preferred_element_type=F32)
    y = _ln_rows(ALPHA * x_ref[...] + acc, g_ref[...], b_ref[...])
    o_ref[...] = y
    ob_ref[...] = y.astype(BF16)


def proj_ln(x, mix, w, g, b, *, tm, name="proj_ln"):
    m, d = x.shape
    k = mix.shape[1]
    return pl.pallas_call(
        _proj_ln_kernel,
        out_shape=(jax.ShapeDtypeStruct((m, d), F32), jax.ShapeDtypeStruct((m, d), BF16)),
        grid=(m // tm,),
        in_specs=[pl.BlockSpec((tm, d), lambda i: (i, 0)),
                  pl.BlockSpec((tm, k), lambda i: (i, 0)),
                  pl.BlockSpec((k, d), lambda i: (0, 0)),
                  pl.BlockSpec((1, d), lambda i: (0, 0)),
                  pl.BlockSpec((1, d), lambda i: (0, 0))],
        out_specs=(pl.BlockSpec((tm, d), lambda i: (i, 0)), pl.BlockSpec((tm, d), lambda i: (i, 0))),
        compiler_params=pltpu.CompilerParams(dimension_semantics=("arbitrary",),
                                             vmem_limit_bytes=VMEM_LIMIT),
        name=name,
    )(x, mix, w, g.reshape(1, d), b.reshape(1, d))


def _row_info(L, lead, nvalid):
    r = lax.broadcasted_iota(jnp.int32, (L, 1), 0)
    return r, (r >= lead) & (r < lead + nvalid)


def _prompt_block(chunks, lead_blk0):
    return lambda b, c: (jnp.where(c == 0, lead_blk0 + b, b * (chunks - 1) + c - 1), 0)


def _tril(L, strict=False):
    r = lax.broadcasted_iota(jnp.int32, (L, L), 0)
    c = lax.broadcasted_iota(jnp.int32, (L, L), 1)
    return (c < r) if strict else (c <= r)


def _hgrn_heads(qs, ks, vs, logfs, s0s, L):
    n = range(len(qs))
    tril = _tril(L).astype(F32)
    ones_l = jnp.ones((L, DK_A), F32)
    ones_k = jnp.ones((DK_A, DK_A), BF16)
    b = [_dot_sel(logfs[i], tril, sel_first=True) for i in n]
    b_end_col = [_dot_sel(logfs[i], ones_l, TN) for i in n]
    o = [_dot_bf(qs[i] * jnp.exp(b[i]), s0s[i]) for i in n]
    upd = [_dot_bf(ks[i] * jnp.exp(b[i][L - 1:L, :] - b[i]), vs[i], TN) for i in n]
    s_new = [jnp.exp(b_end_col[i]) * s0s[i] + upd[i] for i in n]
    sub = min(16, L)
    t_idx = lax.broadcasted_iota(jnp.int32, (sub, 1), 0)
    outs = [[] for _ in n]
    for blk in range(L // sub):
        r0 = blk * sub
        acc = [o[i][r0:r0 + sub] for i in n]
        if blk > 0:
            a = [_dot_bf(qs[i][r0:r0 + sub] * jnp.exp(b[i][r0:r0 + sub] - b[i][r0 - 1:r0, :]),
                         ks[i][0:r0] * jnp.exp(b[i][r0 - 1:r0, :] - b[i][0:r0]), NT) for i in n]
            acc = [acc[i] + _dot_bf(a[i], vs[i][0:r0]) for i in n]
        prod = [jnp.concatenate(
            [qs[i][r0:r0 + sub] * ks[i][r0 + s:r0 + s + 1, :] *
             jnp.exp(jnp.where(t_idx >= s, b[i][r0:r0 + sub] - b[i][r0 + s:r0 + s + 1, :], -jnp.inf))
             for s in range(sub)], axis=0).astype(BF16) for i in n]
        score = [jnp.dot(prod[i], ones_k, preferred_element_type=F32) for i in n]
        for i in n:
            t = acc[i]
            for s in range(sub):
                t = t + score[i][s * sub:(s + 1) * sub, :] * vs[i][r0 + s:r0 + s + 1, :]
            outs[i].append(t)
    return [jnp.concatenate(x, axis=0) if len(x) > 1 else x[0] for x in outs], s_new


def _even_kernel(*refs, L, nvalid, chunks, has_s0):
    if has_s0:
        z_ref, lb_ref, ng_ref, cos_ref, sin_ref, sh0_ref, sr0_ref, o_ref, sho_ref, sro_ref, sh_s, sr_s = refs
    else:
        z_ref, lb_ref, ng_ref, cos_ref, sin_ref, o_ref, sho_ref, sro_ref, sh_s, sr_s = refs
    c = pl.program_id(1)

    @pl.when(c == 0)
    def _():
        if has_s0:
            sh_s[...] = sh0_ref[0]
            sr_s[...] = sr0_ref[0]
        else:
            sh_s[...] = jnp.zeros_like(sh_s)
            sr_s[...] = jnp.zeros_like(sr_s)

    lead = jnp.where(c == 0, P_LEAD, 0) if chunks > 1 else 0
    r, valid = _row_info(L, lead, nvalid)
    cnt_col = jnp.clip(r + 1 - lead, 0, nvalid).astype(F32)
    rr = lax.broadcasted_iota(jnp.int32, (1, L), 1)
    cnt_row = jnp.clip(rr + 1 - lead, 0, nvalid).astype(F32)

    off_q, off_f, off_i, off_g = 0, H_A * DK_A, 2 * H_A * DK_A, 2 * H_A * DK_A + H_A * DV_A
    base_b = 2 * H_A * DK_A + 2 * H_A * DV_A
    off_bq, off_bk, off_bv = base_b, base_b + H_B * DK_B, base_b + 2 * H_B * DK_B
    off_bg = off_bv + H_B * DV_B
    ng = ng_ref[...]
    ha, hb = range(H_A), range(H_B)
    af = [z_ref[:, off_f + h * DK_A: off_f + (h + 1) * DK_A] for h in ha]
    lbs = [lb_ref[:, h * DK_A:(h + 1) * DK_A] for h in ha]
    logf = [jnp.where(valid, jnp.log(lbs[h] + (1.0 - lbs[h]) * _sigmoid(af[h])), 0.0) for h in ha]
    ka = [jnp.where(valid, (1.0 - lbs[h]) * _sigmoid(-af[h]), 0.0) for h in ha]
    qa = [_silu(z_ref[:, off_q + h * DK_A: off_q + (h + 1) * DK_A]) for h in ha]
    va = [z_ref[:, off_i + h * DV_A: off_i + (h + 1) * DV_A] for h in ha]
    cosv, sinv = cos_ref[...], sin_ref[...]
    bq = [z_ref[:, off_bq + h * DK_B: off_bq + (h + 1) * DK_B] for h in hb]
    bk = [z_ref[:, off_bk + h * DK_B: off_bk + (h + 1) * DK_B] for h in hb]
    vb = [z_ref[:, off_bv + h * DV_B: off_bv + (h + 1) * DV_B] for h in hb]
    qb = [bq[h] * cosv + pltpu.roll(bq[h], DK_B // 2, 1) * sinv for h in hb]
    kb = [jnp.where(valid, (bk[h] * cosv + pltpu.roll(bk[h], DK_B // 2, 1) * sinv) * (DK_B ** -0.5), 0.0) for h in hb]
    lg = [math.log(1.0 - 2.0 ** (-5.0 - h)) for h in hb]

    oa, sa_new = _hgrn_heads(qa, ka, va, logf, [sh_s[h] for h in ha], L)
    causal = _tril(L)
    cnt_end = cnt_col[L - 1:L, :]
    scores = [_dot_bf(qb[h], kb[h], NT) * jnp.where(causal, jnp.exp((cnt_col - cnt_row) * lg[h]), 0.0) for h in hb]
    inter = [_dot_bf(qb[h] * jnp.exp(cnt_col * lg[h]), sr_s[h]) for h in hb]
    ob = [_dot_bf(scores[h], vb[h]) + inter[h] for h in hb]
    updb = [_dot_bf(kb[h] * jnp.exp((cnt_end - cnt_col) * lg[h]), vb[h], TN) for h in hb]
    for h in ha:
        sh_s[h] = sa_new[h]
        o = oa[h] * lax.rsqrt(jnp.mean(oa[h] * oa[h], -1, keepdims=True) + RMS_EPS) * ng
        o_ref[:, h * DV_A:(h + 1) * DV_A] = o * _silu(z_ref[:, off_g + h * DV_A: off_g + (h + 1) * DV_A])
    for h in hb:
        sr_s[h] = jnp.exp(cnt_end * lg[h]) * sr_s[h] + updb[h]
        mu = jnp.mean(ob[h], -1, keepdims=True)
        cc = ob[h] - mu
        o = cc * lax.rsqrt(jnp.mean(cc * cc, -1, keepdims=True) + LN_EPS)
        o_ref[:, H_A * DV_A + h * DV_B: H_A * DV_A + (h + 1) * DV_B] = \
            o * _silu(z_ref[:, off_bg + h * DV_B: off_bg + (h + 1) * DV_B])

    @pl.when(c == chunks - 1)
    def _():
        sho_ref[0] = sh_s[...]
        sro_ref[0] = sr_s[...]


def even_mixer_prompt(z, lb, ng, cos, sin, *, nb=BATCH, chunks=P_CHUNKS, lead_blk0=L_ROW0 // CHUNK):
    kern = functools.partial(_even_kernel, L=CHUNK, nvalid=CHUNK, chunks=chunks, has_s0=False)
    blk = _prompt_block(chunks, lead_blk0)
    return pl.pallas_call(
        kern,
        out_shape=(jax.ShapeDtypeStruct((z.shape[0], EVEN_MIX), F32),
                   jax.ShapeDtypeStruct((nb, H_A, DK_A, DV_A), F32),
                   jax.ShapeDtypeStruct((nb, H_B, DK_B, DV_B), F32)),
        grid=(nb, chunks),
        in_specs=[pl.BlockSpec((CHUNK, EVEN_IN), blk),
                  pl.BlockSpec((1, H_A * DK_A), lambda b, c: (0, 0)),
                  pl.BlockSpec((1, DV_A), lambda b, c: (0, 0)),
                  pl.BlockSpec((CHUNK, DK_B), lambda b, c: (c, 0)),
                  pl.BlockSpec((CHUNK, DK_B), lambda b, c: (c, 0))],
        out_specs=(pl.BlockSpec((CHUNK, EVEN_MIX), blk),
                   pl.BlockSpec((1, H_A, DK_A, DV_A), lambda b, c: (b, 0, 0, 0)),
                   pl.BlockSpec((1, H_B, DK_B, DV_B), lambda b, c: (b, 0, 0, 0))),
        scratch_shapes=[pltpu.VMEM((H_A, DK_A, DV_A), F32), pltpu.VMEM((H_B, DK_B, DV_B), F32)],
        compiler_params=pltpu.CompilerParams(dimension_semantics=("arbitrary", "arbitrary"),
                                             vmem_limit_bytes=VMEM_LIMIT),
        name="even_mixer_prompt",
    )(z, lb, ng, cos, sin)


def even_mixer_sample(z, lb, ng, cos, sin, s_hgrn, s_ret):
    nb = z.shape[0]
    kern = functools.partial(_even_kernel, L=S_LEN, nvalid=DEC_SEQ, chunks=1, has_s0=True)
    return pl.pallas_call(
        kern,
        out_shape=(jax.ShapeDtypeStruct((nb, S_LEN, EVEN_MIX), F32),
                   jax.ShapeDtypeStruct((nb, H_A, DK_A, DV_A), F32),
                   jax.ShapeDtypeStruct((nb, H_B, DK_B, DV_B), F32)),
        grid=(nb, 1),
        in_specs=[pl.BlockSpec((None, S_LEN, EVEN_IN), lambda b, c: (b, 0, 0)),
                  pl.BlockSpec((1, H_A * DK_A), lambda b, c: (0, 0)),
                  pl.BlockSpec((1, DV_A), lambda b, c: (0, 0)),
                  pl.BlockSpec((S_LEN, DK_B), lambda b, c: (0, 0)),
                  pl.BlockSpec((S_LEN, DK_B), lambda b, c: (0, 0)),
                  pl.BlockSpec((1, H_A, DK_A, DV_A), lambda b, c: (b, 0, 0, 0)),
                  pl.BlockSpec((1, H_B, DK_B, DV_B), lambda b, c: (b, 0, 0, 0))],
        out_specs=(pl.BlockSpec((None, S_LEN, EVEN_MIX), lambda b, c: (b, 0, 0)),
                   pl.BlockSpec((1, H_A, DK_A, DV_A), lambda b, c: (b, 0, 0, 0)),
                   pl.BlockSpec((1, H_B, DK_B, DV_B), lambda b, c: (b, 0, 0, 0))),
        scratch_shapes=[pltpu.VMEM((H_A, DK_A, DV_A), F32), pltpu.VMEM((H_B, DK_B, DV_B), F32)],
        compiler_params=pltpu.CompilerParams(dimension_semantics=("arbitrary", "arbitrary"),
                                             vmem_limit_bytes=VMEM_LIMIT),
        name="even_mixer_sample",
    )(z, lb, ng, cos, sin, s_hgrn, s_ret)


def _rope_tables(pos):
    half = DK_B // 2
    inv = ROPE_BASE ** (-jnp.arange(half, dtype=F32) / half)
    ang = pos.astype(F32)[:, None] * inv
    cos, sin = jnp.cos(ang), jnp.sin(ang)
    return jnp.concatenate([cos, cos], -1), jnp.concatenate([-sin, sin], -1)


ODD_PAD = 6144
O_Z, O_X, O_R, O_K, O_V, O_T = 0, DI_C, DI_C + CONV_DIM, DI_C + CONV_DIM + DI_D, DI_C + CONV_DIM + 2 * DI_D, \
    DI_C + CONV_DIM + 3 * DI_D
T_W = 384
T_DT = R_W + R_A + R_G
SHIFT_W = 3 * DI_D + T_W
HALF = 64


def _softplus(x):
    return jnp.maximum(x, 0.0) + jnp.log1p(jnp.exp(-jnp.abs(x)))


def _lane_lt(n, width):
    return lax.broadcasted_iota(jnp.int32, (1, width), 1) < n


def _pair_ones():
    r = lax.broadcasted_iota(jnp.int32, (2 * HALF, 2 * HALF), 0) < HALF
    c = lax.broadcasted_iota(jnp.int32, (2 * HALF, 2 * HALF), 1) < HALF
    return r == c


def _stack_pair(x, m_a):
    return jnp.concatenate([jnp.where(m_a, x, 0.0), jnp.where(m_a, 0.0, x)], axis=0)


def _rwkv_pairs(rs, ks_, vs_, als, bes, logws, Gs, bds, L):
    n = range(len(rs))
    m_a = _lane_lt(HALF, 2 * HALF)
    m_l = _lane_lt(L, 2 * L)
    ri = lax.broadcasted_iota(jnp.int32, (2 * L, 2 * L), 0)
    ci = lax.broadcasted_iota(jnp.int32, (2 * L, 2 * L), 1)
    cm = jnp.where(ci >= L, ci - L, ci)
    keep = cm < jnp.where(ri >= L, ri - L + 1, ri)
    eye = (ri == ci).astype(F32)
    blk4 = (ri // 4) == (ci // 4)
    pair_blk = _pair_ones()

    e_inv = [jnp.exp(-Gs[i]) for i in n]
    lhs = [jnp.concatenate([als[i] * jnp.exp(Gs[i] - logws[i]), rs[i] * jnp.exp(Gs[i])], axis=0).astype(BF16) for i in n]
    ks = [_stack_pair(ks_[i] * e_inv[i], m_a).astype(BF16) for i in n]
    bs = [_stack_pair(bes[i] * e_inv[i], m_a).astype(BF16) for i in n]
    vs = [_stack_pair(vs_[i], m_a).astype(BF16) for i in n]
    a_k = [jnp.where(keep, _dot_bf(lhs[i], ks[i], NT), 0.0) for i in n]
    a_b = [jnp.where(keep, _dot_bf(lhs[i], bs[i], NT), 0.0) for i in n]
    sb = [_dot_bf(lhs[i], bds[i], NT) for i in n]
    base = [sb[i] + _dot_bf(a_k[i], vs[i]) for i in n]
    n_bd = [jnp.concatenate([jnp.where(m_l, a_b[i][0:L], 0.0), jnp.where(m_l, 0.0, a_b[i][0:L])], axis=0) for i in n]
    n4 = [jnp.where(blk4, n_bd[i], 0.0) for i in n]
    n4sq = [_dot_bf(n4[i], n4[i]) for i in n]
    t_inv = [eye - n4[i] for i in n]
    t_inv = [t_inv[i] + _dot_bf(t_inv[i], n4sq[i]) for i in n]
    blk = 4
    while blk < L:
        off = ((ri // (2 * blk)) == (ci // (2 * blk))) & ((ri % (2 * blk)) >= blk) & ((ci % (2 * blk)) < blk)
        ct = [_dot_bf(jnp.where(off, n_bd[i], 0.0), t_inv[i]) for i in n]
        t_inv = [t_inv[i] - _dot_bf(t_inv[i], ct[i]) for i in n]
        blk *= 2
    u = [_dot_bf(t_inv[i][0:L, :] + t_inv[i][L:2 * L, :], _stack_pair(base[i][0:L], m_a)) for i in n]
    y = [base[i][L:2 * L] - _dot_bf(a_b[i][L:2 * L], _stack_pair(u[i], m_a)) for i in n]
    gl = [Gs[i][L - 1:L, :] for i in n]
    dec = [jnp.exp(gl[i] - Gs[i]) for i in n]
    upd = [_dot_bf(jnp.concatenate([vs_[i], u[i]], axis=0),
                   jnp.concatenate([ks_[i] * dec[i], -(bes[i] * dec[i])], axis=0), TN) for i in n]
    bd_new = [bds[i] * jnp.exp(gl[i]) + jnp.where(pair_blk, upd[i], 0.0) for i in n]
    return y, bd_new


def _ssd_pair(xdt, cg, bg, cb, bcs, brows, sp, L):
    m_a = _lane_lt(HALF, 2 * HALF)
    causal = _tril(L)
    out = jnp.zeros((L, 2 * HALF), F32)
    ends = [bc[L - 1:L, :] for bc in bcs]
    sp_new = sp * jnp.where(m_a, jnp.exp(ends[0]), jnp.exp(ends[1]))
    for x in range(2):
        keep = m_a if x == 0 else jnp.logical_not(m_a)
        seg = jnp.where(causal, jnp.exp(bcs[x] - brows[x]), 0.0)
        xm = jnp.where(keep, xdt, 0.0)
        out = out + _dot_bf(cb * seg, xm) + _dot_bf(cg * jnp.exp(bcs[x]), jnp.where(keep, sp, 0.0))
        sp_new = sp_new + _dot_bf(bg * jnp.exp(ends[x] - bcs[x]), xm, TN)
    return out, sp_new


def _odd_kernel(*refs, L, nvalid, chunks, has_s0):
    if has_s0:
        (z_ref, vec_ref, cp_ref, tp_ref, w2_ref, a2_ref, g2_ref, ssm0_ref, wkv0_ref, conv0_ref, shift0_ref,
         o_ref, ssmo_ref, wkvo_ref, convo_ref, shifto_ref, ssm_s, wkv_s, conv_c, shift_c) = refs
    else:
        (z_ref, vec_ref, cp_ref, tp_ref, w2_ref, a2_ref, g2_ref,
         o_ref, ssmo_ref, wkvo_ref, convo_ref, shifto_ref, ssm_s, wkv_s, conv_c, shift_c) = refs
    c = pl.program_id(1)
    npair = H_C // 2

    @pl.when(c == 0)
    def _():
        if has_s0:
            zpad = jnp.zeros((HALF, HALF), F32)
            for p in range(npair):
                ssm_s[p] = jnp.concatenate([ssm0_ref[0, 2 * p], ssm0_ref[0, 2 * p + 1]], axis=1)
                wkv_s[p] = jnp.concatenate([jnp.concatenate([wkv0_ref[0, 2 * p], zpad], axis=1),
                                            jnp.concatenate([zpad, wkv0_ref[0, 2 * p + 1]], axis=1)], axis=0)
            conv_c[...] = conv0_ref[0]
            shift_c[...] = shift0_ref[0]
        else:
            ssm_s[...] = jnp.zeros_like(ssm_s)
            wkv_s[...] = jnp.zeros_like(wkv_s)
            conv_c[...] = jnp.zeros_like(conv_c)
            shift_c[...] = jnp.zeros_like(shift_c)

    lead = jnp.where(c == 0, P_LEAD, 0) if chunks > 1 else 0
    r_idx, valid = _row_info(L, lead, nvalid)
    m_a = _lane_lt(HALF, 2 * HALF)
    tril = _tril(L).astype(F32)
    triu = (lax.broadcasted_iota(jnp.int32, (L, L), 0) <= lax.broadcasted_iota(jnp.int32, (L, L), 1)).astype(F32)
    ones_blk = _pair_ones().astype(F32)

    xbc = jnp.where(valid, z_ref[:, O_X:O_X + CONV_DIM], 0.0)
    c8 = conv_c[...]
    r8 = lax.broadcasted_iota(jnp.int32, (8, 1), 0)
    conv = cp_ref[CONV_W:CONV_W + 1, :] + xbc * cp_ref[CONV_W - 1:CONV_W, :]
    for j in range(1, CONV_W):
        rolled = pltpu.roll(xbc, j, 0)
        head = jnp.where(r8 < j, pltpu.roll(c8, j, 0), rolled[0:8])
        sh = jnp.concatenate([head, rolled[8:]], axis=0) if L > 8 else head
        conv = conv + sh * cp_ref[CONV_W - 1 - j:CONV_W - j, :]
    conv_c[...] = xbc[L - 8:L]
    act = _silu(conv)
    xc, bm, cm = act[:, :DI_C], act[:, DI_C:DI_C + G_C * N_C], act[:, DI_C + G_C * N_C:]

    t_raw = z_ref[:, O_T:O_T + T_W]
    dt = jnp.where(valid, _softplus(t_raw + tp_ref[1:2, :])[:, T_DT:T_DT + H_C], 0.0)
    logf = dt * tp_ref[2:3, T_DT:T_DT + H_C]
    bc_all = _dot_sel(logf, tril, sel_first=True)
    brow_all = _dot_sel(logf, triu, TN)
    per_g = npair // G_C
    for g in range(G_C):
        cg = cm[:, g * N_C:(g + 1) * N_C]
        bg = bm[:, g * N_C:(g + 1) * N_C]
        cb = _dot_bf(cg, bg, NT)
        ys, ss = [], jnp.zeros((L, 1), F32)
        for pp in range(per_g):
            p = g * per_g + pp
            sl = slice(p * 2 * HALF, (p + 1) * 2 * HALF)
            h_a, h_b = 2 * p, 2 * p + 1
            xcp = xc[:, sl]
            xdt = xcp * jnp.where(m_a, dt[:, h_a:h_a + 1], dt[:, h_b:h_b + 1])
            o, sp_new = _ssd_pair(xdt, cg, bg, cb, [bc_all[:, h_a:h_a + 1], bc_all[:, h_b:h_b + 1]],
                                  [brow_all[h_a:h_a + 1, :], brow_all[h_b:h_b + 1, :]], ssm_s[p], L)
            ssm_s[p] = sp_new
            y = (o + xcp * vec_ref[0:1, sl]) * _silu(z_ref[:, O_Z + p * 2 * HALF:O_Z + (p + 1) * 2 * HALF])
            ys.append(y)
            ss = ss + jnp.sum(y * y, axis=-1, keepdims=True)
        scale = lax.rsqrt(ss / (DI_C // G_C) + RMS_EPS)
        for pp in range(per_g):
            p = g * per_g + pp
            sl = slice(p * 2 * HALF, (p + 1) * 2 * HALF)
            o_ref[:, sl] = ys[pp] * scale * vec_ref[1:2, sl]

    def shift_mix(p_raw, carry_row, mu):
        pv = jnp.where(valid, p_raw, 0.0)
        prev = jnp.where(r_idx == 0, carry_row, pltpu.roll(pv, 1, 0))
        return pv, pv + (prev - pv) * mu

    r_raw, r = shift_mix(z_ref[:, O_R:O_R + DI_D], shift_c[7:8, 0:DI_D], vec_ref[2:3, :])
    k_raw, k = shift_mix(z_ref[:, O_K:O_K + DI_D], shift_c[7:8, DI_D:2 * DI_D], vec_ref[3:4, :])
    v_raw, v = shift_mix(z_ref[:, O_V:O_V + DI_D], shift_c[7:8, 2 * DI_D:3 * DI_D], vec_ref[4:5, :])
    t_rawm, tm = shift_mix(t_raw, shift_c[7:8, 3 * DI_D:SHIFT_W], tp_ref[0:1, :])
    shift_c[:, 0:DI_D] = r_raw[L - 8:L]
    shift_c[:, DI_D:2 * DI_D] = k_raw[L - 8:L]
    shift_c[:, 2 * DI_D:3 * DI_D] = v_raw[L - 8:L]
    shift_c[:, 3 * DI_D:SHIFT_W] = t_rawm[L - 8:L]

    th_hi, th_lo = _parts(jnp.tanh(tm), 2)
    w2_hi, w2_lo = _parts(w2_ref[...], 2)
    w_pre = vec_ref[5:6, :] + (jnp.dot(th_hi, w2_hi, preferred_element_type=F32) +
                               jnp.dot(th_lo, w2_hi, preferred_element_type=F32) +
                               jnp.dot(th_hi, w2_lo, preferred_element_type=F32))
    logw = jnp.where(valid, -jnp.exp(-_softplus(-w_pre) - 0.5), 0.0)
    g_cum = _dot_sel(logw, tril, sel_first=True)
    a = _sigmoid(vec_ref[6:7, :] + _dot_bf(tm, a2_ref[...]))
    gate = _dot_bf(_sigmoid(tm), g2_ref[...])
    kkr = k * vec_ref[7:8, :]
    k2 = k * (1.0 + (a - 1.0) * vec_ref[8:9, :])
    rk = r * k2 * vec_ref[9:10, :]
    pairs = range(npair)
    sls = [slice(p * 2 * HALF, (p + 1) * 2 * HALF) for p in pairs]
    nrm = [jnp.sqrt(_dot_sel(kkr[:, sl] * kkr[:, sl], ones_blk, n=2)) for sl in sls]
    al = [jnp.where(valid, kkr[:, sls[p]] / jnp.maximum(nrm[p], 1e-12), 0.0) for p in pairs]
    be = [al[p] * a[:, sls[p]] for p in pairs]
    ys, bd_new = _rwkv_pairs([r[:, sl] for sl in sls], [jnp.where(valid, k2[:, sl], 0.0) for sl in sls],
                             [v[:, sl] for sl in sls], al, be, [logw[:, sl] for sl in sls],
                             [g_cum[:, sl] for sl in sls], [wkv_s[p] for p in pairs], L)
    for p in pairs:
        wkv_s[p] = bd_new[p]
    mu = [_dot_sel(ys[p], ones_blk, n=2) * (1.0 / P_D) for p in pairs]
    yc = [ys[p] - mu[p] for p in pairs]
    var = [_dot_sel(yc[p] * yc[p], ones_blk, n=2) * (1.0 / P_D) for p in pairs]
    bonus = [_dot_sel(rk[:, sl], ones_blk, n=2) for sl in sls]
    for p in pairs:
        sl = sls[p]
        yn = yc[p] * lax.rsqrt(var[p] + RWKV_GN_EPS) * vec_ref[10:11, sl] + vec_ref[11:12, sl]
        o_ref[:, DI_C + p * 2 * HALF:DI_C + (p + 1) * 2 * HALF] = (yn + bonus[p] * v[:, sl]) * gate[:, sl]

    @pl.when(c == chunks - 1)
    def _():
        for p in range(npair):
            sp = ssm_s[p]
            ssmo_ref[0, 2 * p] = sp[:, 0:HALF]
            ssmo_ref[0, 2 * p + 1] = sp[:, HALF:2 * HALF]
            bd = wkv_s[p]
            wkvo_ref[0, 2 * p] = bd[0:HALF, 0:HALF]
            wkvo_ref[0, 2 * p + 1] = bd[HALF:2 * HALF, HALF:2 * HALF]
        convo_ref[0] = conv_c[...]
        shifto_ref[0] = shift_c[...]


def _odd_call(z, params, states, *, L, nvalid, nb, chunks, name, lead_blk0=None):
    has_s0 = states is not None
    kern = functools.partial(_odd_kernel, L=L, nvalid=nvalid, chunks=chunks, has_s0=has_s0)
    npair = H_C // 2
    if z.ndim == 2:
        z_spec = pl.BlockSpec((L, ODD_PAD), _prompt_block(chunks, lead_blk0))
        o_spec = pl.BlockSpec((L, ODD_MIX), _prompt_block(chunks, lead_blk0))
        o_shape = jax.ShapeDtypeStruct((z.shape[0], ODD_MIX), F32)
    else:
        z_spec = pl.BlockSpec((None, L, ODD_PAD), lambda b, c: (b, 0, 0))
        o_spec = pl.BlockSpec((None, L, ODD_MIX), lambda b, c: (b, 0, 0))
        o_shape = jax.ShapeDtypeStruct((nb, L, ODD_MIX), F32)
    const2 = lambda b, c: (0, 0)
    per_b = lambda b, c: (b, 0, 0, 0)
    per_b3 = lambda b, c: (b, 0, 0)
    in_specs = [z_spec] + [pl.BlockSpec(p.shape, const2) for p in params]
    args = [z] + list(params)
    st_specs = [pl.BlockSpec((1, H_C, N_C, P_C), per_b), pl.BlockSpec((1, H_D, P_D, P_D), per_b),
                pl.BlockSpec((1, 8, CONV_DIM), per_b3), pl.BlockSpec((1, 8, SHIFT_W), per_b3)]
    if has_s0:
        in_specs += st_specs
        args += list(states)
    return pl.pallas_call(
        kern,
        out_shape=(o_shape,
                   jax.ShapeDtypeStruct((nb, H_C, N_C, P_C), F32), jax.ShapeDtypeStruct((nb, H_D, P_D, P_D), F32),
                   jax.ShapeDtypeStruct((nb, 8, CONV_DIM), F32), jax.ShapeDtypeStruct((nb, 8, SHIFT_W), F32)),
        grid=(nb, chunks),
        in_specs=in_specs,
        out_specs=tuple([o_spec] + st_specs),
        scratch_shapes=[pltpu.VMEM((npair, N_C, 2 * HALF), F32), pltpu.VMEM((npair, 2 * HALF, 2 * HALF), F32),
                        pltpu.VMEM((8, CONV_DIM), F32), pltpu.VMEM((8, SHIFT_W), F32)],
        compiler_params=pltpu.CompilerParams(dimension_semantics=("arbitrary", "arbitrary"),
                                             vmem_limit_bytes=VMEM_LIMIT),
        name=name,
    )(*args)


def odd_mixer_prompt(z, params, *, nb=BATCH, chunks=P_CHUNKS, lead_blk0=L_ROW0 // CHUNK):
    return _odd_call(z, params, None, L=CHUNK, nvalid=CHUNK, nb=nb, chunks=chunks,
                     name="odd_mixer_prompt", lead_blk0=lead_blk0)


def odd_mixer_sample(z, params, states):
    return _odd_call(z, params, states, L=S_LEN, nvalid=DEC_SEQ, nb=z.shape[0], chunks=1,
                     name="odd_mixer_sample")


def _odd_params(odd_w_in, conv_w, conv_b, dt_bias, a_log, d_skip, ssm_norm_g, shift_mu, rwkv_w0, rwkv_w2, rwkv_a0,
                rwkv_a2, rwkv_g2, rwkv_k_k, rwkv_k_a, rwkv_r_k, lnx_g, lnx_b):
    o_dt = DI_C + CONV_DIM
    o_rw = o_dt + H_C
    w = jnp.concatenate([odd_w_in[:, :o_dt], odd_w_in[:, o_rw:], odd_w_in[:, o_dt:o_rw],
                         jnp.zeros((D_MODEL, ODD_PAD - ODD_IN), odd_w_in.dtype)], axis=1)
    vec = jnp.stack([jnp.repeat(d_skip, P_C), ssm_norm_g, shift_mu[:DI_D], shift_mu[DI_D:2 * DI_D],
                     shift_mu[2 * DI_D:3 * DI_D], rwkv_w0, rwkv_a0, rwkv_k_k, rwkv_k_a, rwkv_r_k.reshape(-1),
                     lnx_g, lnx_b] + [jnp.zeros((DI_D,), F32)] * 4)
    cpack = jnp.concatenate([conv_w, conv_b[None], jnp.zeros((3, CONV_DIM), F32)], axis=0)
    zt = jnp.zeros((T_W,), F32)
    tpack = jnp.stack([zt.at[:T_DT].set(shift_mu[3 * DI_D:]), zt.at[T_DT:T_DT + H_C].set(dt_bias),
                       zt.at[T_DT:T_DT + H_C].set(-jnp.exp(a_log.astype(F32)))] + [zt] * 5)
    zw = jnp.zeros((T_W, DI_D), F32)
    w2p = zw.at[:R_W].set(rwkv_w2)
    a2p = zw.at[R_W:R_W + R_A].set(rwkv_a2).astype(BF16)
    g2p = zw.at[R_W + R_A:T_DT].set(rwkv_g2).astype(BF16)
    return w.astype(BF16), (vec, cpack, tpack, w2p, a2p, g2p)


def _top16_desc(cur):
    vals = []
    for _ in range(PEER_TOPK):
        m = jnp.max(cur, axis=0, keepdims=True)
        vals.append(m)
        cur = jnp.where(cur == m, -jnp.inf, cur)
    return vals


PEER_CAND = 112


def _peer_kernel(x_ref, xb_ref, wq_ref, sk_ref, u_ref, v_ref, g_ref, b_ref, o_ref, ob_ref,
                 q_s, s1_s, s2_s, e1_s, e2_s, tau_s, cand_s, w_s, h_s, p_s, pn_s, *, tm, te):
    assert te == 4 * PEER_KEYS
    s = pl.program_id(1)
    ns = pl.num_programs(1)
    nk = PEER_KEYS
    neg = -jnp.inf

    @pl.when(s == 0)
    def _route():
        q_s[...] = lax.dot_general(wq_ref[...], xb_ref[...], NT, preferred_element_type=F32)
        a_idx = lax.broadcasted_iota(jnp.int32, (PEER_TOPK, 1), 0)

        def head(h, carry):
            tops = []
            for c in range(2):
                row0 = pl.multiple_of((2 * h + c) * nk, nk)
                sc = jnp.dot(sk_ref[2 * h + c], q_s[pl.ds(row0, nk), :].astype(BF16), preferred_element_type=F32)
                if c == 0:
                    s1_s[h] = sc
                else:
                    s2_s[h] = sc
                tops.append(_top16_desc(sc))
            t1, t2 = tops
            t1_all = jnp.concatenate(t1, axis=0)
            t2_all = jnp.concatenate(t2, axis=0)
            for b in range(4):
                cand_s[b * PEER_TOPK:(b + 1) * PEER_TOPK, :] = jnp.where(a_idx < PEER_TOPK // (b + 1), t1_all + t2[b], neg)
            for a in range(3):
                ok = (a_idx >= 4) & (a_idx < PEER_TOPK // (a + 1))
                cand_s[(4 + a) * PEER_TOPK:(5 + a) * PEER_TOPK, :] = jnp.where(ok, t1[a] + t2_all, neg)
            best = _top16_desc(cand_s[...])
            mx = t1[0] + t2[0]
            z = jnp.zeros_like(mx)
            for m in best:
                z = z + jnp.exp(m - mx)
            tau_s[pl.ds(h, 1), :] = best[-1]
            e1_s[h] = jnp.exp(s1_s[h] - t1[0])
            e2_s[h] = jnp.exp(s2_s[h] - t2[0]) / z
            return carry

        lax.fori_loop(0, PEER_HEADS, head, 0)
        o_ref[...] = jnp.zeros_like(o_ref)
        p_s[...] = jnp.zeros_like(p_s)

    blk = jnp.minimum(s, ns - 2)
    nj = te // nk
    d = o_ref.shape[1]
    kp, cp = d // 4, d // 8

    def mm1(k):
        part = lax.dot_general(u_ref[:, k * kp:(k + 1) * kp], xb_ref[:, k * kp:(k + 1) * kp], NT,
                               preferred_element_type=F32)
        if k == 0:
            h_s[...] = part
        else:
            h_s[...] += part

    def wbuild(j):
        i1 = blk * nj + j
        w = jnp.zeros((nk, tm), F32)
        for h in range(PEER_HEADS):
            c = s2_s[h] + s1_s[h, pl.ds(i1, 1), :]
            w = w + jnp.where(c >= tau_s[h:h + 1, :], e2_s[h] * e1_s[h, pl.ds(i1, 1), :], 0.0)
        w_s[j * nk:(j + 1) * nk, :] = w

    def act(j):
        hj = h_s[j * nk:(j + 1) * nk, :]
        g = 0.5 * hj * (1.0 + lax.erf(hj * (2.0 ** -0.5)))
        pn_s[:, j * nk:(j + 1) * nk] = jnp.transpose(w_s[j * nk:(j + 1) * nk, :] * g).astype(BF16)

    def mm2(n):
        o_ref[:, n * cp:(n + 1) * cp] += jnp.dot(p_s[...], v_ref[:, n * cp:(n + 1) * cp], preferred_element_type=F32)

    mm1(0); wbuild(0); mm1(1); wbuild(1); mm1(2); wbuild(2); mm1(3)
    mm2(0); wbuild(3); mm2(1); act(0); mm2(2); act(1); mm2(3); act(2); mm2(4); act(3); mm2(5); mm2(6); mm2(7)
    p_s[...] = pn_s[...]

    @pl.when(s == ns - 1)
    def _fin():
        y = _ln_rows(ALPHA * x_ref[...] + o_ref[...], g_ref[...], b_ref[...])
        o_ref[...] = y
        ob_ref[...] = y.astype(BF16)


def peer_ln(x, xb, wq_t, sk, u, v, g, b, *, layer=0, tm=512, te=512):
    m, d = x.shape
    kern = functools.partial(_peer_kernel, tm=tm, te=te)
    const = dict(pipeline_mode=pl.Buffered(1))
    ne = PEER_EXPERTS // te
    route = pltpu.VMEM((PEER_HEADS, PEER_KEYS, tm), F32)
    return pl.pallas_call(
        kern,
        out_shape=(jax.ShapeDtypeStruct((m, d), F32), jax.ShapeDtypeStruct((m, d), BF16)),
        grid=(m // tm, ne + 1),
        in_specs=[pl.BlockSpec((tm, d), lambda i, s: (i, 0), **const),
                  pl.BlockSpec((tm, d), lambda i, s: (i, 0), **const),
                  pl.BlockSpec(wq_t.shape, lambda i, s: (0, 0), **const),
                  pl.BlockSpec(sk.shape, lambda i, s: (0, 0, 0), **const),
                  pl.BlockSpec((None, te, d), lambda i, s: (layer, jnp.minimum(s, ne - 1), 0)),
                  pl.BlockSpec((None, te, d), lambda i, s: (layer, jnp.maximum(s - 1, 0), 0)),
                  pl.BlockSpec((1, d), lambda i, s: (0, 0), **const),
                  pl.BlockSpec((1, d), lambda i, s: (0, 0), **const)],
        out_specs=(pl.BlockSpec((tm, d), lambda i, s: (i, 0)), pl.BlockSpec((tm, d), lambda i, s: (i, 0))),
        scratch_shapes=[pltpu.VMEM((PEER_HEADS * PEER_QDIM, tm), F32), route, route, route, route,
                        pltpu.VMEM((PEER_HEADS, tm), F32), pltpu.VMEM((PEER_CAND, tm), F32),
                        pltpu.VMEM((te, tm), F32), pltpu.VMEM((te, tm), F32),
                        pltpu.VMEM((tm, te), BF16), pltpu.VMEM((tm, te), BF16)],
        compiler_params=pltpu.CompilerParams(dimension_semantics=("arbitrary", "arbitrary"),
                                             vmem_limit_bytes=PEER_VMEM_LIMIT),
        name="peer_ln",
    )(x, xb, wq_t, sk, u, v, g.reshape(1, d), b.reshape(1, d))


def _sample_rows(z):
    f = z.shape[1]
    zs = lax.slice(z, (S_ROW0, 0), (S_ROW0 + DEC_BATCH * DEC_SEQ, f)).reshape(DEC_BATCH, DEC_SEQ, f)
    return jnp.pad(zs, ((0, 0), (0, S_LEN - DEC_SEQ), (0, 0)))


def _merge_rows(buf, sample_out):
    f = buf.shape[1]
    buf = lax.dynamic_update_slice(buf, sample_out[:, :DEC_SEQ].reshape(DEC_BATCH * DEC_SEQ, f), (S_ROW0, 0))
    pad0 = L_ROW0 + BATCH * CHUNK
    return lax.dynamic_update_slice(buf, jnp.zeros((M_PAD - pad0, f), buf.dtype), (pad0, 0))


def kernel(x_prompt, x_sample, state_hgrn, state_ret, state_ssm, state_conv, state_wkv, state_shift, meta_tokens, ln_g, ln_b, even_w_in, hgrn_lb_logits, hgrn_norm_g, even_w_out, odd_w_in, conv_w, conv_b, dt_bias, a_log, d_skip, ssm_norm_g, shift_mu, rwkv_w0, rwkv_w2, rwkv_a0, rwkv_a2, rwkv_g2, rwkv_k_k, rwkv_k_a, rwkv_r_k, lnx_g, lnx_b, odd_w_out, peer_w_query, peer_sub_keys, peer_u, peer_v):
    dt = x_prompt.dtype
    lead = jnp.concatenate([jnp.zeros((P_LEAD, D_MODEL), dt), meta_tokens.astype(dt)], axis=0)
    x = jnp.concatenate([x_prompt.reshape(S_ROW0, D_MODEL), x_sample.reshape(DEC_BATCH * DEC_SEQ, D_MODEL)] +
                        [lead] * BATCH + [jnp.zeros((M_PAD - L_ROW0 - BATCH * CHUNK, D_MODEL), dt)], axis=0)
    xb = x.astype(BF16)
    peer_ub, peer_vb = peer_u.astype(BF16), peer_v.astype(BF16)

    cos_p, sin_p = _rope_tables(jnp.arange(P_ROWS) - P_LEAD)
    cos_s, sin_s = _rope_tables(PAST_LEN + jnp.arange(S_LEN))
    lb_table = jnp.cumsum(jax.nn.softmax(hgrn_lb_logits.astype(F32), axis=0), axis=0)

    z = matmul(xb, even_w_in[0].astype(BF16), tm=TOK_TILE, tn=1024, name="even_in")
    lb = lb_table[0].reshape(1, -1)
    ng = hgrn_norm_g[0].reshape(1, -1)
    mix_p, hgrn_p, ret_p = even_mixer_prompt(z, lb, ng, cos_p, sin_p)
    mix_s, hgrn_s, ret_s = even_mixer_sample(_sample_rows(z), lb, ng, cos_s, sin_s, state_hgrn[0], state_ret[0])
    mix = _merge_rows(mix_p, mix_s)
    x, xb = proj_ln(x, mix, even_w_out[0].astype(BF16), ln_g[0, 0], ln_b[0, 0], tm=TOK_TILE, name="even_out")
    x, xb = peer_ln(x, xb, peer_w_query[0].T.astype(BF16),
                    peer_sub_keys[0].reshape(2 * PEER_HEADS, PEER_KEYS, PEER_QDIM // 2).astype(BF16),
                    peer_ub, peer_vb, ln_g[0, 1], ln_b[0, 1], layer=0)

    w_in1, params = _odd_params(odd_w_in[0], conv_w[0], conv_b[0], dt_bias[0], a_log[0], d_skip[0], ssm_norm_g[0],
                                shift_mu[0], rwkv_w0[0], rwkv_w2[0], rwkv_a0[0], rwkv_a2[0], rwkv_g2[0], rwkv_k_k[0],
                                rwkv_k_a[0], rwkv_r_k[0], lnx_g[0], lnx_b[0])
    z = matmul(xb, w_in1, tm=TOK_TILE, tn=1024, name="odd_in")
    mix_p, ssm_p, wkv_p, conv_p, shift_p = odd_mixer_prompt(z, params)
    conv8 = jnp.pad(state_conv[0], ((0, 0), (8 - (CONV_W - 1), 0), (0, 0)))
    shift8 = jnp.pad(state_shift[0][:, None, :], ((0, 0), (7, 0), (0, SHIFT_W - SHIFT_DIM)))
    mix_s, ssm_s, wkv_s, conv_s, shift_s = odd_mixer_sample(_sample_rows(z), params,
                                                            (state_ssm[0], state_wkv[0], conv8, shift8))
    mix = _merge_rows(mix_p, mix_s)
    x, xb = proj_ln(x, mix, odd_w_out[0].astype(BF16), ln_g[1, 0], ln_b[1, 0], tm=TOK_TILE, name="odd_out")
    x, xb = peer_ln(x, xb, peer_w_query[1].T.astype(BF16),
                    peer_sub_keys[1].reshape(2 * PEER_HEADS, PEER_KEYS, PEER_QDIM // 2).astype(BF16),
                    peer_ub, peer_vb, ln_g[1, 1], ln_b[1, 1], layer=1)

    y_prompt = x[:S_ROW0].reshape(BATCH, SEQ, D_MODEL)
    y_sample = x[S_ROW0:L_ROW0].reshape(DEC_BATCH, DEC_SEQ, D_MODEL)
    nc = CONV_W - 1
    return (y_prompt, y_sample, hgrn_p[None], hgrn_s[None], ret_p[None], ret_s[None], ssm_p[None], ssm_s[None],
            conv_p[None, :, 8 - nc:], conv_s[None, :, DEC_SEQ - nc:DEC_SEQ], wkv_p[None], wkv_s[None],
            shift_p[None, :, 7, :SHIFT_DIM], shift_s[None, :, DEC_SEQ - 1, :SHIFT_DIM])
```

```python
import functools
import math

import jax
import jax.numpy as jnp
from jax import lax
from jax.experimental import pallas as pl
from jax.experimental.pallas import tpu as pltpu

D_MODEL = 2048
BATCH = 4
SEQ = 2048
DEPTH = 2
DEC_BATCH = 128
DEC_SEQ = 4
PAST_LEN = 16384
N_META = 16
CHUNK = 64

H_A, DK_A, DV_A = 8, 128, 128
H_B, DK_B, DV_B = 4, 128, 256
EVEN_IN = 4 * H_A * DK_A + 2 * H_B * DK_B + 2 * H_B * DV_B
EVEN_MIX = H_A * DV_A + H_B * DV_B

H_C, P_C, N_C, G_C, CONV_W = 16, 64, 128, 2, 4
DI_C = H_C * P_C
CONV_DIM = DI_C + 2 * G_C * N_C
H_D, P_D = 16, 64
DI_D = H_D * P_D
R_W, R_A, R_G = 64, 64, 160
SHIFT_DIM = 3 * DI_D + R_W + R_A + R_G
ODD_IN = DI_C + CONV_DIM + H_C + SHIFT_DIM
ODD_MIX = DI_C + DI_D

PEER_KEYS = 128
PEER_EXPERTS = PEER_KEYS * PEER_KEYS
PEER_HEADS = 8
PEER_TOPK = 16
PEER_QDIM = 256

ALPHA = (2.0 * DEPTH) ** 0.25
LN_EPS = 1e-5
RMS_EPS = 1e-6
RWKV_GN_EPS = 64e-5
ROPE_BASE = 10000.0

F32 = jnp.float32
BF16 = jnp.bfloat16

P_LEAD = CHUNK - N_META
P_ROWS = CHUNK + SEQ
P_CHUNKS = P_ROWS // CHUNK
S_ROW0 = BATCH * SEQ
L_ROW0 = S_ROW0 + DEC_BATCH * DEC_SEQ
S_LEN = 8
SAMPLE_GROUP = 4
ODD_SAMPLE_GROUP = 2
TOK_TILE = 512
M_PAD = -(-(L_ROW0 + BATCH * CHUNK) // 1024) * 1024

VMEM_LIMIT = 56 * 1024 * 1024
PEER_VMEM_LIMIT = 60 * 1024 * 1024


def _dot_bf(a, b, dims=(((1,), (0,)), ((), ()))):
    return lax.dot_general(a.astype(BF16), b.astype(BF16), dims, preferred_element_type=F32)


def _parts(x, n):
    out, rem = [], x
    for i in range(n):
        p = rem.astype(BF16)
        out.append(p)
        if i + 1 < n:
            rem = rem - p.astype(F32)
    return out


def _dot_sel(x, sel, dims=(((1,), (0,)), ((), ())), n=3, sel_first=False):
    sel = sel.astype(BF16)
    acc = None
    for p in _parts(x, n):
        d = lax.dot_general(sel, p, dims, preferred_element_type=F32) if sel_first else \
            lax.dot_general(p, sel, dims, preferred_element_type=F32)
        acc = d if acc is None else acc + d
    return acc


NT = (((1,), (1,)), ((), ()))
TN = (((0,), (0,)), ((), ()))


def _sigmoid(x):
    return 1.0 / (1.0 + jnp.exp(-x))


def _silu(x):
    return x * _sigmoid(x)


def _mm_kernel(x_ref, w_ref, o_ref):
    o_ref[...] = jnp.dot(x_ref[...].astype(BF16), w_ref[...], preferred_element_type=F32).astype(o_ref.dtype)


def matmul(x, w, *, tm, tn, out_dtype=F32, name="matmul"):
    m, k = x.shape
    n = w.shape[1]
    return pl.pallas_call(
        _mm_kernel,
        out_shape=jax.ShapeDtypeStruct((m, n), out_dtype),
        grid=(n // tn, m // tm),
        in_specs=[pl.BlockSpec((tm, k), lambda j, i: (i, 0)),
                  pl.BlockSpec((k, tn), lambda j, i: (0, j))],
        out_specs=pl.BlockSpec((tm, tn), lambda j, i: (i, j)),
        compiler_params=pltpu.CompilerParams(dimension_semantics=("arbitrary", "arbitrary"),
                                             vmem_limit_bytes=VMEM_LIMIT),
        name=name,
    )(x, w)


def _ln_rows(v, g, b):
    mu = jnp.mean(v, -1, keepdims=True)
    c = v - mu
    var = jnp.mean(c * c, -1, keepdims=True)
    return c * lax.rsqrt(var + LN_EPS) * g + b


def _proj_ln_kernel(x_ref, m_ref, w_ref, g_ref, b_ref, o_ref, ob_ref):
    acc = jnp.dot(m_ref[...].astype(BF16), w_ref[...], preferred_element_type=F32)
    y = _ln_rows(ALPHA * x_ref[...] + acc, g_ref[...], b_ref[...])
    o_ref[...] = y
    ob_ref[...] = y.astype(BF16)


def proj_ln(x, mix, w, g, b, *, tm, name="proj_ln"):
    m, d = x.shape
    k = mix.shape[1]
    return pl.pallas_call(
        _proj_ln_kernel,
        out_shape=(jax.ShapeDtypeStruct((m, d), F32), jax.ShapeDtypeStruct((m, d), BF16)),
        grid=(m // tm,),
        in_specs=[pl.BlockSpec((tm, d), lambda i: (i, 0)),
                  pl.BlockSpec((tm, k), lambda i: (i, 0)),
                  pl.BlockSpec((k, d), lambda i: (0, 0)),
                  pl.BlockSpec((1, d), lambda i: (0, 0)),
                  pl.BlockSpec((1, d), lambda i: (0, 0))],
        out_specs=(pl.BlockSpec((tm, d), lambda i: (i, 0)), pl.BlockSpec((tm, d), lambda i: (i, 0))),
        compiler_params=pltpu.CompilerParams(dimension_semantics=("arbitrary",),
                                             vmem_limit_bytes=VMEM_LIMIT),
        name=name,
    )(x, mix, w, g.reshape(1, d), b.reshape(1, d))


def _row_info(L, lead, nvalid):
    r = lax.broadcasted_iota(jnp.int32, (L, 1), 0)
    return r, (r >= lead) & (r < lead + nvalid)


def _prompt_block(chunks, lead_blk0):
    return lambda b, c: (jnp.where(c == 0, lead_blk0 + b, b * (chunks - 1) + c - 1), 0)


def _tril(L, strict=False):
    r = lax.broadcasted_iota(jnp.int32, (L, L), 0)
    c = lax.broadcasted_iota(jnp.int32, (L, L), 1)
    return (c < r) if strict else (c <= r)


def _hgrn_heads(qs, ks, vs, logfs, s0s, L):
    n = range(len(qs))
    tril = _tril(L).astype(F32)
    ones_l = jnp.ones((L, DK_A), F32)
    ones_k = jnp.ones((DK_A, DK_A), BF16)
    b = [_dot_sel(logfs[i], tril, sel_first=True) for i in n]
    b_end_col = [_dot_sel(logfs[i], ones_l, TN) for i in n]
    o = [_dot_bf(qs[i] * jnp.exp(b[i]), s0s[i]) for i in n]
    upd = [_dot_bf(ks[i] * jnp.exp(b[i][L - 1:L, :] - b[i]), vs[i], TN) for i in n]
    s_new = [jnp.exp(b_end_col[i]) * s0s[i] + upd[i] for i in n]
    sub = min(16, L)
    t_idx = lax.broadcasted_iota(jnp.int32, (sub, 1), 0)
    outs = [[] for _ in n]
    for blk in range(L // sub):
        r0 = blk * sub
        acc = [o[i][r0:r0 + sub] for i in n]
        if blk > 0:
            a = [_dot_bf(qs[i][r0:r0 + sub] * jnp.exp(b[i][r0:r0 + sub] - b[i][r0 - 1:r0, :]),
                         ks[i][0:r0] * jnp.exp(b[i][r0 - 1:r0, :] - b[i][0:r0]), NT) for i in n]
            acc = [acc[i] + _dot_bf(a[i], vs[i][0:r0]) for i in n]
        prod = [jnp.concatenate(
            [qs[i][r0:r0 + sub] * ks[i][r0 + s:r0 + s + 1, :] *
             jnp.exp(jnp.where(t_idx >= s, b[i][r0:r0 + sub] - b[i][r0 + s:r0 + s + 1, :], -jnp.inf))
             for s in range(sub)], axis=0).astype(BF16) for i in n]
        score = [jnp.dot(prod[i], ones_k, preferred_element_type=F32) for i in n]
        for i in n:
            t = acc[i]
            for s in range(sub):
                t = t + score[i][s * sub:(s + 1) * sub, :] * vs[i][r0 + s:r0 + s + 1, :]
            outs[i].append(t)
    return [jnp.concatenate(x, axis=0) if len(x) > 1 else x[0] for x in outs], s_new


def _even_kernel(*refs, L, nvalid, chunks, has_s0, G):
    if has_s0:
        z_ref, lb_ref, ng_ref, cos_ref, sin_ref, sh0_ref, sr0_ref, o_ref, sho_ref, sro_ref, sh_s, sr_s = refs
    else:
        z_ref, lb_ref, ng_ref, cos_ref, sin_ref, o_ref, sho_ref, sro_ref, sh_s, sr_s = refs
    c = pl.program_id(1)

    @pl.when(c == 0)
    def _():
        if has_s0:
            for g in range(G):
                sh_s[g * H_A:(g + 1) * H_A] = sh0_ref[g]
                sr_s[g * H_B:(g + 1) * H_B] = sr0_ref[g]
        else:
            sh_s[...] = jnp.zeros_like(sh_s)
            sr_s[...] = jnp.zeros_like(sr_s)

    lead = jnp.where(c == 0, P_LEAD, 0) if chunks > 1 else 0
    r, valid = _row_info(L, lead, nvalid)
    cnt_col = jnp.clip(r + 1 - lead, 0, nvalid).astype(F32)
    rr = lax.broadcasted_iota(jnp.int32, (1, L), 1)
    cnt_row = jnp.clip(rr + 1 - lead, 0, nvalid).astype(F32)

    off_q, off_f, off_i, off_g = 0, H_A * DK_A, 2 * H_A * DK_A, 2 * H_A * DK_A + H_A * DV_A
    base_b = 2 * H_A * DK_A + 2 * H_A * DV_A
    off_bq, off_bk, off_bv = base_b, base_b + H_B * DK_B, base_b + 2 * H_B * DK_B
    off_bg = off_bv + H_B * DV_B
    ng = ng_ref[...]
    zv = [z_ref] if len(z_ref.shape) == 2 else [z_ref.at[g] for g in range(G)]
    ov = [o_ref] if len(o_ref.shape) == 2 else [o_ref.at[g] for g in range(G)]
    ia = [(g, h) for g in range(G) for h in range(H_A)]
    ib = [(g, h) for g in range(G) for h in range(H_B)]
    na, nb_ = range(len(ia)), range(len(ib))
    af = [zv[g][:, off_f + h * DK_A: off_f + (h + 1) * DK_A] for g, h in ia]
    lbs = [lb_ref[:, h * DK_A:(h + 1) * DK_A] for g, h in ia]
    logf = [jnp.where(valid, jnp.log(lbs[i] + (1.0 - lbs[i]) * _sigmoid(af[i])), 0.0) for i in na]
    ka = [jnp.where(valid, (1.0 - lbs[i]) * _sigmoid(-af[i]), 0.0) for i in na]
    qa = [_silu(zv[g][:, off_q + h * DK_A: off_q + (h + 1) * DK_A]) for g, h in ia]
    va = [zv[g][:, off_i + h * DV_A: off_i + (h + 1) * DV_A] for g, h in ia]
    cosv, sinv = cos_ref[...], sin_ref[...]
    bq = [zv[g][:, off_bq + h * DK_B: off_bq + (h + 1) * DK_B] for g, h in ib]
    bk = [zv[g][:, off_bk + h * DK_B: off_bk + (h + 1) * DK_B] for g, h in ib]
    vb = [zv[g][:, off_bv + h * DV_B: off_bv + (h + 1) * DV_B] for g, h in ib]
    qb = [bq[i] * cosv + pltpu.roll(bq[i], DK_B // 2, 1) * sinv for i in nb_]
    kb = [jnp.where(valid, (bk[i] * cosv + pltpu.roll(bk[i], DK_B // 2, 1) * sinv) * (DK_B ** -0.5), 0.0) for i in nb_]
    lg = [math.log(1.0 - 2.0 ** (-5.0 - h)) for g, h in ib]

    oa, sa_new = _hgrn_heads(qa, ka, va, logf, [sh_s[i] for i in na], L)
    causal = _tril(L)
    cnt_end = cnt_col[L - 1:L, :]
    scores = [_dot_bf(qb[i], kb[i], NT) * jnp.where(causal, jnp.exp((cnt_col - cnt_row) * lg[i]), 0.0) for i in nb_]
    inter = [_dot_bf(qb[i] * jnp.exp(cnt_col * lg[i]), sr_s[i]) for i in nb_]
    ob = [_dot_bf(scores[i], vb[i]) + inter[i] for i in nb_]
    updb = [_dot_bf(kb[i] * jnp.exp((cnt_end - cnt_col) * lg[i]), vb[i], TN) for i in nb_]
    for i, (g, h) in enumerate(ia):
        sh_s[i] = sa_new[i]
        o = oa[i] * lax.rsqrt(jnp.mean(oa[i] * oa[i], -1, keepdims=True) + RMS_EPS) * ng
        ov[g][:, h * DV_A:(h + 1) * DV_A] = o * _silu(zv[g][:, off_g + h * DV_A: off_g + (h + 1) * DV_A])
    for i, (g, h) in enumerate(ib):
        sr_s[i] = jnp.exp(cnt_end * lg[i]) * sr_s[i] + updb[i]
        mu = jnp.mean(ob[i], -1, keepdims=True)
        cc = ob[i] - mu
        o = cc * lax.rsqrt(jnp.mean(cc * cc, -1, keepdims=True) + LN_EPS)
        ov[g][:, H_A * DV_A + h * DV_B: H_A * DV_A + (h + 1) * DV_B] = \
            o * _silu(zv[g][:, off_bg + h * DV_B: off_bg + (h + 1) * DV_B])

    @pl.when(c == chunks - 1)
    def _():
        for g in range(G):
            sho_ref[g] = sh_s[g * H_A:(g + 1) * H_A]
            sro_ref[g] = sr_s[g * H_B:(g + 1) * H_B]


def even_mixer_prompt(z, lb, ng, cos, sin, *, nb=BATCH, chunks=P_CHUNKS, lead_blk0=L_ROW0 // CHUNK):
    kern = functools.partial(_even_kernel, L=CHUNK, nvalid=CHUNK, chunks=chunks, has_s0=False, G=1)
    blk = _prompt_block(chunks, lead_blk0)
    return pl.pallas_call(
        kern,
        out_shape=(jax.ShapeDtypeStruct((z.shape[0], EVEN_MIX), F32),
                   jax.ShapeDtypeStruct((nb, H_A, DK_A, DV_A), F32),
                   jax.ShapeDtypeStruct((nb, H_B, DK_B, DV_B), F32)),
        grid=(nb, chunks),
        in_specs=[pl.BlockSpec((CHUNK, EVEN_IN), blk),
                  pl.BlockSpec((1, H_A * DK_A), lambda b, c: (0, 0)),
                  pl.BlockSpec((1, DV_A), lambda b, c: (0, 0)),
                  pl.BlockSpec((CHUNK, DK_B), lambda b, c: (c, 0)),
                  pl.BlockSpec((CHUNK, DK_B), lambda b, c: (c, 0))],
        out_specs=(pl.BlockSpec((CHUNK, EVEN_MIX), blk),
                   pl.BlockSpec((1, H_A, DK_A, DV_A), lambda b, c: (b, 0, 0, 0)),
                   pl.BlockSpec((1, H_B, DK_B, DV_B), lambda b, c: (b, 0, 0, 0))),
        scratch_shapes=[pltpu.VMEM((H_A, DK_A, DV_A), F32), pltpu.VMEM((H_B, DK_B, DV_B), F32)],
        compiler_params=pltpu.CompilerParams(dimension_semantics=("arbitrary", "arbitrary"),
                                             vmem_limit_bytes=VMEM_LIMIT),
        name="even_mixer_prompt",
    )(z, lb, ng, cos, sin)


def even_mixer_sample(z, lb, ng, cos, sin, s_hgrn, s_ret):
    nb = z.shape[0]
    G = SAMPLE_GROUP
    kern = functools.partial(_even_kernel, L=S_LEN, nvalid=DEC_SEQ, chunks=1, has_s0=True, G=G)
    return pl.pallas_call(
        kern,
        out_shape=(jax.ShapeDtypeStruct((nb, S_LEN, EVEN_MIX), F32),
                   jax.ShapeDtypeStruct((nb, H_A, DK_A, DV_A), F32),
                   jax.ShapeDtypeStruct((nb, H_B, DK_B, DV_B), F32)),
        grid=(nb // G, 1),
        in_specs=[pl.BlockSpec((G, S_LEN, EVEN_IN), lambda b, c: (b, 0, 0)),
                  pl.BlockSpec((1, H_A * DK_A), lambda b, c: (0, 0)),
                  pl.BlockSpec((1, DV_A), lambda b, c: (0, 0)),
                  pl.BlockSpec((S_LEN, DK_B), lambda b, c: (0, 0)),
                  pl.BlockSpec((S_LEN, DK_B), lambda b, c: (0, 0)),
                  pl.BlockSpec((G, H_A, DK_A, DV_A), lambda b, c: (b, 0, 0, 0)),
                  pl.BlockSpec((G, H_B, DK_B, DV_B), lambda b, c: (b, 0, 0, 0))],
        out_specs=(pl.BlockSpec((G, S_LEN, EVEN_MIX), lambda b, c: (b, 0, 0)),
                   pl.BlockSpec((G, H_A, DK_A, DV_A), lambda b, c: (b, 0, 0, 0)),
                   pl.BlockSpec((G, H_B, DK_B, DV_B), lambda b, c: (b, 0, 0, 0))),
        scratch_shapes=[pltpu.VMEM((G * H_A, DK_A, DV_A), F32), pltpu.VMEM((G * H_B, DK_B, DV_B), F32)],
        compiler_params=pltpu.CompilerParams(dimension_semantics=("arbitrary", "arbitrary"),
                                             vmem_limit_bytes=VMEM_LIMIT),
        name="even_mixer_sample",
    )(z, lb, ng, cos, sin, s_hgrn, s_ret)


def _rope_tables(pos):
    half = DK_B // 2
    inv = ROPE_BASE ** (-jnp.arange(half, dtype=F32) / half)
    ang = pos.astype(F32)[:, None] * inv
    cos, sin = jnp.cos(ang), jnp.sin(ang)
    return jnp.concatenate([cos, cos], -1), jnp.concatenate([-sin, sin], -1)


ODD_PAD = 6144
O_Z, O_X, O_R, O_K, O_V, O_T = 0, DI_C, DI_C + CONV_DIM, DI_C + CONV_DIM + DI_D, DI_C + CONV_DIM + 2 * DI_D, \
    DI_C + CONV_DIM + 3 * DI_D
T_W = 384
T_DT = R_W + R_A + R_G
SHIFT_W = 3 * DI_D + T_W
HALF = 64


def _softplus(x):
    return jnp.maximum(x, 0.0) + jnp.log1p(jnp.exp(-jnp.abs(x)))


def _lane_lt(n, width):
    return lax.broadcasted_iota(jnp.int32, (1, width), 1) < n


def _pair_ones():
    r = lax.broadcasted_iota(jnp.int32, (2 * HALF, 2 * HALF), 0) < HALF
    c = lax.broadcasted_iota(jnp.int32, (2 * HALF, 2 * HALF), 1) < HALF
    return r == c


def _stack_pair(x, m_a):
    return jnp.concatenate([jnp.where(m_a, x, 0.0), jnp.where(m_a, 0.0, x)], axis=0)


def _rwkv_pairs(rs, ks_, vs_, als, bes, logws, Gs, bds, L):
    n = range(len(rs))
    m_a = _lane_lt(HALF, 2 * HALF)
    m_l = _lane_lt(L, 2 * L)
    ri = lax.broadcasted_iota(jnp.int32, (2 * L, 2 * L), 0)
    ci = lax.broadcasted_iota(jnp.int32, (2 * L, 2 * L), 1)
    cm = jnp.where(ci >= L, ci - L, ci)
    keep = cm < jnp.where(ri >= L, ri - L + 1, ri)
    eye = (ri == ci).astype(F32)
    blk4 = (ri // 4) == (ci // 4)
    pair_blk = _pair_ones()

    e_inv = [jnp.exp(-Gs[i]) for i in n]
    lhs = [jnp.concatenate([als[i] * jnp.exp(Gs[i] - logws[i]), rs[i] * jnp.exp(Gs[i])], axis=0).astype(BF16) for i in n]
    ks = [_stack_pair(ks_[i] * e_inv[i], m_a).astype(BF16) for i in n]
    bs = [_stack_pair(bes[i] * e_inv[i], m_a).astype(BF16) for i in n]
    vs = [_stack_pair(vs_[i], m_a).astype(BF16) for i in n]
    a_k = [jnp.where(keep, _dot_bf(lhs[i], ks[i], NT), 0.0) for i in n]
    a_b = [jnp.where(keep, _dot_bf(lhs[i], bs[i], NT), 0.0) for i in n]
    sb = [_dot_bf(lhs[i], bds[i], NT) for i in n]
    base = [sb[i] + _dot_bf(a_k[i], vs[i]) for i in n]
    n_bd = [jnp.concatenate([jnp.where(m_l, a_b[i][0:L], 0.0), jnp.where(m_l, 0.0, a_b[i][0:L])], axis=0) for i in n]
    n4 = [jnp.where(blk4, n_bd[i], 0.0) for i in n]
    n4sq = [_dot_bf(n4[i], n4[i]) for i in n]
    t_inv = [eye - n4[i] for i in n]
    t_inv = [t_inv[i] + _dot_bf(t_inv[i], n4sq[i]) for i in n]
    blk = 4
    while blk < L:
        off = ((ri // (2 * blk)) == (ci // (2 * blk))) & ((ri % (2 * blk)) >= blk) & ((ci % (2 * blk)) < blk)
        ct = [_dot_bf(jnp.where(off, n_bd[i], 0.0), t_inv[i]) for i in n]
        t_inv = [t_inv[i] - _dot_bf(t_inv[i], ct[i]) for i in n]
        blk *= 2
    u = [_dot_bf(t_inv[i][0:L, :] + t_inv[i][L:2 * L, :], _stack_pair(base[i][0:L], m_a)) for i in n]
    y = [base[i][L:2 * L] - _dot_bf(a_b[i][L:2 * L], _stack_pair(u[i], m_a)) for i in n]
    gl = [Gs[i][L - 1:L, :] for i in n]
    dec = [jnp.exp(gl[i] - Gs[i]) for i in n]
    upd = [_dot_bf(jnp.concatenate([vs_[i], u[i]], axis=0),
                   jnp.concatenate([ks_[i] * dec[i], -(bes[i] * dec[i])], axis=0), TN) for i in n]
    bd_new = [bds[i] * jnp.exp(gl[i]) + jnp.where(pair_blk, upd[i], 0.0) for i in n]
    return y, bd_new


def _ssd_pair(xdt, cg, bg, cb, bcs, brows, sp, L):
    m_a = _lane_lt(HALF, 2 * HALF)
    causal = _tril(L)
    out = jnp.zeros((L, 2 * HALF), F32)
    ends = [bc[L - 1:L, :] for bc in bcs]
    sp_new = sp * jnp.where(m_a, jnp.exp(ends[0]), jnp.exp(ends[1]))
    for x in range(2):
        keep = m_a if x == 0 else jnp.logical_not(m_a)
        seg = jnp.where(causal, jnp.exp(bcs[x] - brows[x]), 0.0)
        xm = jnp.where(keep, xdt, 0.0)
        out = out + _dot_bf(cb * seg, xm) + _dot_bf(cg * jnp.exp(bcs[x]), jnp.where(keep, sp, 0.0))
        sp_new = sp_new + _dot_bf(bg * jnp.exp(ends[x] - bcs[x]), xm, TN)
    return out, sp_new


def _odd_kernel(*refs, L, nvalid, chunks, has_s0, G):
    if has_s0:
        (z_ref, vec_ref, cp_ref, tp_ref, w2_ref, a2_ref, g2_ref, ssm0_ref, wkv0_ref, conv0_ref, shift0_ref,
         o_ref, ssmo_ref, wkvo_ref, convo_ref, shifto_ref, ssm_s, wkv_s, conv_c, shift_c) = refs
    else:
        (z_ref, vec_ref, cp_ref, tp_ref, w2_ref, a2_ref, g2_ref,
         o_ref, ssmo_ref, wkvo_ref, convo_ref, shifto_ref, ssm_s, wkv_s, conv_c, shift_c) = refs
    c = pl.program_id(1)
    npair = H_C // 2

    @pl.when(c == 0)
    def _():
        if has_s0:
            zpad = jnp.zeros((HALF, HALF), F32)
            for g in range(G):
                for p in range(npair):
                    ssm_s[g * npair + p] = jnp.concatenate([ssm0_ref[g, 2 * p], ssm0_ref[g, 2 * p + 1]], axis=1)
                    wkv_s[g * npair + p] = jnp.concatenate(
                        [jnp.concatenate([wkv0_ref[g, 2 * p], zpad], axis=1),
                         jnp.concatenate([zpad, wkv0_ref[g, 2 * p + 1]], axis=1)], axis=0)
                conv_c[g * 8:(g + 1) * 8, :] = conv0_ref[g]
                shift_c[g * 8:(g + 1) * 8, :] = shift0_ref[g]
        else:
            ssm_s[...] = jnp.zeros_like(ssm_s)
            wkv_s[...] = jnp.zeros_like(wkv_s)
            conv_c[...] = jnp.zeros_like(conv_c)
            shift_c[...] = jnp.zeros_like(shift_c)

    lead = jnp.where(c == 0, P_LEAD, 0) if chunks > 1 else 0
    r_idx, valid = _row_info(L, lead, nvalid)
    m_a = _lane_lt(HALF, 2 * HALF)
    tril = _tril(L).astype(F32)
    triu = (lax.broadcasted_iota(jnp.int32, (L, L), 0) <= lax.broadcasted_iota(jnp.int32, (L, L), 1)).astype(F32)
    ones_blk = _pair_ones().astype(F32)
    r8 = lax.broadcasted_iota(jnp.int32, (8, 1), 0)
    zv = [z_ref] if len(z_ref.shape) == 2 else [z_ref.at[g] for g in range(G)]
    ov = [o_ref] if len(o_ref.shape) == 2 else [o_ref.at[g] for g in range(G)]
    pairs = range(npair)
    sls = [slice(p * 2 * HALF, (p + 1) * 2 * HALF) for p in pairs]
    w2_hi, w2_lo = _parts(w2_ref[...], 2)

    def shift_mix(p_raw, carry_row, mu):
        pv = jnp.where(valid, p_raw, 0.0)
        prev = jnp.where(r_idx == 0, carry_row, pltpu.roll(pv, 1, 0))
        return pv, pv + (prev - pv) * mu

    rw = []
    for g in range(G):
        zg, og = zv[g], ov[g]
        cs = slice(g * 8, (g + 1) * 8)
        xbc = jnp.where(valid, zg[:, O_X:O_X + CONV_DIM], 0.0)
        c8 = conv_c[cs, :]
        conv = cp_ref[CONV_W:CONV_W + 1, :] + xbc * cp_ref[CONV_W - 1:CONV_W, :]
        for j in range(1, CONV_W):
            rolled = pltpu.roll(xbc, j, 0)
            head = jnp.where(r8 < j, pltpu.roll(c8, j, 0), rolled[0:8])
            sh = jnp.concatenate([head, rolled[8:]], axis=0) if L > 8 else head
            conv = conv + sh * cp_ref[CONV_W - 1 - j:CONV_W - j, :]
        conv_c[cs, :] = xbc[L - 8:L]
        act = _silu(conv)
        xc, bm, cm = act[:, :DI_C], act[:, DI_C:DI_C + G_C * N_C], act[:, DI_C + G_C * N_C:]

        t_raw = zg[:, O_T:O_T + T_W]
        dt = jnp.where(valid, _softplus(t_raw + tp_ref[1:2, :])[:, T_DT:T_DT + H_C], 0.0)
        logf = dt * tp_ref[2:3, T_DT:T_DT + H_C]
        bc_all = _dot_sel(logf, tril, sel_first=True)
        brow_all = _dot_sel(logf, triu, TN)
        per_g = npair // G_C
        for grp in range(G_C):
            cg = cm[:, grp * N_C:(grp + 1) * N_C]
            bg = bm[:, grp * N_C:(grp + 1) * N_C]
            cb = _dot_bf(cg, bg, NT)
            ys, ss = [], jnp.zeros((L, 1), F32)
            for pp in range(per_g):
                p = grp * per_g + pp
                sl = sls[p]
                h_a, h_b = 2 * p, 2 * p + 1
                xcp = xc[:, sl]
                xdt = xcp * jnp.where(m_a, dt[:, h_a:h_a + 1], dt[:, h_b:h_b + 1])
                o, sp_new = _ssd_pair(xdt, cg, bg, cb, [bc_all[:, h_a:h_a + 1], bc_all[:, h_b:h_b + 1]],
                                      [brow_all[h_a:h_a + 1, :], brow_all[h_b:h_b + 1, :]], ssm_s[g * npair + p], L)
                ssm_s[g * npair + p] = sp_new
                y = (o + xcp * vec_ref[0:1, sl]) * _silu(zg[:, O_Z + p * 2 * HALF:O_Z + (p + 1) * 2 * HALF])
                ys.append(y)
                ss = ss + jnp.sum(y * y, axis=-1, keepdims=True)
            scale = lax.rsqrt(ss / (DI_C // G_C) + RMS_EPS)
            for pp in range(per_g):
                sl = sls[grp * per_g + pp]
                og[:, sl] = ys[pp] * scale * vec_ref[1:2, sl]

        r_raw, r = shift_mix(zg[:, O_R:O_R + DI_D], shift_c[g * 8 + 7:g * 8 + 8, 0:DI_D], vec_ref[2:3, :])
        k_raw, k = shift_mix(zg[:, O_K:O_K + DI_D], shift_c[g * 8 + 7:g * 8 + 8, DI_D:2 * DI_D], vec_ref[3:4, :])
        v_raw, v = shift_mix(zg[:, O_V:O_V + DI_D], shift_c[g * 8 + 7:g * 8 + 8, 2 * DI_D:3 * DI_D], vec_ref[4:5, :])
        t_rawm, tm = shift_mix(t_raw, shift_c[g * 8 + 7:g * 8 + 8, 3 * DI_D:SHIFT_W], tp_ref[0:1, :])
        shift_c[cs, 0:DI_D] = r_raw[L - 8:L]
        shift_c[cs, DI_D:2 * DI_D] = k_raw[L - 8:L]
        shift_c[cs, 2 * DI_D:3 * DI_D] = v_raw[L - 8:L]
        shift_c[cs, 3 * DI_D:SHIFT_W] = t_rawm[L - 8:L]

        th_hi, th_lo = _parts(jnp.tanh(tm), 2)
        w_pre = vec_ref[5:6, :] + (jnp.dot(th_hi, w2_hi, preferred_element_type=F32) +
                                   jnp.dot(th_lo, w2_hi, preferred_element_type=F32) +
                                   jnp.dot(th_hi, w2_lo, preferred_element_type=F32))
        logw = jnp.where(valid, -jnp.exp(-_softplus(-w_pre) - 0.5), 0.0)
        g_cum = _dot_sel(logw, tril, sel_first=True)
        a = _sigmoid(vec_ref[6:7, :] + _dot_bf(tm, a2_ref[...]))
        gate = _dot_bf(_sigmoid(tm), g2_ref[...])
        kkr = k * vec_ref[7:8, :]
        k2 = k * (1.0 + (a - 1.0) * vec_ref[8:9, :])
        rw.append(dict(r=r, v=v, a=a, gate=gate, kkr=kkr, k2=k2, rk=r * k2 * vec_ref[9:10, :], logw=logw, g_cum=g_cum))

    gp = [(g, p) for g in range(G) for p in pairs]
    n = range(len(gp))
    nrm = [jnp.sqrt(_dot_sel(rw[g]["kkr"][:, sls[p]] * rw[g]["kkr"][:, sls[p]], ones_blk, n=2)) for g, p in gp]
    al = [jnp.where(valid, rw[g]["kkr"][:, sls[p]] / jnp.maximum(nrm[i], 1e-12), 0.0) for i, (g, p) in enumerate(gp)]
    be = [al[i] * rw[g]["a"][:, sls[p]] for i, (g, p) in enumerate(gp)]
    ys, bd_new = _rwkv_pairs([rw[g]["r"][:, sls[p]] for g, p in gp],
                             [jnp.where(valid, rw[g]["k2"][:, sls[p]], 0.0) for g, p in gp],
                             [rw[g]["v"][:, sls[p]] for g, p in gp], al, be,
                             [rw[g]["logw"][:, sls[p]] for g, p in gp], [rw[g]["g_cum"][:, sls[p]] for g, p in gp],
                             [wkv_s[i] for i in n], L)
    for i in n:
        wkv_s[i] = bd_new[i]
    mu = [_dot_sel(ys[i], ones_blk, n=2) * (1.0 / P_D) for i in n]
    yc = [ys[i] - mu[i] for i in n]
    var = [_dot_sel(yc[i] * yc[i], ones_blk, n=2) * (1.0 / P_D) for i in n]
    bonus = [_dot_sel(rw[g]["rk"][:, sls[p]], ones_blk, n=2) for g, p in gp]
    for i, (g, p) in enumerate(gp):
        sl = sls[p]
        yn = yc[i] * lax.rsqrt(var[i] + RWKV_GN_EPS) * vec_ref[10:11, sl] + vec_ref[11:12, sl]
        ov[g][:, DI_C + p * 2 * HALF:DI_C + (p + 1) * 2 * HALF] = (yn + bonus[i] * rw[g]["v"][:, sl]) * rw[g]["gate"][:, sl]

    @pl.when(c == chunks - 1)
    def _():
        for g in range(G):
            for p in range(npair):
                sp = ssm_s[g * npair + p]
                ssmo_ref[g, 2 * p] = sp[:, 0:HALF]
                ssmo_ref[g, 2 * p + 1] = sp[:, HALF:2 * HALF]
                bd = wkv_s[g * npair + p]
                wkvo_ref[g, 2 * p] = bd[0:HALF, 0:HALF]
                wkvo_ref[g, 2 * p + 1] = bd[HALF:2 * HALF, HALF:2 * HALF]
            convo_ref[g] = conv_c[g * 8:(g + 1) * 8, :]
            shifto_ref[g] = shift_c[g * 8:(g + 1) * 8, :]


def _odd_call(z, params, states, *, L, nvalid, nb, chunks, name, lead_blk0=None, G=1):
    has_s0 = states is not None
    kern = functools.partial(_odd_kernel, L=L, nvalid=nvalid, chunks=chunks, has_s0=has_s0, G=G)
    npair = H_C // 2
    if z.ndim == 2:
        z_spec = pl.BlockSpec((L, ODD_PAD), _prompt_block(chunks, lead_blk0))
        o_spec = pl.BlockSpec((L, ODD_MIX), _prompt_block(chunks, lead_blk0))
        o_shape = jax.ShapeDtypeStruct((z.shape[0], ODD_MIX), F32)
    else:
        z_spec = pl.BlockSpec((G, L, ODD_PAD), lambda b, c: (b, 0, 0))
        o_spec = pl.BlockSpec((G, L, ODD_MIX), lambda b, c: (b, 0, 0))
        o_shape = jax.ShapeDtypeStruct((nb, L, ODD_MIX), F32)
    const2 = lambda b, c: (0, 0)
    per_b = lambda b, c: (b, 0, 0, 0)
    per_b3 = lambda b, c: (b, 0, 0)
    in_specs = [z_spec] + [pl.BlockSpec(p.shape, const2) for p in params]
    args = [z] + list(params)
    st_specs = [pl.BlockSpec((G, H_C, N_C, P_C), per_b), pl.BlockSpec((G, H_D, P_D, P_D), per_b),
                pl.BlockSpec((G, 8, CONV_DIM), per_b3), pl.BlockSpec((G, 8, SHIFT_W), per_b3)]
    if has_s0:
        in_specs += st_specs
        args += list(states)
    return pl.pallas_call(
        kern,
        out_shape=(o_shape,
                   jax.ShapeDtypeStruct((nb, H_C, N_C, P_C), F32), jax.ShapeDtypeStruct((nb, H_D, P_D, P_D), F32),
                   jax.ShapeDtypeStruct((nb, 8, CONV_DIM), F32), jax.ShapeDtypeStruct((nb, 8, SHIFT_W), F32)),
        grid=(nb // G, chunks),
        in_specs=in_specs,
        out_specs=tuple([o_spec] + st_specs),
        scratch_shapes=[pltpu.VMEM((G * npair, N_C, 2 * HALF), F32), pltpu.VMEM((G * npair, 2 * HALF, 2 * HALF), F32),
                        pltpu.VMEM((G * 8, CONV_DIM), F32), pltpu.VMEM((G * 8, SHIFT_W), F32)],
        compiler_params=pltpu.CompilerParams(dimension_semantics=("arbitrary", "arbitrary"),
                                             vmem_limit_bytes=VMEM_LIMIT),
        name=name,
    )(*args)


def odd_mixer_prompt(z, params, *, nb=BATCH, chunks=P_CHUNKS, lead_blk0=L_ROW0 // CHUNK):
    return _odd_call(z, params, None, L=CHUNK, nvalid=CHUNK, nb=nb, chunks=chunks,
                     name="odd_mixer_prompt", lead_blk0=lead_blk0)


def odd_mixer_sample(z, params, states):
    return _odd_call(z, params, states, L=S_LEN, nvalid=DEC_SEQ, nb=z.shape[0], chunks=1,
                     name="odd_mixer_sample", G=ODD_SAMPLE_GROUP)


def _odd_params(odd_w_in, conv_w, conv_b, dt_bias, a_log, d_skip, ssm_norm_g, shift_mu, rwkv_w0, rwkv_w2, rwkv_a0,
                rwkv_a2, rwkv_g2, rwkv_k_k, rwkv_k_a, rwkv_r_k, lnx_g, lnx_b):
    o_dt = DI_C + CONV_DIM
    o_rw = o_dt + H_C
    w = jnp.concatenate([odd_w_in[:, :o_dt], odd_w_in[:, o_rw:], odd_w_in[:, o_dt:o_rw],
                         jnp.zeros((D_MODEL, ODD_PAD - ODD_IN), odd_w_in.dtype)], axis=1)
    vec = jnp.stack([jnp.repeat(d_skip, P_C), ssm_norm_g, shift_mu[:DI_D], shift_mu[DI_D:2 * DI_D],
                     shift_mu[2 * DI_D:3 * DI_D], rwkv_w0, rwkv_a0, rwkv_k_k, rwkv_k_a, rwkv_r_k.reshape(-1),
                     lnx_g, lnx_b] + [jnp.zeros((DI_D,), F32)] * 4)
    cpack = jnp.concatenate([conv_w, conv_b[None], jnp.zeros((3, CONV_DIM), F32)], axis=0)
    zt = jnp.zeros((T_W,), F32)
    tpack = jnp.stack([zt.at[:T_DT].set(shift_mu[3 * DI_D:]), zt.at[T_DT:T_DT + H_C].set(dt_bias),
                       zt.at[T_DT:T_DT + H_C].set(-jnp.exp(a_log.astype(F32)))] + [zt] * 5)
    zw = jnp.zeros((T_W, DI_D), F32)
    w2p = zw.at[:R_W].set(rwkv_w2)
    a2p = zw.at[R_W:R_W + R_A].set(rwkv_a2).astype(BF16)
    g2p = zw.at[R_W + R_A:T_DT].set(rwkv_g2).astype(BF16)
    return w.astype(BF16), (vec, cpack, tpack, w2p, a2p, g2p)


def _top16_desc(cur):
    vals = []
    for _ in range(PEER_TOPK):
        m = jnp.max(cur, axis=0, keepdims=True)
        vals.append(m)
        cur = jnp.where(cur == m, -jnp.inf, cur)
    return vals


PEER_CAND = 112


def _peer_kernel(x_ref, xb_ref, wq_ref, sk_ref, u_ref, v_ref, g_ref, b_ref, o_ref, ob_ref,
                 q_s, s1_s, s2_s, e1_s, e2_s, tau_s, cand_s, w_s, h_s, p_s, pn_s, *, tm, te):
    assert te == 4 * PEER_KEYS
    s = pl.program_id(1)
    ns = pl.num_programs(1)
    nk = PEER_KEYS
    neg = -jnp.inf

    @pl.when(s == 0)
    def _route():
        q_s[...] = lax.dot_general(wq_ref[...], xb_ref[...], NT, preferred_element_type=F32)
        a_idx = lax.broadcasted_iota(jnp.int32, (PEER_TOPK, 1), 0)

        def head(h, carry):
            tops = []
            for c in range(2):
                row0 = pl.multiple_of((2 * h + c) * nk, nk)
                sc = jnp.dot(sk_ref[2 * h + c], q_s[pl.ds(row0, nk), :].astype(BF16), preferred_element_type=F32)
                if c == 0:
                    s1_s[h] = sc
                else:
                    s2_s[h] = sc
                tops.append(_top16_desc(sc))
            t1, t2 = tops
            t1_all = jnp.concatenate(t1, axis=0)
            t2_all = jnp.concatenate(t2, axis=0)
            for b in range(4):
                cand_s[b * PEER_TOPK:(b + 1) * PEER_TOPK, :] = jnp.where(a_idx < PEER_TOPK // (b + 1), t1_all + t2[b], neg)
            for a in range(3):
                ok = (a_idx >= 4) & (a_idx < PEER_TOPK // (a + 1))
                cand_s[(4 + a) * PEER_TOPK:(5 + a) * PEER_TOPK, :] = jnp.where(ok, t1[a] + t2_all, neg)
            best = _top16_desc(cand_s[...])
            mx = t1[0] + t2[0]
            z = jnp.zeros_like(mx)
            for m in best:
                z = z + jnp.exp(m - mx)
            tau_s[pl.ds(h, 1), :] = best[-1]
            e1_s[h] = jnp.exp(s1_s[h] - t1[0])
            e2_s[h] = jnp.exp(s2_s[h] - t2[0]) / z
            return carry

        lax.fori_loop(0, PEER_HEADS, head, 0)
        o_ref[...] = jnp.zeros_like(o_ref)
        p_s[...] = jnp.zeros_like(p_s)

    blk = jnp.minimum(s, ns - 2)
    nj = te // nk
    d = o_ref.shape[1]
    kp, cp = d // 4, d // 8

    def mm1(k):
        part = lax.dot_general(u_ref[:, k * kp:(k + 1) * kp], xb_ref[:, k * kp:(k + 1) * kp], NT,
                               preferred_element_type=F32)
        if k == 0:
            h_s[...] = part
        else:
            h_s[...] += part

    def wbuild(j):
        i1 = blk * nj + j
        w = jnp.zeros((nk, tm), F32)
        for h in range(PEER_HEADS):
            c = s2_s[h] + s1_s[h, pl.ds(i1, 1), :]
            w = w + jnp.where(c >= tau_s[h:h + 1, :], e2_s[h] * e1_s[h, pl.ds(i1, 1), :], 0.0)
        w_s[j * nk:(j + 1) * nk, :] = w

    def act(j):
        hj = h_s[j * nk:(j + 1) * nk, :]
        g = 0.5 * hj * (1.0 + lax.erf(hj * (2.0 ** -0.5)))
        pn_s[:, j * nk:(j + 1) * nk] = jnp.transpose(w_s[j * nk:(j + 1) * nk, :] * g).astype(BF16)

    def mm2(n):
        o_ref[:, n * cp:(n + 1) * cp] += jnp.dot(p_s[...], v_ref[:, n * cp:(n + 1) * cp], preferred_element_type=F32)

    mm1(0); wbuild(0); mm1(1); wbuild(1); mm1(2); wbuild(2); mm1(3)
    mm2(0); wbuild(3); mm2(1); act(0); mm2(2); act(1); mm2(3); act(2); mm2(4); act(3); mm2(5); mm2(6); mm2(7)
    p_s[...] = pn_s[...]

    @pl.when(s == ns - 1)
    def _fin():
        y = _ln_rows(ALPHA * x_ref[...] + o_ref[...], g_ref[...], b_ref[...])
        o_ref[...] = y
        ob_ref[...] = y.astype(BF16)


def peer_ln(x, xb, wq_t, sk, u, v, g, b, *, layer=0, tm=512, te=512):
    m, d = x.shape
    kern = functools.partial(_peer_kernel, tm=tm, te=te)
    const = dict(pipeline_mode=pl.Buffered(1))
    ne = PEER_EXPERTS // te
    route = pltpu.VMEM((PEER_HEADS, PEER_KEYS, tm), F32)
    return pl.pallas_call(
        kern,
        out_shape=(jax.ShapeDtypeStruct((m, d), F32), jax.ShapeDtypeStruct((m, d), BF16)),
        grid=(m // tm, ne + 1),
        in_specs=[pl.BlockSpec((tm, d), lambda i, s: (i, 0), **const),
                  pl.BlockSpec((tm, d), lambda i, s: (i, 0), **const),
                  pl.BlockSpec(wq_t.shape, lambda i, s: (0, 0), **const),
                  pl.BlockSpec(sk.shape, lambda i, s: (0, 0, 0), **const),
                  pl.BlockSpec((None, te, d), lambda i, s: (layer, jnp.minimum(s, ne - 1), 0)),
                  pl.BlockSpec((None, te, d), lambda i, s: (layer, jnp.maximum(s - 1, 0), 0)),
                  pl.BlockSpec((1, d), lambda i, s: (0, 0), **const),
                  pl.BlockSpec((1, d), lambda i, s: (0, 0), **const)],
        out_specs=(pl.BlockSpec((tm, d), lambda i, s: (i, 0)), pl.BlockSpec((tm, d), lambda i, s: (i, 0))),
        scratch_shapes=[pltpu.VMEM((PEER_HEADS * PEER_QDIM, tm), F32), route, route, route, route,
                        pltpu.VMEM((PEER_HEADS, tm), F32), pltpu.VMEM((PEER_CAND, tm), F32),
                        pltpu.VMEM((te, tm), F32), pltpu.VMEM((te, tm), F32),
                        pltpu.VMEM((tm, te), BF16), pltpu.VMEM((tm, te), BF16)],
        compiler_params=pltpu.CompilerParams(dimension_semantics=("arbitrary", "arbitrary"),
                                             vmem_limit_bytes=PEER_VMEM_LIMIT),
        name="peer_ln",
    )(x, xb, wq_t, sk, u, v, g.reshape(1, d), b.reshape(1, d))


def _sample_rows(z):
    f = z.shape[1]
    zs = lax.slice(z, (S_ROW0, 0), (S_ROW0 + DEC_BATCH * DEC_SEQ, f)).reshape(DEC_BATCH, DEC_SEQ, f)
    return jnp.pad(zs, ((0, 0), (0, S_LEN - DEC_SEQ), (0, 0)))


def _merge_rows(buf, sample_out):
    f = buf.shape[1]
    buf = lax.dynamic_update_slice(buf, sample_out[:, :DEC_SEQ].reshape(DEC_BATCH * DEC_SEQ, f), (S_ROW0, 0))
    pad0 = L_ROW0 + BATCH * CHUNK
    return lax.dynamic_update_slice(buf, jnp.zeros((M_PAD - pad0, f), buf.dtype), (pad0, 0))


def kernel(x_prompt, x_sample, state_hgrn, state_ret, state_ssm, state_conv, state_wkv, state_shift, meta_tokens, ln_g, ln_b, even_w_in, hgrn_lb_logits, hgrn_norm_g, even_w_out, odd_w_in, conv_w, conv_b, dt_bias, a_log, d_skip, ssm_norm_g, shift_mu, rwkv_w0, rwkv_w2, rwkv_a0, rwkv_a2, rwkv_g2, rwkv_k_k, rwkv_k_a, rwkv_r_k, lnx_g, lnx_b, odd_w_out, peer_w_query, peer_sub_keys, peer_u, peer_v):
    dt = x_prompt.dtype
    lead = jnp.concatenate([jnp.zeros((P_LEAD, D_MODEL), dt), meta_tokens.astype(dt)], axis=0)
    x = jnp.concatenate([x_prompt.reshape(S_ROW0, D_MODEL), x_sample.reshape(DEC_BATCH * DEC_SEQ, D_MODEL)] +
                        [lead] * BATCH + [jnp.zeros((M_PAD - L_ROW0 - BATCH * CHUNK, D_MODEL), dt)], axis=0)
    xb = x.astype(BF16)
    peer_ub, peer_vb = peer_u.astype(BF16), peer_v.astype(BF16)

    cos_p, sin_p = _rope_tables(jnp.arange(P_ROWS) - P_LEAD)
    cos_s, sin_s = _rope_tables(PAST_LEN + jnp.arange(S_LEN))
    lb_table = jnp.cumsum(jax.nn.softmax(hgrn_lb_logits.astype(F32), axis=0), axis=0)

    z = matmul(xb, even_w_in[0].astype(BF16), tm=TOK_TILE, tn=1024, name="even_in")
    lb = lb_table[0].reshape(1, -1)
    ng = hgrn_norm_g[0].reshape(1, -1)
    mix_p, hgrn_p, ret_p = even_mixer_prompt(z, lb, ng, cos_p, sin_p)
    mix_s, hgrn_s, ret_s = even_mixer_sample(_sample_rows(z), lb, ng, cos_s, sin_s, state_hgrn[0], state_ret[0])
    mix = _merge_rows(mix_p, mix_s)
    x, xb = proj_ln(x, mix, even_w_out[0].astype(BF16), ln_g[0, 0], ln_b[0, 0], tm=TOK_TILE, name="even_out")
    x, xb = peer_ln(x, xb, peer_w_query[0].T.astype(BF16),
                    peer_sub_keys[0].reshape(2 * PEER_HEADS, PEER_KEYS, PEER_QDIM // 2).astype(BF16),
                    peer_ub, peer_vb, ln_g[0, 1], ln_b[0, 1], layer=0)

    w_in1, params = _odd_params(odd_w_in[0], conv_w[0], conv_b[0], dt_bias[0], a_log[0], d_skip[0], ssm_norm_g[0],
                                shift_mu[0], rwkv_w0[0], rwkv_w2[0], rwkv_a0[0], rwkv_a2[0], rwkv_g2[0], rwkv_k_k[0],
                                rwkv_k_a[0], rwkv_r_k[0], lnx_g[0], lnx_b[0])
    z = matmul(xb, w_in1, tm=TOK_TILE, tn=1024, name="odd_in")
    mix_p, ssm_p, wkv_p, conv_p, shift_p = odd_mixer_prompt(z, params)
    conv8 = jnp.pad(state_conv[0], ((0, 0), (8 - (CONV_W - 1), 0), (0, 0)))
    shift8 = jnp.pad(state_shift[0][:, None, :], ((0, 0), (7, 0), (0, SHIFT_W - SHIFT_DIM)))
    mix_s, ssm_s, wkv_s, conv_s, shift_s = odd_mixer_sample(_sample_rows(z), params,
                                                            (state_ssm[0], state_wkv[0], conv8, shift8))
    mix = _merge_rows(mix_p, mix_s)
    x, xb = proj_ln(x, mix, odd_w_out[0].astype(BF16), ln_g[1, 0], ln_b[1, 0], tm=TOK_TILE, name="odd_out")
    x, xb = peer_ln(x, xb, peer_w_query[1].T.astype(BF16),
                    peer_sub_keys[1].reshape(2 * PEER_HEADS, PEER_KEYS, PEER_QDIM // 2).astype(BF16),
                    peer_ub, peer_vb, ln_g[1, 1], ln_b[1, 1], layer=1)

    y_prompt = x[:S_ROW0].reshape(BATCH, SEQ, D_MODEL)
    y_sample = x[S_ROW0:L_ROW0].reshape(DEC_BATCH, DEC_SEQ, D_MODEL)
    nc = CONV_W - 1
    return (y_prompt, y_sample, hgrn_p[None], hgrn_s[None], ret_p[None], ret_s[None], ssm_p[None], ssm_s[None],
            conv_p[None, :, 8 - nc:], conv_s[None, :, DEC_SEQ - nc:DEC_SEQ], wkv_p[None], wkv_s[None],
            shift_p[None, :, 7, :SHIFT_DIM], shift_s[None, :, DEC_SEQ - 1, :SHIFT_DIM])
```

```python
import functools
import math

import jax
import jax.numpy as jnp
from jax import lax
from jax.experimental import pallas as pl
from jax.experimental.pallas import tpu as pltpu

D_MODEL = 2048
BATCH = 4
SEQ = 2048
DEPTH = 2
DEC_BATCH = 128
DEC_SEQ = 4
PAST_LEN = 16384
N_META = 16
CHUNK = 64

H_A, DK_A, DV_A = 8, 128, 128
H_B, DK_B, DV_B = 4, 128, 256
EVEN_IN = 4 * H_A * DK_A + 2 * H_B * DK_B + 2 * H_B * DV_B
EVEN_MIX = H_A * DV_A + H_B * DV_B

H_C, P_C, N_C, G_C, CONV_W = 16, 64, 128, 2, 4
DI_C = H_C * P_C
CONV_DIM = DI_C + 2 * G_C * N_C
H_D, P_D = 16, 64
DI_D = H_D * P_D
R_W, R_A, R_G = 64, 64, 160
SHIFT_DIM = 3 * DI_D + R_W + R_A + R_G
ODD_IN = DI_C + CONV_DIM + H_C + SHIFT_DIM
ODD_MIX = DI_C + DI_D

PEER_KEYS = 128
PEER_EXPERTS = PEER_KEYS * PEER_KEYS
PEER_HEADS = 8
PEER_TOPK = 16
PEER_QDIM = 256

ALPHA = (2.0 * DEPTH) ** 0.25
LN_EPS = 1e-5
RMS_EPS = 1e-6
RWKV_GN_EPS = 64e-5
ROPE_BASE = 10000.0

F32 = jnp.float32
BF16 = jnp.bfloat16

P_LEAD = CHUNK - N_META
P_ROWS = CHUNK + SEQ
P_CHUNKS = P_ROWS // CHUNK
S_CHUNKS = DEC_BATCH * DEC_SEQ // (BATCH * CHUNK)
SLOT_CHUNKS = P_CHUNKS + S_CHUNKS + 1
S_LEN = 8
SAMPLE_GROUP = 4
ODD_SAMPLE_GROUP = 2
TOK_TILE = 512
M_PAD = BATCH * SLOT_CHUNKS * CHUNK
assert M_PAD % 1024 == 0 and DEC_BATCH * DEC_SEQ == BATCH * S_CHUNKS * CHUNK

VMEM_LIMIT = 56 * 1024 * 1024
PEER_VMEM_LIMIT = 60 * 1024 * 1024


def _dot_bf(a, b, dims=(((1,), (0,)), ((), ()))):
    return lax.dot_general(a.astype(BF16), b.astype(BF16), dims, preferred_element_type=F32)


def _parts(x, n):
    out, rem = [], x
    for i in range(n):
        p = rem.astype(BF16)
        out.append(p)
        if i + 1 < n:
            rem = rem - p.astype(F32)
    return out


def _dot_sel(x, sel, dims=(((1,), (0,)), ((), ())), n=3, sel_first=False):
    sel = sel.astype(BF16)
    acc = None
    for p in _parts(x, n):
        d = lax.dot_general(sel, p, dims, preferred_element_type=F32) if sel_first else \
            lax.dot_general(p, sel, dims, preferred_element_type=F32)
        acc = d if acc is None else acc + d
    return acc


NT = (((1,), (1,)), ((), ()))
TN = (((0,), (0,)), ((), ()))


def _sigmoid(x):
    return 1.0 / (1.0 + jnp.exp(-x))


def _silu(x):
    return x * _sigmoid(x)


def _mm_kernel(x_ref, w_ref, o_ref):
    o_ref[...] = jnp.dot(x_ref[...].astype(BF16), w_ref[...], preferred_element_type=F32).astype(o_ref.dtype)


def matmul(x, w, *, tm, tn, out_dtype=F32, name="matmul"):
    m, k = x.shape
    n = w.shape[1]
    return pl.pallas_call(
        _mm_kernel,
        out_shape=jax.ShapeDtypeStruct((m, n), out_dtype),
        grid=(n // tn, m // tm),
        in_specs=[pl.BlockSpec((tm, k), lambda j, i: (i, 0)),
                  pl.BlockSpec((k, tn), lambda j, i: (0, j))],
        out_specs=pl.BlockSpec((tm, tn), lambda j, i: (i, j)),
        compiler_params=pltpu.CompilerParams(dimension_semantics=("arbitrary", "arbitrary"),
                                             vmem_limit_bytes=VMEM_LIMIT),
        name=name,
    )(x, w)


def _ln_rows(v, g, b):
    mu = jnp.mean(v, -1, keepdims=True)
    c = v - mu
    var = jnp.mean(c * c, -1, keepdims=True)
    return c * lax.rsqrt(var + LN_EPS) * g + b


def _proj_ln_kernel(x_ref, m_ref, w_ref, g_ref, b_ref, o_ref, ob_ref):
    acc = jnp.dot(m_ref[...].astype(BF16), w_ref[...], preferred_element_type=F32)
    y = _ln_rows(ALPHA * x_ref[...] + acc, g_ref[...], b_ref[...])
    o_ref[...] = y
    ob_ref[...] = y.astype(BF16)


def proj_ln(x, mix, w, g, b, *, tm, name="proj_ln"):
    m, d = x.shape
    k = mix.shape[1]
    return pl.pallas_call(
        _proj_ln_kernel,
        out_shape=(jax.ShapeDtypeStruct((m, d), F32), jax.ShapeDtypeStruct((m, d), BF16)),
        grid=(m // tm,),
        in_specs=[pl.BlockSpec((tm, d), lambda i: (i, 0)),
                  pl.BlockSpec((tm, k), lambda i: (i, 0)),
                  pl.BlockSpec((k, d), lambda i: (0, 0)),
                  pl.BlockSpec((1, d), lambda i: (0, 0)),
                  pl.BlockSpec((1, d), lambda i: (0, 0))],
        out_specs=(pl.BlockSpec((tm, d), lambda i: (i, 0)), pl.BlockSpec((tm, d), lambda i: (i, 0))),
        compiler_params=pltpu.CompilerParams(dimension_semantics=("arbitrary",),
                                             vmem_limit_bytes=VMEM_LIMIT),
        name=name,
    )(x, mix, w, g.reshape(1, d), b.reshape(1, d))


def _row_info(L, lead, nvalid):
    r = lax.broadcasted_iota(jnp.int32, (L, 1), 0)
    return r, (r >= lead) & (r < lead + nvalid)


def _seq_views(ref, G):
    if len(ref.shape) == 2:
        return [ref]
    if len(ref.shape) == 3:
        return [ref.at[g] for g in range(G)]
    return [ref.at[g, 0] for g in range(G)]


def _tril(L, strict=False):
    r = lax.broadcasted_iota(jnp.int32, (L, L), 0)
    c = lax.broadcasted_iota(jnp.int32, (L, L), 1)
    return (c < r) if strict else (c <= r)


def _hgrn_heads(qs, ks, vs, logfs, s0s, L):
    n = range(len(qs))
    tril = _tril(L).astype(F32)
    ones_l = jnp.ones((L, DK_A), F32)
    ones_k = jnp.ones((DK_A, DK_A), BF16)
    b = [_dot_sel(logfs[i], tril, sel_first=True) for i in n]
    b_end_col = [_dot_sel(logfs[i], ones_l, TN) for i in n]
    o = [_dot_bf(qs[i] * jnp.exp(b[i]), s0s[i]) for i in n]
    upd = [_dot_bf(ks[i] * jnp.exp(b[i][L - 1:L, :] - b[i]), vs[i], TN) for i in n]
    s_new = [jnp.exp(b_end_col[i]) * s0s[i] + upd[i] for i in n]
    sub = min(16, L)
    t_idx = lax.broadcasted_iota(jnp.int32, (sub, 1), 0)
    outs = [[] for _ in n]
    for blk in range(L // sub):
        r0 = blk * sub
        acc = [o[i][r0:r0 + sub] for i in n]
        if blk > 0:
            a = [_dot_bf(qs[i][r0:r0 + sub] * jnp.exp(b[i][r0:r0 + sub] - b[i][r0 - 1:r0, :]),
                         ks[i][0:r0] * jnp.exp(b[i][r0 - 1:r0, :] - b[i][0:r0]), NT) for i in n]
            acc = [acc[i] + _dot_bf(a[i], vs[i][0:r0]) for i in n]
        prod = [jnp.concatenate(
            [qs[i][r0:r0 + sub] * ks[i][r0 + s:r0 + s + 1, :] *
             jnp.exp(jnp.where(t_idx >= s, b[i][r0:r0 + sub] - b[i][r0 + s:r0 + s + 1, :], -jnp.inf))
             for s in range(sub)], axis=0).astype(BF16) for i in n]
        score = [jnp.dot(prod[i], ones_k, preferred_element_type=F32) for i in n]
        for i in n:
            t = acc[i]
            for s in range(sub):
                t = t + score[i][s * sub:(s + 1) * sub, :] * vs[i][r0 + s:r0 + s + 1, :]
            outs[i].append(t)
    return [jnp.concatenate(x, axis=0) if len(x) > 1 else x[0] for x in outs], s_new


def _even_kernel(*refs, L, nvalid, chunks, has_s0, G):
    if has_s0:
        z_ref, lb_ref, ng_ref, cos_ref, sin_ref, sh0_ref, sr0_ref, o_ref, sho_ref, sro_ref, sh_s, sr_s = refs
    else:
        z_ref, lb_ref, ng_ref, cos_ref, sin_ref, o_ref, sho_ref, sro_ref, sh_s, sr_s = refs
    c = pl.program_id(1)

    @pl.when(c == 0)
    def _():
        if has_s0:
            for g in range(G):
                sh_s[g * H_A:(g + 1) * H_A] = sh0_ref[g]
                sr_s[g * H_B:(g + 1) * H_B] = sr0_ref[g]
        else:
            sh_s[...] = jnp.zeros_like(sh_s)
            sr_s[...] = jnp.zeros_like(sr_s)

    lead = jnp.where(c == 0, P_LEAD, 0) if chunks > 1 else 0
    r, valid = _row_info(L, lead, nvalid)
    cnt_col = jnp.clip(r + 1 - lead, 0, nvalid).astype(F32)
    rr = lax.broadcasted_iota(jnp.int32, (1, L), 1)
    cnt_row = jnp.clip(rr + 1 - lead, 0, nvalid).astype(F32)

    off_q, off_f, off_i, off_g = 0, H_A * DK_A, 2 * H_A * DK_A, 2 * H_A * DK_A + H_A * DV_A
    base_b = 2 * H_A * DK_A + 2 * H_A * DV_A
    off_bq, off_bk, off_bv = base_b, base_b + H_B * DK_B, base_b + 2 * H_B * DK_B
    off_bg = off_bv + H_B * DV_B
    ng = ng_ref[...]
    zv, ov = _seq_views(z_ref, G), _seq_views(o_ref, G)
    ia = [(g, h) for g in range(G) for h in range(H_A)]
    ib = [(g, h) for g in range(G) for h in range(H_B)]
    na, nb_ = range(len(ia)), range(len(ib))
    af = [zv[g][:, off_f + h * DK_A: off_f + (h + 1) * DK_A] for g, h in ia]
    lbs = [lb_ref[:, h * DK_A:(h + 1) * DK_A] for g, h in ia]
    logf = [jnp.where(valid, jnp.log(lbs[i] + (1.0 - lbs[i]) * _sigmoid(af[i])), 0.0) for i in na]
    ka = [jnp.where(valid, (1.0 - lbs[i]) * _sigmoid(-af[i]), 0.0) for i in na]
    qa = [_silu(zv[g][:, off_q + h * DK_A: off_q + (h + 1) * DK_A]) for g, h in ia]
    va = [zv[g][:, off_i + h * DV_A: off_i + (h + 1) * DV_A] for g, h in ia]
    cosv, sinv = cos_ref[...], sin_ref[...]
    bq = [zv[g][:, off_bq + h * DK_B: off_bq + (h + 1) * DK_B] for g, h in ib]
    bk = [zv[g][:, off_bk + h * DK_B: off_bk + (h + 1) * DK_B] for g, h in ib]
    vb = [zv[g][:, off_bv + h * DV_B: off_bv + (h + 1) * DV_B] for g, h in ib]
    qb = [bq[i] * cosv + pltpu.roll(bq[i], DK_B // 2, 1) * sinv for i in nb_]
    kb = [jnp.where(valid, (bk[i] * cosv + pltpu.roll(bk[i], DK_B // 2, 1) * sinv) * (DK_B ** -0.5), 0.0) for i in nb_]
    lg = [math.log(1.0 - 2.0 ** (-5.0 - h)) for g, h in ib]

    oa, sa_new = _hgrn_heads(qa, ka, va, logf, [sh_s[i] for i in na], L)
    causal = _tril(L)
    cnt_end = cnt_col[L - 1:L, :]
    scores = [_dot_bf(qb[i], kb[i], NT) * jnp.where(causal, jnp.exp((cnt_col - cnt_row) * lg[i]), 0.0) for i in nb_]
    inter = [_dot_bf(qb[i] * jnp.exp(cnt_col * lg[i]), sr_s[i]) for i in nb_]
    ob = [_dot_bf(scores[i], vb[i]) + inter[i] for i in nb_]
    updb = [_dot_bf(kb[i] * jnp.exp((cnt_end - cnt_col) * lg[i]), vb[i], TN) for i in nb_]
    for i, (g, h) in enumerate(ia):
        sh_s[i] = sa_new[i]
        o = oa[i] * lax.rsqrt(jnp.mean(oa[i] * oa[i], -1, keepdims=True) + RMS_EPS) * ng
        ov[g][:, h * DV_A:(h + 1) * DV_A] = o * _silu(zv[g][:, off_g + h * DV_A: off_g + (h + 1) * DV_A])
    for i, (g, h) in enumerate(ib):
        sr_s[i] = jnp.exp(cnt_end * lg[i]) * sr_s[i] + updb[i]
        mu = jnp.mean(ob[i], -1, keepdims=True)
        cc = ob[i] - mu
        o = cc * lax.rsqrt(jnp.mean(cc * cc, -1, keepdims=True) + LN_EPS)
        ov[g][:, H_A * DV_A + h * DV_B: H_A * DV_A + (h + 1) * DV_B] = \
            o * _silu(zv[g][:, off_bg + h * DV_B: off_bg + (h + 1) * DV_B])

    @pl.when(c == chunks - 1)
    def _():
        for g in range(G):
            sho_ref[g] = sh_s[g * H_A:(g + 1) * H_A]
            sro_ref[g] = sr_s[g * H_B:(g + 1) * H_B]


def even_mixer_prompt(z, lb, ng, cos, sin, *, chunks=P_CHUNKS):
    nb = z.shape[0]
    kern = functools.partial(_even_kernel, L=CHUNK, nvalid=CHUNK, chunks=chunks, has_s0=False, G=nb)
    blk = lambda b, c: (0, c, 0, 0)
    whole = lambda b, c: (0, 0, 0, 0)
    return pl.pallas_call(
        kern,
        out_shape=(jax.ShapeDtypeStruct(z.shape[:3] + (EVEN_MIX,), F32),
                   jax.ShapeDtypeStruct((nb, H_A, DK_A, DV_A), F32),
                   jax.ShapeDtypeStruct((nb, H_B, DK_B, DV_B), F32)),
        grid=(1, chunks),
        in_specs=[pl.BlockSpec((nb, 1, CHUNK, EVEN_IN), blk),
                  pl.BlockSpec((1, H_A * DK_A), lambda b, c: (0, 0)),
                  pl.BlockSpec((1, DV_A), lambda b, c: (0, 0)),
                  pl.BlockSpec((CHUNK, DK_B), lambda b, c: (c, 0)),
                  pl.BlockSpec((CHUNK, DK_B), lambda b, c: (c, 0))],
        out_specs=(pl.BlockSpec((nb, 1, CHUNK, EVEN_MIX), blk),
                   pl.BlockSpec((nb, H_A, DK_A, DV_A), whole),
                   pl.BlockSpec((nb, H_B, DK_B, DV_B), whole)),
        scratch_shapes=[pltpu.VMEM((nb * H_A, DK_A, DV_A), F32), pltpu.VMEM((nb * H_B, DK_B, DV_B), F32)],
        compiler_params=pltpu.CompilerParams(dimension_semantics=("arbitrary", "arbitrary"),
                                             vmem_limit_bytes=VMEM_LIMIT),
        name="even_mixer_prompt",
    )(z, lb, ng, cos, sin)


def even_mixer_sample(z, lb, ng, cos, sin, s_hgrn, s_ret):
    nb = z.shape[0]
    G = SAMPLE_GROUP
    kern = functools.partial(_even_kernel, L=S_LEN, nvalid=DEC_SEQ, chunks=1, has_s0=True, G=G)
    return pl.pallas_call(
        kern,
        out_shape=(jax.ShapeDtypeStruct((nb, S_LEN, EVEN_MIX), F32),
                   jax.ShapeDtypeStruct((nb, H_A, DK_A, DV_A), F32),
                   jax.ShapeDtypeStruct((nb, H_B, DK_B, DV_B), F32)),
        grid=(nb // G, 1),
        in_specs=[pl.BlockSpec((G, S_LEN, EVEN_IN), lambda b, c: (b, 0, 0)),
                  pl.BlockSpec((1, H_A * DK_A), lambda b, c: (0, 0)),
                  pl.BlockSpec((1, DV_A), lambda b, c: (0, 0)),
                  pl.BlockSpec((S_LEN, DK_B), lambda b, c: (0, 0)),
                  pl.BlockSpec((S_LEN, DK_B), lambda b, c: (0, 0)),
                  pl.BlockSpec((G, H_A, DK_A, DV_A), lambda b, c: (b, 0, 0, 0)),
                  pl.BlockSpec((G, H_B, DK_B, DV_B), lambda b, c: (b, 0, 0, 0))],
        out_specs=(pl.BlockSpec((G, S_LEN, EVEN_MIX), lambda b, c: (b, 0, 0)),
                   pl.BlockSpec((G, H_A, DK_A, DV_A), lambda b, c: (b, 0, 0, 0)),
                   pl.BlockSpec((G, H_B, DK_B, DV_B), lambda b, c: (b, 0, 0, 0))),
        scratch_shapes=[pltpu.VMEM((G * H_A, DK_A, DV_A), F32), pltpu.VMEM((G * H_B, DK_B, DV_B), F32)],
        compiler_params=pltpu.CompilerParams(dimension_semantics=("arbitrary", "arbitrary"),
                                             vmem_limit_bytes=VMEM_LIMIT),
        name="even_mixer_sample",
    )(z, lb, ng, cos, sin, s_hgrn, s_ret)


def _rope_tables(pos):
    half = DK_B // 2
    inv = ROPE_BASE ** (-jnp.arange(half, dtype=F32) / half)
    ang = pos.astype(F32)[:, None] * inv
    cos, sin = jnp.cos(ang), jnp.sin(ang)
    return jnp.concatenate([cos, cos], -1), jnp.concatenate([-sin, sin], -1)


ODD_PAD = 6144
O_Z, O_X, O_R, O_K, O_V, O_T = 0, DI_C, DI_C + CONV_DIM, DI_C + CONV_DIM + DI_D, DI_C + CONV_DIM + 2 * DI_D, \
    DI_C + CONV_DIM + 3 * DI_D
T_W = 384
T_DT = R_W + R_A + R_G
SHIFT_W = 3 * DI_D + T_W
HALF = 64


def _softplus(x):
    return jnp.maximum(x, 0.0) + jnp.log1p(jnp.exp(-jnp.abs(x)))


def _lane_lt(n, width):
    return lax.broadcasted_iota(jnp.int32, (1, width), 1) < n


def _pair_ones():
    r = lax.broadcasted_iota(jnp.int32, (2 * HALF, 2 * HALF), 0) < HALF
    c = lax.broadcasted_iota(jnp.int32, (2 * HALF, 2 * HALF), 1) < HALF
    return r == c


def _stack_pair(x, m_a):
    return jnp.concatenate([jnp.where(m_a, x, 0.0), jnp.where(m_a, 0.0, x)], axis=0)


def _rwkv_pairs(rs, ks_, vs_, als, bes, logws, Gs, bds, L):
    n = range(len(rs))
    m_a = _lane_lt(HALF, 2 * HALF)
    m_l = _lane_lt(L, 2 * L)
    ri = lax.broadcasted_iota(jnp.int32, (2 * L, 2 * L), 0)
    ci = lax.broadcasted_iota(jnp.int32, (2 * L, 2 * L), 1)
    cm = jnp.where(ci >= L, ci - L, ci)
    keep = cm < jnp.where(ri >= L, ri - L + 1, ri)
    eye = (ri == ci).astype(F32)
    blk4 = (ri // 4) == (ci // 4)
    pair_blk = _pair_ones()

    e_inv = [jnp.exp(-Gs[i]) for i in n]
    lhs = [jnp.concatenate([als[i] * jnp.exp(Gs[i] - logws[i]), rs[i] * jnp.exp(Gs[i])], axis=0).astype(BF16) for i in n]
    ks = [_stack_pair(ks_[i] * e_inv[i], m_a).astype(BF16) for i in n]
    bs = [_stack_pair(bes[i] * e_inv[i], m_a).astype(BF16) for i in n]
    vs = [_stack_pair(vs_[i], m_a).astype(BF16) for i in n]
    a_k = [jnp.where(keep, _dot_bf(lhs[i], ks[i], NT), 0.0) for i in n]
    a_b = [jnp.where(keep, _dot_bf(lhs[i], bs[i], NT), 0.0) for i in n]
    sb = [_dot_bf(lhs[i], bds[i], NT) for i in n]
    base = [sb[i] + _dot_bf(a_k[i], vs[i]) for i in n]
    n_bd = [jnp.concatenate([jnp.where(m_l, a_b[i][0:L], 0.0), jnp.where(m_l, 0.0, a_b[i][0:L])], axis=0) for i in n]
    n4 = [jnp.where(blk4, n_bd[i], 0.0) for i in n]
    n4sq = [_dot_bf(n4[i], n4[i]) for i in n]
    t_inv = [eye - n4[i] for i in n]
    t_inv = [t_inv[i] + _dot_bf(t_inv[i], n4sq[i]) for i in n]
    blk = 4
    while blk < L:
        off = ((ri // (2 * blk)) == (ci // (2 * blk))) & ((ri % (2 * blk)) >= blk) & ((ci % (2 * blk)) < blk)
        ct = [_dot_bf(jnp.where(off, n_bd[i], 0.0), t_inv[i]) for i in n]
        t_inv = [t_inv[i] - _dot_bf(t_inv[i], ct[i]) for i in n]
        blk *= 2
    u = [_dot_bf(t_inv[i][0:L, :] + t_inv[i][L:2 * L, :], _stack_pair(base[i][0:L], m_a)) for i in n]
    y = [base[i][L:2 * L] - _dot_bf(a_b[i][L:2 * L], _stack_pair(u[i], m_a)) for i in n]
    gl = [Gs[i][L - 1:L, :] for i in n]
    dec = [jnp.exp(gl[i] - Gs[i]) for i in n]
    upd = [_dot_bf(jnp.concatenate([vs_[i], u[i]], axis=0),
                   jnp.concatenate([ks_[i] * dec[i], -(bes[i] * dec[i])], axis=0), TN) for i in n]
    bd_new = [bds[i] * jnp.exp(gl[i]) + jnp.where(pair_blk, upd[i], 0.0) for i in n]
    return y, bd_new


def _ssd_pair(xdt, cg, bg, cb, bcs, brows, sp, L):
    m_a = _lane_lt(HALF, 2 * HALF)
    causal = _tril(L)
    out = jnp.zeros((L, 2 * HALF), F32)
    ends = [bc[L - 1:L, :] for bc in bcs]
    sp_new = sp * jnp.where(m_a, jnp.exp(ends[0]), jnp.exp(ends[1]))
    for x in range(2):
        keep = m_a if x == 0 else jnp.logical_not(m_a)
        seg = jnp.where(causal, jnp.exp(bcs[x] - brows[x]), 0.0)
        xm = jnp.where(keep, xdt, 0.0)
        out = out + _dot_bf(cb * seg, xm) + _dot_bf(cg * jnp.exp(bcs[x]), jnp.where(keep, sp, 0.0))
        sp_new = sp_new + _dot_bf(bg * jnp.exp(ends[x] - bcs[x]), xm, TN)
    return out, sp_new


def _odd_kernel(*refs, L, nvalid, chunks, has_s0, G):
    if has_s0:
        (z_ref, vec_ref, cp_ref, tp_ref, w2_ref, a2_ref, g2_ref, ssm0_ref, wkv0_ref, conv0_ref, shift0_ref,
         o_ref, ssmo_ref, wkvo_ref, convo_ref, shifto_ref, ssm_s, wkv_s, conv_c, shift_c) = refs
    else:
        (z_ref, vec_ref, cp_ref, tp_ref, w2_ref, a2_ref, g2_ref,
         o_ref, ssmo_ref, wkvo_ref, convo_ref, shifto_ref, ssm_s, wkv_s, conv_c, shift_c) = refs
    c = pl.program_id(1)
    npair = H_C // 2

    @pl.when(c == 0)
    def _():
        if has_s0:
            zpad = jnp.zeros((HALF, HALF), F32)
            for g in range(G):
                for p in range(npair):
                    ssm_s[g * npair + p] = jnp.concatenate([ssm0_ref[g, 2 * p], ssm0_ref[g, 2 * p + 1]], axis=1)
                    wkv_s[g * npair + p] = jnp.concatenate(
                        [jnp.concatenate([wkv0_ref[g, 2 * p], zpad], axis=1),
                         jnp.concatenate([zpad, wkv0_ref[g, 2 * p + 1]], axis=1)], axis=0)
                conv_c[g * 8:(g + 1) * 8, :] = conv0_ref[g]
                shift_c[g * 8:(g + 1) * 8, :] = shift0_ref[g]
        else:
            ssm_s[...] = jnp.zeros_like(ssm_s)
            wkv_s[...] = jnp.zeros_like(wkv_s)
            conv_c[...] = jnp.zeros_like(conv_c)
            shift_c[...] = jnp.zeros_like(shift_c)

    lead = jnp.where(c == 0, P_LEAD, 0) if chunks > 1 else 0
    r_idx, valid = _row_info(L, lead, nvalid)
    m_a = _lane_lt(HALF, 2 * HALF)
    tril = _tril(L).astype(F32)
    triu = (lax.broadcasted_iota(jnp.int32, (L, L), 0) <= lax.broadcasted_iota(jnp.int32, (L, L), 1)).astype(F32)
    ones_blk = _pair_ones().astype(F32)
    r8 = lax.broadcasted_iota(jnp.int32, (8, 1), 0)
    zv, ov = _seq_views(z_ref, G), _seq_views(o_ref, G)
    pairs = range(npair)
    sls = [slice(p * 2 * HALF, (p + 1) * 2 * HALF) for p in pairs]
    w2_hi, w2_lo = _parts(w2_ref[...], 2)

    def shift_mix(p_raw, carry_row, mu):
        pv = jnp.where(valid, p_raw, 0.0)
        prev = jnp.where(r_idx == 0, carry_row, pltpu.roll(pv, 1, 0))
        return pv, pv + (prev - pv) * mu

    rw = []
    for g in range(G):
        zg, og = zv[g], ov[g]
        cs = slice(g * 8, (g + 1) * 8)
        xbc = jnp.where(valid, zg[:, O_X:O_X + CONV_DIM], 0.0)
        c8 = conv_c[cs, :]
        conv = cp_ref[CONV_W:CONV_W + 1, :] + xbc * cp_ref[CONV_W - 1:CONV_W, :]
        for j in range(1, CONV_W):
            rolled = pltpu.roll(xbc, j, 0)
            head = jnp.where(r8 < j, pltpu.roll(c8, j, 0), rolled[0:8])
            sh = jnp.concatenate([head, rolled[8:]], axis=0) if L > 8 else head
            conv = conv + sh * cp_ref[CONV_W - 1 - j:CONV_W - j, :]
        conv_c[cs, :] = xbc[L - 8:L]
        act = _silu(conv)
        xc, bm, cm = act[:, :DI_C], act[:, DI_C:DI_C + G_C * N_C], act[:, DI_C + G_C * N_C:]

        t_raw = zg[:, O_T:O_T + T_W]
        dt = jnp.where(valid, _softplus(t_raw + tp_ref[1:2, :])[:, T_DT:T_DT + H_C], 0.0)
        logf = dt * tp_ref[2:3, T_DT:T_DT + H_C]
        bc_all = _dot_sel(logf, tril, sel_first=True)
        brow_all = _dot_sel(logf, triu, TN)
        per_g = npair // G_C
        for grp in range(G_C):
            cg = cm[:, grp * N_C:(grp + 1) * N_C]
            bg = bm[:, grp * N_C:(grp + 1) * N_C]
            cb = _dot_bf(cg, bg, NT)
            ys, ss = [], jnp.zeros((L, 1), F32)
            for pp in range(per_g):
                p = grp * per_g + pp
                sl = sls[p]
                h_a, h_b = 2 * p, 2 * p + 1
                xcp = xc[:, sl]
                xdt = xcp * jnp.where(m_a, dt[:, h_a:h_a + 1], dt[:, h_b:h_b + 1])
                o, sp_new = _ssd_pair(xdt, cg, bg, cb, [bc_all[:, h_a:h_a + 1], bc_all[:, h_b:h_b + 1]],
                                      [brow_all[h_a:h_a + 1, :], brow_all[h_b:h_b + 1, :]], ssm_s[g * npair + p], L)
                ssm_s[g * npair + p] = sp_new
                y = (o + xcp * vec_ref[0:1, sl]) * _silu(zg[:, O_Z + p * 2 * HALF:O_Z + (p + 1) * 2 * HALF])
                ys.append(y)
                ss = ss + jnp.sum(y * y, axis=-1, keepdims=True)
            scale = lax.rsqrt(ss / (DI_C // G_C) + RMS_EPS)
            for pp in range(per_g):
                sl = sls[grp * per_g + pp]
                og[:, sl] = ys[pp] * scale * vec_ref[1:2, sl]

        r_raw, r = shift_mix(zg[:, O_R:O_R + DI_D], shift_c[g * 8 + 7:g * 8 + 8, 0:DI_D], vec_ref[2:3, :])
        k_raw, k = shift_mix(zg[:, O_K:O_K + DI_D], shift_c[g * 8 + 7:g * 8 + 8, DI_D:2 * DI_D], vec_ref[3:4, :])
        v_raw, v = shift_mix(zg[:, O_V:O_V + DI_D], shift_c[g * 8 + 7:g * 8 + 8, 2 * DI_D:3 * DI_D], vec_ref[4:5, :])
        t_rawm, tm = shift_mix(t_raw, shift_c[g * 8 + 7:g * 8 + 8, 3 * DI_D:SHIFT_W], tp_ref[0:1, :])
        shift_c[cs, 0:DI_D] = r_raw[L - 8:L]
        shift_c[cs, DI_D:2 * DI_D] = k_raw[L - 8:L]
        shift_c[cs, 2 * DI_D:3 * DI_D] = v_raw[L - 8:L]
        shift_c[cs, 3 * DI_D:SHIFT_W] = t_rawm[L - 8:L]

        th_hi, th_lo = _parts(jnp.tanh(tm), 2)
        w_pre = vec_ref[5:6, :] + (jnp.dot(th_hi, w2_hi, preferred_element_type=F32) +
                                   jnp.dot(th_lo, w2_hi, preferred_element_type=F32) +
                                   jnp.dot(th_hi, w2_lo, preferred_element_type=F32))
        logw = jnp.where(valid, -jnp.exp(-_softplus(-w_pre) - 0.5), 0.0)
        g_cum = _dot_sel(logw, tril, sel_first=True)
        a = _sigmoid(vec_ref[6:7, :] + _dot_bf(tm, a2_ref[...]))
        gate = _dot_bf(_sigmoid(tm), g2_ref[...])
        kkr = k * vec_ref[7:8, :]
        k2 = k * (1.0 + (a - 1.0) * vec_ref[8:9, :])
        rw.append(dict(r=r, v=v, a=a, gate=gate, kkr=kkr, k2=k2, rk=r * k2 * vec_ref[9:10, :], logw=logw, g_cum=g_cum))

    gp = [(g, p) for g in range(G) for p in pairs]
    n = range(len(gp))
    nrm = [jnp.sqrt(_dot_sel(rw[g]["kkr"][:, sls[p]] * rw[g]["kkr"][:, sls[p]], ones_blk, n=2)) for g, p in gp]
    al = [jnp.where(valid, rw[g]["kkr"][:, sls[p]] / jnp.maximum(nrm[i], 1e-12), 0.0) for i, (g, p) in enumerate(gp)]
    be = [al[i] * rw[g]["a"][:, sls[p]] for i, (g, p) in enumerate(gp)]
    ys, bd_new = _rwkv_pairs([rw[g]["r"][:, sls[p]] for g, p in gp],
                             [jnp.where(valid, rw[g]["k2"][:, sls[p]], 0.0) for g, p in gp],
                             [rw[g]["v"][:, sls[p]] for g, p in gp], al, be,
                             [rw[g]["logw"][:, sls[p]] for g, p in gp], [rw[g]["g_cum"][:, sls[p]] for g, p in gp],
                             [wkv_s[i] for i in n], L)
    for i in n:
        wkv_s[i] = bd_new[i]
    mu = [_dot_sel(ys[i], ones_blk, n=2) * (1.0 / P_D) for i in n]
    yc = [ys[i] - mu[i] for i in n]
    var = [_dot_sel(yc[i] * yc[i], ones_blk, n=2) * (1.0 / P_D) for i in n]
    bonus = [_dot_sel(rw[g]["rk"][:, sls[p]], ones_blk, n=2) for g, p in gp]
    for i, (g, p) in enumerate(gp):
        sl = sls[p]
        yn = yc[i] * lax.rsqrt(var[i] + RWKV_GN_EPS) * vec_ref[10:11, sl] + vec_ref[11:12, sl]
        ov[g][:, DI_C + p * 2 * HALF:DI_C + (p + 1) * 2 * HALF] = (yn + bonus[i] * rw[g]["v"][:, sl]) * rw[g]["gate"][:, sl]

    @pl.when(c == chunks - 1)
    def _():
        for g in range(G):
            for p in range(npair):
                sp = ssm_s[g * npair + p]
                ssmo_ref[g, 2 * p] = sp[:, 0:HALF]
                ssmo_ref[g, 2 * p + 1] = sp[:, HALF:2 * HALF]
                bd = wkv_s[g * npair + p]
                wkvo_ref[g, 2 * p] = bd[0:HALF, 0:HALF]
                wkvo_ref[g, 2 * p + 1] = bd[HALF:2 * HALF, HALF:2 * HALF]
            convo_ref[g] = conv_c[g * 8:(g + 1) * 8, :]
            shifto_ref[g] = shift_c[g * 8:(g + 1) * 8, :]


def _odd_call(z, params, states, *, L, nvalid, nb, chunks, name, G):
    has_s0 = states is not None
    kern = functools.partial(_odd_kernel, L=L, nvalid=nvalid, chunks=chunks, has_s0=has_s0, G=G)
    npair = H_C // 2
    if z.ndim == 4:
        z_spec = pl.BlockSpec((G, 1, L, ODD_PAD), lambda b, c: (0, c, 0, 0))
        o_spec = pl.BlockSpec((G, 1, L, ODD_MIX), lambda b, c: (0, c, 0, 0))
        o_shape = jax.ShapeDtypeStruct(z.shape[:3] + (ODD_MIX,), F32)
    else:
        z_spec = pl.BlockSpec((G, L, ODD_PAD), lambda b, c: (b, 0, 0))
        o_spec = pl.BlockSpec((G, L, ODD_MIX), lambda b, c: (b, 0, 0))
        o_shape = jax.ShapeDtypeStruct((nb, L, ODD_MIX), F32)
    const2 = lambda b, c: (0, 0)
    per_b = lambda b, c: (b, 0, 0, 0)
    per_b3 = lambda b, c: (b, 0, 0)
    in_specs = [z_spec] + [pl.BlockSpec(p.shape, const2) for p in params]
    args = [z] + list(params)
    st_specs = [pl.BlockSpec((G, H_C, N_C, P_C), per_b), pl.BlockSpec((G, H_D, P_D, P_D), per_b),
                pl.BlockSpec((G, 8, CONV_DIM), per_b3), pl.BlockSpec((G, 8, SHIFT_W), per_b3)]
    if has_s0:
        in_specs += st_specs
        args += list(states)
    return pl.pallas_call(
        kern,
        out_shape=(o_shape,
                   jax.ShapeDtypeStruct((nb, H_C, N_C, P_C), F32), jax.ShapeDtypeStruct((nb, H_D, P_D, P_D), F32),
                   jax.ShapeDtypeStruct((nb, 8, CONV_DIM), F32), jax.ShapeDtypeStruct((nb, 8, SHIFT_W), F32)),
        grid=(nb // G, chunks),
        in_specs=in_specs,
        out_specs=tuple([o_spec] + st_specs),
        scratch_shapes=[pltpu.VMEM((G * npair, N_C, 2 * HALF), F32), pltpu.VMEM((G * npair, 2 * HALF, 2 * HALF), F32),
                        pltpu.VMEM((G * 8, CONV_DIM), F32), pltpu.VMEM((G * 8, SHIFT_W), F32)],
        compiler_params=pltpu.CompilerParams(dimension_semantics=("arbitrary", "arbitrary"),
                                             vmem_limit_bytes=VMEM_LIMIT),
        name=name,
    )(*args)


def odd_mixer_prompt(z, params, *, chunks=P_CHUNKS):
    return _odd_call(z, params, None, L=CHUNK, nvalid=CHUNK, nb=z.shape[0], chunks=chunks,
                     name="odd_mixer_prompt", G=z.shape[0])


def odd_mixer_sample(z, params, states):
    return _odd_call(z, params, states, L=S_LEN, nvalid=DEC_SEQ, nb=z.shape[0], chunks=1,
                     name="odd_mixer_sample", G=ODD_SAMPLE_GROUP)


def _odd_params(odd_w_in, conv_w, conv_b, dt_bias, a_log, d_skip, ssm_norm_g, shift_mu, rwkv_w0, rwkv_w2, rwkv_a0,
                rwkv_a2, rwkv_g2, rwkv_k_k, rwkv_k_a, rwkv_r_k, lnx_g, lnx_b):
    o_dt = DI_C + CONV_DIM
    o_rw = o_dt + H_C
    w = jnp.concatenate([odd_w_in[:, :o_dt], odd_w_in[:, o_rw:], odd_w_in[:, o_dt:o_rw],
                         jnp.zeros((D_MODEL, ODD_PAD - ODD_IN), odd_w_in.dtype)], axis=1)
    vec = jnp.stack([jnp.repeat(d_skip, P_C), ssm_norm_g, shift_mu[:DI_D], shift_mu[DI_D:2 * DI_D],
                     shift_mu[2 * DI_D:3 * DI_D], rwkv_w0, rwkv_a0, rwkv_k_k, rwkv_k_a, rwkv_r_k.reshape(-1),
                     lnx_g, lnx_b] + [jnp.zeros((DI_D,), F32)] * 4)
    cpack = jnp.concatenate([conv_w, conv_b[None], jnp.zeros((3, CONV_DIM), F32)], axis=0)
    zt = jnp.zeros((T_W,), F32)
    tpack = jnp.stack([zt.at[:T_DT].set(shift_mu[3 * DI_D:]), zt.at[T_DT:T_DT + H_C].set(dt_bias),
                       zt.at[T_DT:T_DT + H_C].set(-jnp.exp(a_log.astype(F32)))] + [zt] * 5)
    zw = jnp.zeros((T_W, DI_D), F32)
    w2p = zw.at[:R_W].set(rwkv_w2)
    a2p = zw.at[R_W:R_W + R_A].set(rwkv_a2).astype(BF16)
    g2p = zw.at[R_W + R_A:T_DT].set(rwkv_g2).astype(BF16)
    return w.astype(BF16), (vec, cpack, tpack, w2p, a2p, g2p)


def _top16_desc(cur):
    vals = []
    for _ in range(PEER_TOPK):
        m = jnp.max(cur, axis=0, keepdims=True)
        vals.append(m)
        cur = jnp.where(cur == m, -jnp.inf, cur)
    return vals


PEER_CAND = 112


def _peer_kernel(x_ref, xb_ref, wq_ref, sk_ref, u_ref, v_ref, g_ref, b_ref, o_ref, ob_ref,
                 q_s, s1_s, s2_s, e1_s, e2_s, tau_s, cand_s, w_s, h_s, p_s, pn_s, *, tm, te):
    assert te == 4 * PEER_KEYS
    s = pl.program_id(1)
    ns = pl.num_programs(1)
    nk = PEER_KEYS
    neg = -jnp.inf

    @pl.when(s == 0)
    def _route():
        q_s[...] = lax.dot_general(wq_ref[...], xb_ref[...], NT, preferred_element_type=F32)
        a_idx = lax.broadcasted_iota(jnp.int32, (PEER_TOPK, 1), 0)

        def head(h, carry):
            tops = []
            for c in range(2):
                row0 = pl.multiple_of((2 * h + c) * nk, nk)
                sc = jnp.dot(sk_ref[2 * h + c], q_s[pl.ds(row0, nk), :].astype(BF16), preferred_element_type=F32)
                if c == 0:
                    s1_s[h] = sc
                else:
                    s2_s[h] = sc
                tops.append(_top16_desc(sc))
            t1, t2 = tops
            t1_all = jnp.concatenate(t1, axis=0)
            t2_all = jnp.concatenate(t2, axis=0)
            for b in range(4):
                cand_s[b * PEER_TOPK:(b + 1) * PEER_TOPK, :] = jnp.where(a_idx < PEER_TOPK // (b + 1), t1_all + t2[b], neg)
            for a in range(3):
                ok = (a_idx >= 4) & (a_idx < PEER_TOPK // (a + 1))
                cand_s[(4 + a) * PEER_TOPK:(5 + a) * PEER_TOPK, :] = jnp.where(ok, t1[a] + t2_all, neg)
            best = _top16_desc(cand_s[...])
            mx = t1[0] + t2[0]
            z = jnp.zeros_like(mx)
            for m in best:
                z = z + jnp.exp(m - mx)
            tau_s[pl.ds(h, 1), :] = best[-1]
            e1_s[h] = jnp.exp(s1_s[h] - t1[0])
            e2_s[h] = jnp.exp(s2_s[h] - t2[0]) / z
            return carry

        lax.fori_loop(0, PEER_HEADS, head, 0)
        o_ref[...] = jnp.zeros_like(o_ref)
        p_s[...] = jnp.zeros_like(p_s)

    blk = jnp.minimum(s, ns - 2)
    nj = te // nk
    d = o_ref.shape[1]
    kp, cp = d // 4, d // 8

    def mm1(k):
        part = lax.dot_general(u_ref[:, k * kp:(k + 1) * kp], xb_ref[:, k * kp:(k + 1) * kp], NT,
                               preferred_element_type=F32)
        if k == 0:
            h_s[...] = part
        else:
            h_s[...] += part

    def wbuild(j):
        i1 = blk * nj + j
        w = jnp.zeros((nk, tm), F32)
        for h in range(PEER_HEADS):
            c = s2_s[h] + s1_s[h, pl.ds(i1, 1), :]
            w = w + jnp.where(c >= tau_s[h:h + 1, :], e2_s[h] * e1_s[h, pl.ds(i1, 1), :], 0.0)
        w_s[j * nk:(j + 1) * nk, :] = w

    def act(j):
        hj = h_s[j * nk:(j + 1) * nk, :]
        g = 0.5 * hj * (1.0 + lax.erf(hj * (2.0 ** -0.5)))
        pn_s[:, j * nk:(j + 1) * nk] = jnp.transpose(w_s[j * nk:(j + 1) * nk, :] * g).astype(BF16)

    def mm2(n):
        o_ref[:, n * cp:(n + 1) * cp] += jnp.dot(p_s[...], v_ref[:, n * cp:(n + 1) * cp], preferred_element_type=F32)

    mm1(0); wbuild(0); mm1(1); wbuild(1); mm1(2); wbuild(2); mm1(3)
    mm2(0); wbuild(3); mm2(1); act(0); mm2(2); act(1); mm2(3); act(2); mm2(4); act(3); mm2(5); mm2(6); mm2(7)
    p_s[...] = pn_s[...]

    @pl.when(s == ns - 1)
    def _fin():
        y = _ln_rows(ALPHA * x_ref[...] + o_ref[...], g_ref[...], b_ref[...])
        o_ref[...] = y
        ob_ref[...] = y.astype(BF16)


def peer_ln(x, xb, wq_t, sk, u, v, g, b, *, layer=0, tm=512, te=512):
    m, d = x.shape
    kern = functools.partial(_peer_kernel, tm=tm, te=te)
    const = dict(pipeline_mode=pl.Buffered(1))
    ne = PEER_EXPERTS // te
    route = pltpu.VMEM((PEER_HEADS, PEER_KEYS, tm), F32)
    return pl.pallas_call(
        kern,
        out_shape=(jax.ShapeDtypeStruct((m, d), F32), jax.ShapeDtypeStruct((m, d), BF16)),
        grid=(m // tm, ne + 1),
        in_specs=[pl.BlockSpec((tm, d), lambda i, s: (i, 0), **const),
                  pl.BlockSpec((tm, d), lambda i, s: (i, 0), **const),
                  pl.BlockSpec(wq_t.shape, lambda i, s: (0, 0), **const),
                  pl.BlockSpec(sk.shape, lambda i, s: (0, 0, 0), **const),
                  pl.BlockSpec((None, te, d), lambda i, s: (layer, jnp.minimum(s, ne - 1), 0)),
                  pl.BlockSpec((None, te, d), lambda i, s: (layer, jnp.maximum(s - 1, 0), 0)),
                  pl.BlockSpec((1, d), lambda i, s: (0, 0), **const),
                  pl.BlockSpec((1, d), lambda i, s: (0, 0), **const)],
        out_specs=(pl.BlockSpec((tm, d), lambda i, s: (i, 0)), pl.BlockSpec((tm, d), lambda i, s: (i, 0))),
        scratch_shapes=[pltpu.VMEM((PEER_HEADS * PEER_QDIM, tm), F32), route, route, route, route,
                        pltpu.VMEM((PEER_HEADS, tm), F32), pltpu.VMEM((PEER_CAND, tm), F32),
                        pltpu.VMEM((te, tm), F32), pltpu.VMEM((te, tm), F32),
                        pltpu.VMEM((tm, te), BF16), pltpu.VMEM((tm, te), BF16)],
        compiler_params=pltpu.CompilerParams(dimension_semantics=("arbitrary", "arbitrary"),
                                             vmem_limit_bytes=PEER_VMEM_LIMIT),
        name="peer_ln",
    )(x, xb, wq_t, sk, u, v, g.reshape(1, d), b.reshape(1, d))


def _slots(a):
    return a.reshape(BATCH, SLOT_CHUNKS, CHUNK, a.shape[-1])


def _sample_rows(z4):
    zs = z4[:, P_CHUNKS:P_CHUNKS + S_CHUNKS].reshape(DEC_BATCH, DEC_SEQ, z4.shape[-1])
    return jnp.pad(zs, ((0, 0), (0, S_LEN - DEC_SEQ), (0, 0)))


def _merge_rows(buf4, sample_out):
    f = buf4.shape[-1]
    tail = jnp.concatenate([sample_out[:, :DEC_SEQ].reshape(BATCH, S_CHUNKS, CHUNK, f),
                            jnp.zeros((BATCH, 1, CHUNK, f), buf4.dtype)], axis=1)
    return lax.dynamic_update_slice(buf4, tail, (0, P_CHUNKS, 0, 0)).reshape(M_PAD, f)


def kernel(x_prompt, x_sample, state_hgrn, state_ret, state_ssm, state_conv, state_wkv, state_shift, meta_tokens, ln_g, ln_b, even_w_in, hgrn_lb_logits, hgrn_norm_g, even_w_out, odd_w_in, conv_w, conv_b, dt_bias, a_log, d_skip, ssm_norm_g, shift_mu, rwkv_w0, rwkv_w2, rwkv_a0, rwkv_a2, rwkv_g2, rwkv_k_k, rwkv_k_a, rwkv_r_k, lnx_g, lnx_b, odd_w_out, peer_w_query, peer_sub_keys, peer_u, peer_v):
    dt = x_prompt.dtype
    lead = jnp.concatenate([jnp.zeros((P_LEAD, D_MODEL), dt), meta_tokens.astype(dt)], axis=0)
    x = jnp.concatenate([jnp.broadcast_to(lead[None, None], (BATCH, 1, CHUNK, D_MODEL)),
                         x_prompt.reshape(BATCH, P_CHUNKS - 1, CHUNK, D_MODEL),
                         x_sample.reshape(BATCH, S_CHUNKS, CHUNK, D_MODEL),
                         jnp.zeros((BATCH, 1, CHUNK, D_MODEL), dt)], axis=1).reshape(M_PAD, D_MODEL)
    xb = x.astype(BF16)
    peer_ub, peer_vb = peer_u.astype(BF16), peer_v.astype(BF16)

    cos_p, sin_p = _rope_tables(jnp.arange(P_ROWS) - P_LEAD)
    cos_s, sin_s = _rope_tables(PAST_LEN + jnp.arange(S_LEN))
    lb_table = jnp.cumsum(jax.nn.softmax(hgrn_lb_logits.astype(F32), axis=0), axis=0)

    z = matmul(xb, even_w_in[0].astype(BF16), tm=TOK_TILE, tn=1024, name="even_in")
    lb = lb_table[0].reshape(1, -1)
    ng = hgrn_norm_g[0].reshape(1, -1)
    z = _slots(z)
    mix_p, hgrn_p, ret_p = even_mixer_prompt(z, lb, ng, cos_p, sin_p)
    mix_s, hgrn_s, ret_s = even_mixer_sample(_sample_rows(z), lb, ng, cos_s, sin_s, state_hgrn[0], state_ret[0])
    mix = _merge_rows(mix_p, mix_s)
    x, xb = proj_ln(x, mix, even_w_out[0].astype(BF16), ln_g[0, 0], ln_b[0, 0], tm=TOK_TILE, name="even_out")
    x, xb = peer_ln(x, xb, peer_w_query[0].T.astype(BF16),
                    peer_sub_keys[0].reshape(2 * PEER_HEADS, PEER_KEYS, PEER_QDIM // 2).astype(BF16),
                    peer_ub, peer_vb, ln_g[0, 1], ln_b[0, 1], layer=0)

    w_in1, params = _odd_params(odd_w_in[0], conv_w[0], conv_b[0], dt_bias[0], a_log[0], d_skip[0], ssm_norm_g[0],
                                shift_mu[0], rwkv_w0[0], rwkv_w2[0], rwkv_a0[0], rwkv_a2[0], rwkv_g2[0], rwkv_k_k[0],
                                rwkv_k_a[0], rwkv_r_k[0], lnx_g[0], lnx_b[0])
    z = _slots(matmul(xb, w_in1, tm=TOK_TILE, tn=1024, name="odd_in"))
    mix_p, ssm_p, wkv_p, conv_p, shift_p = odd_mixer_prompt(z, params)
    conv8 = jnp.pad(state_conv[0], ((0, 0), (8 - (CONV_W - 1), 0), (0, 0)))
    shift8 = jnp.pad(state_shift[0][:, None, :], ((0, 0), (7, 0), (0, SHIFT_W - SHIFT_DIM)))
    mix_s, ssm_s, wkv_s, conv_s, shift_s = odd_mixer_sample(_sample_rows(z), params,
                                                            (state_ssm[0], state_wkv[0], conv8, shift8))
    mix = _merge_rows(mix_p, mix_s)
    x, xb = proj_ln(x, mix, odd_w_out[0].astype(BF16), ln_g[1, 0], ln_b[1, 0], tm=TOK_TILE, name="odd_out")
    x, xb = peer_ln(x, xb, peer_w_query[1].T.astype(BF16),
                    peer_sub_keys[1].reshape(2 * PEER_HEADS, PEER_KEYS, PEER_QDIM // 2).astype(BF16),
                    peer_ub, peer_vb, ln_g[1, 1], ln_b[1, 1], layer=1)

    x4 = _slots(x)
    y_prompt = x4[:, 1:P_CHUNKS].reshape(BATCH, SEQ, D_MODEL)
    y_sample = x4[:, P_CHUNKS:P_CHUNKS + S_CHUNKS].reshape(DEC_BATCH, DEC_SEQ, D_MODEL)
    nc = CONV_W - 1
    return (y_prompt, y_sample, hgrn_p[None], hgrn_s[None], ret_p[None], ret_s[None], ssm_p[None], ssm_s[None],
            conv_p[None, :, 8 - nc:], conv_s[None, :, DEC_SEQ - nc:DEC_SEQ], wkv_p[None], wkv_s[None],
            shift_p[None, :, 7, :SHIFT_DIM], shift_s[None, :, DEC_SEQ - 1, :SHIFT_DIM])
```

```python
import functools
import math

import jax
import jax.numpy as jnp
from jax import lax
from jax.experimental import pallas as pl
from jax.experimental.pallas import tpu as pltpu

D_MODEL = 2048
BATCH = 4
SEQ = 2048
DEPTH = 2
DEC_BATCH = 128
DEC_SEQ = 4
PAST_LEN = 16384
N_META = 16
CHUNK = 64

H_A, DK_A, DV_A = 8, 128, 128
H_B, DK_B, DV_B = 4, 128, 256
EVEN_IN = 4 * H_A * DK_A + 2 * H_B * DK_B + 2 * H_B * DV_B
EVEN_MIX = H_A * DV_A + H_B * DV_B

H_C, P_C, N_C, G_C, CONV_W = 16, 64, 128, 2, 4
DI_C = H_C * P_C
CONV_DIM = DI_C + 2 * G_C * N_C
H_D, P_D = 16, 64
DI_D = H_D * P_D
R_W, R_A, R_G = 64, 64, 160
SHIFT_DIM = 3 * DI_D + R_W + R_A + R_G
ODD_IN = DI_C + CONV_DIM + H_C + SHIFT_DIM
ODD_MIX = DI_C + DI_D

PEER_KEYS = 128
PEER_EXPERTS = PEER_KEYS * PEER_KEYS
PEER_HEADS = 8
PEER_TOPK = 16
PEER_QDIM = 256

ALPHA = (2.0 * DEPTH) ** 0.25
LN_EPS = 1e-5
RMS_EPS = 1e-6
RWKV_GN_EPS = 64e-5
ROPE_BASE = 10000.0

F32 = jnp.float32
BF16 = jnp.bfloat16

P_LEAD = CHUNK - N_META
P_ROWS = CHUNK + SEQ
P_CHUNKS = P_ROWS // CHUNK
S_CHUNKS = DEC_BATCH * DEC_SEQ // (BATCH * CHUNK)
SLOT_CHUNKS = P_CHUNKS + S_CHUNKS + 1
S_LEN = 8
SAMPLE_GROUP = 4
ODD_SAMPLE_GROUP = 4
TOK_TILE = 512
M_PAD = BATCH * SLOT_CHUNKS * CHUNK
assert M_PAD % 1024 == 0 and DEC_BATCH * DEC_SEQ == BATCH * S_CHUNKS * CHUNK

VMEM_LIMIT = 56 * 1024 * 1024
PEER_VMEM_LIMIT = 60 * 1024 * 1024


def _dot_bf(a, b, dims=(((1,), (0,)), ((), ()))):
    return lax.dot_general(a.astype(BF16), b.astype(BF16), dims, preferred_element_type=F32)


def _parts(x, n):
    out, rem = [], x
    for i in range(n):
        p = rem.astype(BF16)
        out.append(p)
        if i + 1 < n:
            rem = rem - p.astype(F32)
    return out


def _dot_sel(x, sel, dims=(((1,), (0,)), ((), ())), n=3, sel_first=False):
    sel = sel.astype(BF16)
    acc = None
    for p in _parts(x, n):
        d = lax.dot_general(sel, p, dims, preferred_element_type=F32) if sel_first else \
            lax.dot_general(p, sel, dims, preferred_element_type=F32)
        acc = d if acc is None else acc + d
    return acc


NT = (((1,), (1,)), ((), ()))
TN = (((0,), (0,)), ((), ()))


def _sigmoid(x):
    return 1.0 / (1.0 + jnp.exp(-x))


def _silu(x):
    return x * _sigmoid(x)


def _mm_kernel(x_ref, w_ref, o_ref):
    o_ref[...] = jnp.dot(x_ref[...].astype(BF16), w_ref[...], preferred_element_type=F32).astype(o_ref.dtype)


def matmul(x, w, *, tm, tn, out_dtype=F32, name="matmul"):
    m, k = x.shape
    n = w.shape[1]
    return pl.pallas_call(
        _mm_kernel,
        out_shape=jax.ShapeDtypeStruct((m, n), out_dtype),
        grid=(n // tn, m // tm),
        in_specs=[pl.BlockSpec((tm, k), lambda j, i: (i, 0)),
                  pl.BlockSpec((k, tn), lambda j, i: (0, j))],
        out_specs=pl.BlockSpec((tm, tn), lambda j, i: (i, j)),
        compiler_params=pltpu.CompilerParams(dimension_semantics=("arbitrary", "arbitrary"),
                                             vmem_limit_bytes=VMEM_LIMIT),
        name=name,
    )(x, w)


def _ln_rows(v, g, b):
    mu = jnp.mean(v, -1, keepdims=True)
    c = v - mu
    var = jnp.mean(c * c, -1, keepdims=True)
    return c * lax.rsqrt(var + LN_EPS) * g + b


def _proj_ln_kernel(x_ref, m_ref, w_ref, g_ref, b_ref, o_ref, ob_ref):
    acc = jnp.dot(m_ref[...].astype(BF16), w_ref[...], preferred_element_type=F32)
    y = _ln_rows(ALPHA * x_ref[...] + acc, g_ref[...], b_ref[...])
    o_ref[...] = y
    ob_ref[...] = y.astype(BF16)


def proj_ln(x, mix, w, g, b, *, tm, name="proj_ln"):
    m, d = x.shape
    k = mix.shape[1]
    return pl.pallas_call(
        _proj_ln_kernel,
        out_shape=(jax.ShapeDtypeStruct((m, d), F32), jax.ShapeDtypeStruct((m, d), BF16)),
        grid=(m // tm,),
        in_specs=[pl.BlockSpec((tm, d), lambda i: (i, 0)),
                  pl.BlockSpec((tm, k), lambda i: (i, 0)),
                  pl.BlockSpec((k, d), lambda i: (0, 0)),
                  pl.BlockSpec((1, d), lambda i: (0, 0)),
                  pl.BlockSpec((1, d), lambda i: (0, 0))],
        out_specs=(pl.BlockSpec((tm, d), lambda i: (i, 0)), pl.BlockSpec((tm, d), lambda i: (i, 0))),
        compiler_params=pltpu.CompilerParams(dimension_semantics=("arbitrary",),
                                             vmem_limit_bytes=VMEM_LIMIT),
        name=name,
    )(x, mix, w, g.reshape(1, d), b.reshape(1, d))


def _row_info(L, lead, nvalid):
    r = lax.broadcasted_iota(jnp.int32, (L, 1), 0)
    return r, (r >= lead) & (r < lead + nvalid)


def _seq_views(ref, G):
    if len(ref.shape) == 2:
        return [ref]
    if len(ref.shape) == 3:
        return [ref.at[g] for g in range(G)]
    return [ref.at[g, 0] for g in range(G)]


def _tril(L, strict=False):
    r = lax.broadcasted_iota(jnp.int32, (L, L), 0)
    c = lax.broadcasted_iota(jnp.int32, (L, L), 1)
    return (c < r) if strict else (c <= r)


def _hgrn_heads(qs, ks, vs, logfs, s0s, L):
    n = range(len(qs))
    tril = _tril(L).astype(F32)
    ones_l = jnp.ones((L, DK_A), F32)
    ones_k = jnp.ones((DK_A, DK_A), BF16)
    b = [_dot_sel(logfs[i], tril, sel_first=True) for i in n]
    b_end_col = [_dot_sel(logfs[i], ones_l, TN) for i in n]
    o = [_dot_bf(qs[i] * jnp.exp(b[i]), s0s[i]) for i in n]
    upd = [_dot_bf(ks[i] * jnp.exp(b[i][L - 1:L, :] - b[i]), vs[i], TN) for i in n]
    s_new = [jnp.exp(b_end_col[i]) * s0s[i] + upd[i] for i in n]
    sub = min(16, L)
    t_idx = lax.broadcasted_iota(jnp.int32, (sub, 1), 0)
    outs = [[] for _ in n]
    for blk in range(L // sub):
        r0 = blk * sub
        acc = [o[i][r0:r0 + sub] for i in n]
        if blk > 0:
            a = [_dot_bf(qs[i][r0:r0 + sub] * jnp.exp(b[i][r0:r0 + sub] - b[i][r0 - 1:r0, :]),
                         ks[i][0:r0] * jnp.exp(b[i][r0 - 1:r0, :] - b[i][0:r0]), NT) for i in n]
            acc = [acc[i] + _dot_bf(a[i], vs[i][0:r0]) for i in n]
        prod = [jnp.concatenate(
            [qs[i][r0:r0 + sub] * ks[i][r0 + s:r0 + s + 1, :] *
             jnp.exp(jnp.where(t_idx >= s, b[i][r0:r0 + sub] - b[i][r0 + s:r0 + s + 1, :], -jnp.inf))
             for s in range(sub)], axis=0).astype(BF16) for i in n]
        score = [jnp.dot(prod[i], ones_k, preferred_element_type=F32) for i in n]
        for i in n:
            t = acc[i]
            for s in range(sub):
                t = t + score[i][s * sub:(s + 1) * sub, :] * vs[i][r0 + s:r0 + s + 1, :]
            outs[i].append(t)
    return [jnp.concatenate(x, axis=0) if len(x) > 1 else x[0] for x in outs], s_new


def _even_kernel(*refs, L, nvalid, chunks, has_s0, G):
    if has_s0:
        z_ref, lb_ref, ng_ref, cos_ref, sin_ref, sh0_ref, sr0_ref, o_ref, sho_ref, sro_ref, sh_s, sr_s = refs
    else:
        z_ref, lb_ref, ng_ref, cos_ref, sin_ref, o_ref, sho_ref, sro_ref, sh_s, sr_s = refs
    c = pl.program_id(1)

    @pl.when(c == 0)
    def _():
        if has_s0:
            for g in range(G):
                sh_s[g * H_A:(g + 1) * H_A] = sh0_ref[g]
                sr_s[g * H_B:(g + 1) * H_B] = sr0_ref[g]
        else:
            sh_s[...] = jnp.zeros_like(sh_s)
            sr_s[...] = jnp.zeros_like(sr_s)

    lead = jnp.where(c == 0, P_LEAD, 0) if chunks > 1 else 0
    r, valid = _row_info(L, lead, nvalid)
    cnt_col = jnp.clip(r + 1 - lead, 0, nvalid).astype(F32)
    rr = lax.broadcasted_iota(jnp.int32, (1, L), 1)
    cnt_row = jnp.clip(rr + 1 - lead, 0, nvalid).astype(F32)

    off_q, off_f, off_i, off_g = 0, H_A * DK_A, 2 * H_A * DK_A, 2 * H_A * DK_A + H_A * DV_A
    base_b = 2 * H_A * DK_A + 2 * H_A * DV_A
    off_bq, off_bk, off_bv = base_b, base_b + H_B * DK_B, base_b + 2 * H_B * DK_B
    off_bg = off_bv + H_B * DV_B
    ng = ng_ref[...]
    zv, ov = _seq_views(z_ref, G), _seq_views(o_ref, G)
    ia = [(g, h) for g in range(G) for h in range(H_A)]
    ib = [(g, h) for g in range(G) for h in range(H_B)]
    na, nb_ = range(len(ia)), range(len(ib))
    af = [zv[g][:, off_f + h * DK_A: off_f + (h + 1) * DK_A] for g, h in ia]
    lbs = [lb_ref[:, h * DK_A:(h + 1) * DK_A] for g, h in ia]
    logf = [jnp.where(valid, jnp.log(lbs[i] + (1.0 - lbs[i]) * _sigmoid(af[i])), 0.0) for i in na]
    ka = [jnp.where(valid, (1.0 - lbs[i]) * _sigmoid(-af[i]), 0.0) for i in na]
    qa = [_silu(zv[g][:, off_q + h * DK_A: off_q + (h + 1) * DK_A]) for g, h in ia]
    va = [zv[g][:, off_i + h * DV_A: off_i + (h + 1) * DV_A] for g, h in ia]
    cosv, sinv = cos_ref[...], sin_ref[...]
    bq = [zv[g][:, off_bq + h * DK_B: off_bq + (h + 1) * DK_B] for g, h in ib]
    bk = [zv[g][:, off_bk + h * DK_B: off_bk + (h + 1) * DK_B] for g, h in ib]
    vb = [zv[g][:, off_bv + h * DV_B: off_bv + (h + 1) * DV_B] for g, h in ib]
    qb = [bq[i] * cosv + pltpu.roll(bq[i], DK_B // 2, 1) * sinv for i in nb_]
    kb = [jnp.where(valid, (bk[i] * cosv + pltpu.roll(bk[i], DK_B // 2, 1) * sinv) * (DK_B ** -0.5), 0.0) for i in nb_]
    lg = [math.log(1.0 - 2.0 ** (-5.0 - h)) for g, h in ib]

    oa, sa_new = _hgrn_heads(qa, ka, va, logf, [sh_s[i] for i in na], L)
    causal = _tril(L)
    cnt_end = cnt_col[L - 1:L, :]
    scores = [_dot_bf(qb[i], kb[i], NT) * jnp.where(causal, jnp.exp((cnt_col - cnt_row) * lg[i]), 0.0) for i in nb_]
    inter = [_dot_bf(qb[i] * jnp.exp(cnt_col * lg[i]), sr_s[i]) for i in nb_]
    ob = [_dot_bf(scores[i], vb[i]) + inter[i] for i in nb_]
    updb = [_dot_bf(kb[i] * jnp.exp((cnt_end - cnt_col) * lg[i]), vb[i], TN) for i in nb_]
    for i, (g, h) in enumerate(ia):
        sh_s[i] = sa_new[i]
        o = oa[i] * lax.rsqrt(jnp.mean(oa[i] * oa[i], -1, keepdims=True) + RMS_EPS) * ng
        ov[g][:, h * DV_A:(h + 1) * DV_A] = o * _silu(zv[g][:, off_g + h * DV_A: off_g + (h + 1) * DV_A])
    for i, (g, h) in enumerate(ib):
        sr_s[i] = jnp.exp(cnt_end * lg[i]) * sr_s[i] + updb[i]
        mu = jnp.mean(ob[i], -1, keepdims=True)
        cc = ob[i] - mu
        o = cc * lax.rsqrt(jnp.mean(cc * cc, -1, keepdims=True) + LN_EPS)
        ov[g][:, H_A * DV_A + h * DV_B: H_A * DV_A + (h + 1) * DV_B] = \
            o * _silu(zv[g][:, off_bg + h * DV_B: off_bg + (h + 1) * DV_B])

    @pl.when(c == chunks - 1)
    def _():
        for g in range(G):
            sho_ref[g] = sh_s[g * H_A:(g + 1) * H_A]
            sro_ref[g] = sr_s[g * H_B:(g + 1) * H_B]


def even_mixer_prompt(z, lb, ng, cos, sin, *, chunks=P_CHUNKS):
    nb = z.shape[0]
    kern = functools.partial(_even_kernel, L=CHUNK, nvalid=CHUNK, chunks=chunks, has_s0=False, G=nb)
    blk = lambda b, c: (0, c, 0, 0)
    whole = lambda b, c: (0, 0, 0, 0)
    return pl.pallas_call(
        kern,
        out_shape=(jax.ShapeDtypeStruct(z.shape[:3] + (EVEN_MIX,), F32),
                   jax.ShapeDtypeStruct((nb, H_A, DK_A, DV_A), F32),
                   jax.ShapeDtypeStruct((nb, H_B, DK_B, DV_B), F32)),
        grid=(1, chunks),
        in_specs=[pl.BlockSpec((nb, 1, CHUNK, EVEN_IN), blk),
                  pl.BlockSpec((1, H_A * DK_A), lambda b, c: (0, 0)),
                  pl.BlockSpec((1, DV_A), lambda b, c: (0, 0)),
                  pl.BlockSpec((CHUNK, DK_B), lambda b, c: (c, 0)),
                  pl.BlockSpec((CHUNK, DK_B), lambda b, c: (c, 0))],
        out_specs=(pl.BlockSpec((nb, 1, CHUNK, EVEN_MIX), blk),
                   pl.BlockSpec((nb, H_A, DK_A, DV_A), whole),
                   pl.BlockSpec((nb, H_B, DK_B, DV_B), whole)),
        scratch_shapes=[pltpu.VMEM((nb * H_A, DK_A, DV_A), F32), pltpu.VMEM((nb * H_B, DK_B, DV_B), F32)],
        compiler_params=pltpu.CompilerParams(dimension_semantics=("arbitrary", "arbitrary"),
                                             vmem_limit_bytes=VMEM_LIMIT),
        name="even_mixer_prompt",
    )(z, lb, ng, cos, sin)


def even_mixer_sample(z, lb, ng, cos, sin, s_hgrn, s_ret):
    nb = z.shape[0]
    G = SAMPLE_GROUP
    kern = functools.partial(_even_kernel, L=S_LEN, nvalid=DEC_SEQ, chunks=1, has_s0=True, G=G)
    return pl.pallas_call(
        kern,
        out_shape=(jax.ShapeDtypeStruct((nb, S_LEN, EVEN_MIX), F32),
                   jax.ShapeDtypeStruct((nb, H_A, DK_A, DV_A), F32),
                   jax.ShapeDtypeStruct((nb, H_B, DK_B, DV_B), F32)),
        grid=(nb // G, 1),
        in_specs=[pl.BlockSpec((G, S_LEN, EVEN_IN), lambda b, c: (b, 0, 0)),
                  pl.BlockSpec((1, H_A * DK_A), lambda b, c: (0, 0)),
                  pl.BlockSpec((1, DV_A), lambda b, c: (0, 0)),
                  pl.BlockSpec((S_LEN, DK_B), lambda b, c: (0, 0)),
                  pl.BlockSpec((S_LEN, DK_B), lambda b, c: (0, 0)),
                  pl.BlockSpec((G, H_A, DK_A, DV_A), lambda b, c: (b, 0, 0, 0)),
                  pl.BlockSpec((G, H_B, DK_B, DV_B), lambda b, c: (b, 0, 0, 0))],
        out_specs=(pl.BlockSpec((G, S_LEN, EVEN_MIX), lambda b, c: (b, 0, 0)),
                   pl.BlockSpec((G, H_A, DK_A, DV_A), lambda b, c: (b, 0, 0, 0)),
                   pl.BlockSpec((G, H_B, DK_B, DV_B), lambda b, c: (b, 0, 0, 0))),
        scratch_shapes=[pltpu.VMEM((G * H_A, DK_A, DV_A), F32), pltpu.VMEM((G * H_B, DK_B, DV_B), F32)],
        compiler_params=pltpu.CompilerParams(dimension_semantics=("arbitrary", "arbitrary"),
                                             vmem_limit_bytes=VMEM_LIMIT),
        name="even_mixer_sample",
    )(z, lb, ng, cos, sin, s_hgrn, s_ret)


def _rope_tables(pos):
    half = DK_B // 2
    inv = ROPE_BASE ** (-jnp.arange(half, dtype=F32) / half)
    ang = pos.astype(F32)[:, None] * inv
    cos, sin = jnp.cos(ang), jnp.sin(ang)
    return jnp.concatenate([cos, cos], -1), jnp.concatenate([-sin, sin], -1)


ODD_PAD = 6144
O_Z, O_X, O_R, O_K, O_V, O_T = 0, DI_C, DI_C + CONV_DIM, DI_C + CONV_DIM + DI_D, DI_C + CONV_DIM + 2 * DI_D, \
    DI_C + CONV_DIM + 3 * DI_D
T_W = 384
T_DT = R_W + R_A + R_G
SHIFT_W = 3 * DI_D + T_W
HALF = 64


def _softplus(x):
    return jnp.maximum(x, 0.0) + jnp.log1p(jnp.exp(-jnp.abs(x)))


def _lane_lt(n, width):
    return lax.broadcasted_iota(jnp.int32, (1, width), 1) < n


def _pair_ones():
    r = lax.broadcasted_iota(jnp.int32, (2 * HALF, 2 * HALF), 0) < HALF
    c = lax.broadcasted_iota(jnp.int32, (2 * HALF, 2 * HALF), 1) < HALF
    return r == c


def _stack_pair(x, m_a):
    return jnp.concatenate([jnp.where(m_a, x, 0.0), jnp.where(m_a, 0.0, x)], axis=0)


def _rwkv_pairs(rs, ks_, vs_, als, bes, logws, Gs, bds, L):
    n = range(len(rs))
    m_a = _lane_lt(HALF, 2 * HALF)
    m_l = _lane_lt(L, 2 * L)
    ri = lax.broadcasted_iota(jnp.int32, (2 * L, 2 * L), 0)
    ci = lax.broadcasted_iota(jnp.int32, (2 * L, 2 * L), 1)
    cm = jnp.where(ci >= L, ci - L, ci)
    keep = cm < jnp.where(ri >= L, ri - L + 1, ri)
    eye = (ri == ci).astype(F32)
    blk4 = (ri // 4) == (ci // 4)
    pair_blk = _pair_ones()

    e_inv = [jnp.exp(-Gs[i]) for i in n]
    lhs = [jnp.concatenate([als[i] * jnp.exp(Gs[i] - logws[i]), rs[i] * jnp.exp(Gs[i])], axis=0).astype(BF16) for i in n]
    ks = [_stack_pair(ks_[i] * e_inv[i], m_a).astype(BF16) for i in n]
    bs = [_stack_pair(bes[i] * e_inv[i], m_a).astype(BF16) for i in n]
    vs = [_stack_pair(vs_[i], m_a).astype(BF16) for i in n]
    a_k = [jnp.where(keep, _dot_bf(lhs[i], ks[i], NT), 0.0) for i in n]
    a_b = [jnp.where(keep, _dot_bf(lhs[i], bs[i], NT), 0.0) for i in n]
    sb = [_dot_bf(lhs[i], bds[i], NT) for i in n]
    base = [sb[i] + _dot_bf(a_k[i], vs[i]) for i in n]
    n_bd = [jnp.concatenate([jnp.where(m_l, a_b[i][0:L], 0.0), jnp.where(m_l, 0.0, a_b[i][0:L])], axis=0) for i in n]
    n4 = [jnp.where(blk4, n_bd[i], 0.0) for i in n]
    n4sq = [_dot_bf(n4[i], n4[i]) for i in n]
    t_inv = [eye - n4[i] for i in n]
    t_inv = [t_inv[i] + _dot_bf(t_inv[i], n4sq[i]) for i in n]
    blk = 4
    while blk < L:
        off = ((ri // (2 * blk)) == (ci // (2 * blk))) & ((ri % (2 * blk)) >= blk) & ((ci % (2 * blk)) < blk)
        ct = [_dot_bf(jnp.where(off, n_bd[i], 0.0), t_inv[i]) for i in n]
        t_inv = [t_inv[i] - _dot_bf(t_inv[i], ct[i]) for i in n]
        blk *= 2
    u = [_dot_bf(t_inv[i][0:L, :] + t_inv[i][L:2 * L, :], _stack_pair(base[i][0:L], m_a)) for i in n]
    y = [base[i][L:2 * L] - _dot_bf(a_b[i][L:2 * L], _stack_pair(u[i], m_a)) for i in n]
    gl = [Gs[i][L - 1:L, :] for i in n]
    dec = [jnp.exp(gl[i] - Gs[i]) for i in n]
    upd = [_dot_bf(jnp.concatenate([vs_[i], u[i]], axis=0),
                   jnp.concatenate([ks_[i] * dec[i], -(bes[i] * dec[i])], axis=0), TN) for i in n]
    bd_new = [bds[i] * jnp.exp(gl[i]) + jnp.where(pair_blk, upd[i], 0.0) for i in n]
    return y, bd_new


def _ssd_pair(xdt, cg, bg, cb, bcs, brows, sp, L):
    m_a = _lane_lt(HALF, 2 * HALF)
    causal = _tril(L)
    out = jnp.zeros((L, 2 * HALF), F32)
    ends = [bc[L - 1:L, :] for bc in bcs]
    sp_new = sp * jnp.where(m_a, jnp.exp(ends[0]), jnp.exp(ends[1]))
    for x in range(2):
        keep = m_a if x == 0 else jnp.logical_not(m_a)
        seg = jnp.where(causal, jnp.exp(bcs[x] - brows[x]), 0.0)
        xm = jnp.where(keep, xdt, 0.0)
        out = out + _dot_bf(cb * seg, xm) + _dot_bf(cg * jnp.exp(bcs[x]), jnp.where(keep, sp, 0.0))
        sp_new = sp_new + _dot_bf(bg * jnp.exp(ends[x] - bcs[x]), xm, TN)
    return out, sp_new


def _odd_kernel(*refs, L, nvalid, chunks, has_s0, G):
    if has_s0:
        (z_ref, vec_ref, cp_ref, tp_ref, w2_ref, a2_ref, g2_ref, ssm0_ref, wkv0_ref, conv0_ref, shift0_ref,
         o_ref, ssmo_ref, wkvo_ref, convo_ref, shifto_ref, ssm_s, wkv_s, conv_c, shift_c) = refs
    else:
        (z_ref, vec_ref, cp_ref, tp_ref, w2_ref, a2_ref, g2_ref,
         o_ref, ssmo_ref, wkvo_ref, convo_ref, shifto_ref, ssm_s, wkv_s, conv_c, shift_c) = refs
    c = pl.program_id(1)
    npair = H_C // 2

    @pl.when(c == 0)
    def _():
        if has_s0:
            zpad = jnp.zeros((HALF, HALF), F32)
            for g in range(G):
                for p in range(npair):
                    ssm_s[g * npair + p] = jnp.concatenate([ssm0_ref[g, 2 * p], ssm0_ref[g, 2 * p + 1]], axis=1)
                    wkv_s[g * npair + p] = jnp.concatenate(
                        [jnp.concatenate([wkv0_ref[g, 2 * p], zpad], axis=1),
                         jnp.concatenate([zpad, wkv0_ref[g, 2 * p + 1]], axis=1)], axis=0)
                conv_c[g * 8:(g + 1) * 8, :] = conv0_ref[g]
                shift_c[g * 8:(g + 1) * 8, :] = shift0_ref[g]
        else:
            ssm_s[...] = jnp.zeros_like(ssm_s)
            wkv_s[...] = jnp.zeros_like(wkv_s)
            conv_c[...] = jnp.zeros_like(conv_c)
            shift_c[...] = jnp.zeros_like(shift_c)

    lead = jnp.where(c == 0, P_LEAD, 0) if chunks > 1 else 0
    r_idx, valid = _row_info(L, lead, nvalid)
    m_a = _lane_lt(HALF, 2 * HALF)
    tril = _tril(L).astype(F32)
    triu = (lax.broadcasted_iota(jnp.int32, (L, L), 0) <= lax.broadcasted_iota(jnp.int32, (L, L), 1)).astype(F32)
    ones_blk = _pair_ones().astype(F32)
    r8 = lax.broadcasted_iota(jnp.int32, (8, 1), 0)
    zv, ov = _seq_views(z_ref, G), _seq_views(o_ref, G)
    pairs = range(npair)
    sls = [slice(p * 2 * HALF, (p + 1) * 2 * HALF) for p in pairs]
    w2_hi, w2_lo = _parts(w2_ref[...], 2)

    def shift_mix(p_raw, carry_row, mu):
        pv = jnp.where(valid, p_raw, 0.0)
        prev = jnp.where(r_idx == 0, carry_row, pltpu.roll(pv, 1, 0))
        return pv, pv + (prev - pv) * mu

    rw = []
    for g in range(G):
        zg, og = zv[g], ov[g]
        cs = slice(g * 8, (g + 1) * 8)
        xbc = jnp.where(valid, zg[:, O_X:O_X + CONV_DIM], 0.0)
        c8 = conv_c[cs, :]
        conv = cp_ref[CONV_W:CONV_W + 1, :] + xbc * cp_ref[CONV_W - 1:CONV_W, :]
        for j in range(1, CONV_W):
            rolled = pltpu.roll(xbc, j, 0)
            head = jnp.where(r8 < j, pltpu.roll(c8, j, 0), rolled[0:8])
            sh = jnp.concatenate([head, rolled[8:]], axis=0) if L > 8 else head
            conv = conv + sh * cp_ref[CONV_W - 1 - j:CONV_W - j, :]
        conv_c[cs, :] = xbc[L - 8:L]
        act = _silu(conv)
        xc, bm, cm = act[:, :DI_C], act[:, DI_C:DI_C + G_C * N_C], act[:, DI_C + G_C * N_C:]

        t_raw = zg[:, O_T:O_T + T_W]
        dt = jnp.where(valid, _softplus(t_raw + tp_ref[1:2, :])[:, T_DT:T_DT + H_C], 0.0)
        logf = dt * tp_ref[2:3, T_DT:T_DT + H_C]
        bc_all = _dot_sel(logf, tril, sel_first=True)
        brow_all = _dot_sel(logf, triu, TN)
        per_g = npair // G_C
        for grp in range(G_C):
            cg = cm[:, grp * N_C:(grp + 1) * N_C]
            bg = bm[:, grp * N_C:(grp + 1) * N_C]
            cb = _dot_bf(cg, bg, NT)
            ys, ss = [], jnp.zeros((L, 1), F32)
            for pp in range(per_g):
                p = grp * per_g + pp
                sl = sls[p]
                h_a, h_b = 2 * p, 2 * p + 1
                xcp = xc[:, sl]
                xdt = xcp * jnp.where(m_a, dt[:, h_a:h_a + 1], dt[:, h_b:h_b + 1])
                o, sp_new = _ssd_pair(xdt, cg, bg, cb, [bc_all[:, h_a:h_a + 1], bc_all[:, h_b:h_b + 1]],
                                      [brow_all[h_a:h_a + 1, :], brow_all[h_b:h_b + 1, :]], ssm_s[g * npair + p], L)
                ssm_s[g * npair + p] = sp_new
                y = (o + xcp * vec_ref[0:1, sl]) * _silu(zg[:, O_Z + p * 2 * HALF:O_Z + (p + 1) * 2 * HALF])
                ys.append(y)
                ss = ss + jnp.sum(y * y, axis=-1, keepdims=True)
            scale = lax.rsqrt(ss / (DI_C // G_C) + RMS_EPS)
            for pp in range(per_g):
                sl = sls[grp * per_g + pp]
                og[:, sl] = ys[pp] * scale * vec_ref[1:2, sl]

        r_raw, r = shift_mix(zg[:, O_R:O_R + DI_D], shift_c[g * 8 + 7:g * 8 + 8, 0:DI_D], vec_ref[2:3, :])
        k_raw, k = shift_mix(zg[:, O_K:O_K + DI_D], shift_c[g * 8 + 7:g * 8 + 8, DI_D:2 * DI_D], vec_ref[3:4, :])
        v_raw, v = shift_mix(zg[:, O_V:O_V + DI_D], shift_c[g * 8 + 7:g * 8 + 8, 2 * DI_D:3 * DI_D], vec_ref[4:5, :])
        t_rawm, tm = shift_mix(t_raw, shift_c[g * 8 + 7:g * 8 + 8, 3 * DI_D:SHIFT_W], tp_ref[0:1, :])
        shift_c[cs, 0:DI_D] = r_raw[L - 8:L]
        shift_c[cs, DI_D:2 * DI_D] = k_raw[L - 8:L]
        shift_c[cs, 2 * DI_D:3 * DI_D] = v_raw[L - 8:L]
        shift_c[cs, 3 * DI_D:SHIFT_W] = t_rawm[L - 8:L]

        th_hi, th_lo = _parts(jnp.tanh(tm), 2)
        w_pre = vec_ref[5:6, :] + (jnp.dot(th_hi, w2_hi, preferred_element_type=F32) +
                                   jnp.dot(th_lo, w2_hi, preferred_element_type=F32) +
                                   jnp.dot(th_hi, w2_lo, preferred_element_type=F32))
        logw = jnp.where(valid, -jnp.exp(-_softplus(-w_pre) - 0.5), 0.0)
        g_cum = _dot_sel(logw, tril, sel_first=True)
        a = _sigmoid(vec_ref[6:7, :] + _dot_bf(tm, a2_ref[...]))
        gate = _dot_bf(_sigmoid(tm), g2_ref[...])
        kkr = k * vec_ref[7:8, :]
        k2 = k * (1.0 + (a - 1.0) * vec_ref[8:9, :])
        rw.append(dict(r=r, v=v, a=a, gate=gate, kkr=kkr, k2=k2, rk=r * k2 * vec_ref[9:10, :], logw=logw, g_cum=g_cum))

    gp = [(g, p) for g in range(G) for p in pairs]
    n = range(len(gp))
    nrm = [jnp.sqrt(_dot_sel(rw[g]["kkr"][:, sls[p]] * rw[g]["kkr"][:, sls[p]], ones_blk, n=2)) for g, p in gp]
    al = [jnp.where(valid, rw[g]["kkr"][:, sls[p]] / jnp.maximum(nrm[i], 1e-12), 0.0) for i, (g, p) in enumerate(gp)]
    be = [al[i] * rw[g]["a"][:, sls[p]] for i, (g, p) in enumerate(gp)]
    ys, bd_new = _rwkv_pairs([rw[g]["r"][:, sls[p]] for g, p in gp],
                             [jnp.where(valid, rw[g]["k2"][:, sls[p]], 0.0) for g, p in gp],
                             [rw[g]["v"][:, sls[p]] for g, p in gp], al, be,
                             [rw[g]["logw"][:, sls[p]] for g, p in gp], [rw[g]["g_cum"][:, sls[p]] for g, p in gp],
                             [wkv_s[i] for i in n], L)
    for i in n:
        wkv_s[i] = bd_new[i]
    mu = [_dot_sel(ys[i], ones_blk, n=2) * (1.0 / P_D) for i in n]
    yc = [ys[i] - mu[i] for i in n]
    var = [_dot_sel(yc[i] * yc[i], ones_blk, n=2) * (1.0 / P_D) for i in n]
    bonus = [_dot_sel(rw[g]["rk"][:, sls[p]], ones_blk, n=2) for g, p in gp]
    for i, (g, p) in enumerate(gp):
        sl = sls[p]
        yn = yc[i] * lax.rsqrt(var[i] + RWKV_GN_EPS) * vec_ref[10:11, sl] + vec_ref[11:12, sl]
        ov[g][:, DI_C + p * 2 * HALF:DI_C + (p + 1) * 2 * HALF] = (yn + bonus[i] * rw[g]["v"][:, sl]) * rw[g]["gate"][:, sl]

    @pl.when(c == chunks - 1)
    def _():
        for g in range(G):
            for p in range(npair):
                sp = ssm_s[g * npair + p]
                ssmo_ref[g, 2 * p] = sp[:, 0:HALF]
                ssmo_ref[g, 2 * p + 1] = sp[:, HALF:2 * HALF]
                bd = wkv_s[g * npair + p]
                wkvo_ref[g, 2 * p] = bd[0:HALF, 0:HALF]
                wkvo_ref[g, 2 * p + 1] = bd[HALF:2 * HALF, HALF:2 * HALF]
            convo_ref[g] = conv_c[g * 8:(g + 1) * 8, :]
            shifto_ref[g] = shift_c[g * 8:(g + 1) * 8, :]


def _odd_call(z, params, states, *, L, nvalid, nb, chunks, name, G):
    has_s0 = states is not None
    kern = functools.partial(_odd_kernel, L=L, nvalid=nvalid, chunks=chunks, has_s0=has_s0, G=G)
    npair = H_C // 2
    if z.ndim == 4:
        z_spec = pl.BlockSpec((G, 1, L, ODD_PAD), lambda b, c: (0, c, 0, 0))
        o_spec = pl.BlockSpec((G, 1, L, ODD_MIX), lambda b, c: (0, c, 0, 0))
        o_shape = jax.ShapeDtypeStruct(z.shape[:3] + (ODD_MIX,), F32)
    else:
        z_spec = pl.BlockSpec((G, L, ODD_PAD), lambda b, c: (b, 0, 0))
        o_spec = pl.BlockSpec((G, L, ODD_MIX), lambda b, c: (b, 0, 0))
        o_shape = jax.ShapeDtypeStruct((nb, L, ODD_MIX), F32)
    const2 = lambda b, c: (0, 0)
    per_b = lambda b, c: (b, 0, 0, 0)
    per_b3 = lambda b, c: (b, 0, 0)
    in_specs = [z_spec] + [pl.BlockSpec(p.shape, const2) for p in params]
    args = [z] + list(params)
    st_specs = [pl.BlockSpec((G, H_C, N_C, P_C), per_b), pl.BlockSpec((G, H_D, P_D, P_D), per_b),
                pl.BlockSpec((G, 8, CONV_DIM), per_b3), pl.BlockSpec((G, 8, SHIFT_W), per_b3)]
    if has_s0:
        in_specs += st_specs
        args += list(states)
    return pl.pallas_call(
        kern,
        out_shape=(o_shape,
                   jax.ShapeDtypeStruct((nb, H_C, N_C, P_C), F32), jax.ShapeDtypeStruct((nb, H_D, P_D, P_D), F32),
                   jax.ShapeDtypeStruct((nb, 8, CONV_DIM), F32), jax.ShapeDtypeStruct((nb, 8, SHIFT_W), F32)),
        grid=(nb // G, chunks),
        in_specs=in_specs,
        out_specs=tuple([o_spec] + st_specs),
        scratch_shapes=[pltpu.VMEM((G * npair, N_C, 2 * HALF), F32), pltpu.VMEM((G * npair, 2 * HALF, 2 * HALF), F32),
                        pltpu.VMEM((G * 8, CONV_DIM), F32), pltpu.VMEM((G * 8, SHIFT_W), F32)],
        compiler_params=pltpu.CompilerParams(dimension_semantics=("arbitrary", "arbitrary"),
                                             vmem_limit_bytes=VMEM_LIMIT),
        name=name,
    )(*args)


def odd_mixer_prompt(z, params, *, chunks=P_CHUNKS):
    return _odd_call(z, params, None, L=CHUNK, nvalid=CHUNK, nb=z.shape[0], chunks=chunks,
                     name="odd_mixer_prompt", G=z.shape[0])


def odd_mixer_sample(z, params, states):
    return _odd_call(z, params, states, L=S_LEN, nvalid=DEC_SEQ, nb=z.shape[0], chunks=1,
                     name="odd_mixer_sample", G=ODD_SAMPLE_GROUP)


def _odd_params(odd_w_in, conv_w, conv_b, dt_bias, a_log, d_skip, ssm_norm_g, shift_mu, rwkv_w0, rwkv_w2, rwkv_a0,
                rwkv_a2, rwkv_g2, rwkv_k_k, rwkv_k_a, rwkv_r_k, lnx_g, lnx_b):
    o_dt = DI_C + CONV_DIM
    o_rw = o_dt + H_C
    w = jnp.concatenate([odd_w_in[:, :o_dt], odd_w_in[:, o_rw:], odd_w_in[:, o_dt:o_rw],
                         jnp.zeros((D_MODEL, ODD_PAD - ODD_IN), odd_w_in.dtype)], axis=1)
    vec = jnp.stack([jnp.repeat(d_skip, P_C), ssm_norm_g, shift_mu[:DI_D], shift_mu[DI_D:2 * DI_D],
                     shift_mu[2 * DI_D:3 * DI_D], rwkv_w0, rwkv_a0, rwkv_k_k, rwkv_k_a, rwkv_r_k.reshape(-1),
                     lnx_g, lnx_b] + [jnp.zeros((DI_D,), F32)] * 4)
    cpack = jnp.concatenate([conv_w, conv_b[None], jnp.zeros((3, CONV_DIM), F32)], axis=0)
    zt = jnp.zeros((T_W,), F32)
    tpack = jnp.stack([zt.at[:T_DT].set(shift_mu[3 * DI_D:]), zt.at[T_DT:T_DT + H_C].set(dt_bias),
                       zt.at[T_DT:T_DT + H_C].set(-jnp.exp(a_log.astype(F32)))] + [zt] * 5)
    zw = jnp.zeros((T_W, DI_D), F32)
    w2p = zw.at[:R_W].set(rwkv_w2)
    a2p = zw.at[R_W:R_W + R_A].set(rwkv_a2).astype(BF16)
    g2p = zw.at[R_W + R_A:T_DT].set(rwkv_g2).astype(BF16)
    return w.astype(BF16), (vec, cpack, tpack, w2p, a2p, g2p)


def _top16_desc(cur):
    vals = []
    for _ in range(PEER_TOPK):
        m = jnp.max(cur, axis=0, keepdims=True)
        vals.append(m)
        cur = jnp.where(cur == m, -jnp.inf, cur)
    return vals


PEER_CAND = 112


def _peer_kernel(x_ref, xb_ref, wq_ref, sk_ref, u_ref, v_ref, g_ref, b_ref, o_ref, ob_ref,
                 q_s, s1_s, s2_s, e1_s, e2_s, tau_s, cand_s, w_s, h_s, p_s, pn_s, *, tm, te):
    assert te == 4 * PEER_KEYS
    s = pl.program_id(1)
    ns = pl.num_programs(1)
    nk = PEER_KEYS
    neg = -jnp.inf

    @pl.when(s == 0)
    def _route():
        q_s[...] = lax.dot_general(wq_ref[...], xb_ref[...], NT, preferred_element_type=F32)
        a_idx = lax.broadcasted_iota(jnp.int32, (PEER_TOPK, 1), 0)

        def head(h, carry):
            tops = []
            for c in range(2):
                row0 = pl.multiple_of((2 * h + c) * nk, nk)
                sc = jnp.dot(sk_ref[2 * h + c], q_s[pl.ds(row0, nk), :].astype(BF16), preferred_element_type=F32)
                if c == 0:
                    s1_s[h] = sc
                else:
                    s2_s[h] = sc
                tops.append(_top16_desc(sc))
            t1, t2 = tops
            t1_all = jnp.concatenate(t1, axis=0)
            t2_all = jnp.concatenate(t2, axis=0)
            for b in range(4):
                cand_s[b * PEER_TOPK:(b + 1) * PEER_TOPK, :] = jnp.where(a_idx < PEER_TOPK // (b + 1), t1_all + t2[b], neg)
            for a in range(3):
                ok = (a_idx >= 4) & (a_idx < PEER_TOPK // (a + 1))
                cand_s[(4 + a) * PEER_TOPK:(5 + a) * PEER_TOPK, :] = jnp.where(ok, t1[a] + t2_all, neg)
            best = _top16_desc(cand_s[...])
            mx = t1[0] + t2[0]
            z = jnp.zeros_like(mx)
            for m in best:
                z = z + jnp.exp(m - mx)
            tau_s[pl.ds(h, 1), :] = best[-1]
            e1_s[h] = jnp.exp(s1_s[h] - t1[0])
            e2_s[h] = jnp.exp(s2_s[h] - t2[0]) / z
            return carry

        lax.fori_loop(0, PEER_HEADS, head, 0)
        o_ref[...] = jnp.zeros_like(o_ref)
        p_s[...] = jnp.zeros_like(p_s)

    blk = jnp.minimum(s, ns - 2)
    nj = te // nk
    d = o_ref.shape[1]
    kp, cp = d // 4, d // 8

    def mm1(k):
        part = lax.dot_general(u_ref[:, k * kp:(k + 1) * kp], xb_ref[:, k * kp:(k + 1) * kp], NT,
                               preferred_element_type=F32)
        if k == 0:
            h_s[...] = part
        else:
            h_s[...] += part

    def wbuild(j):
        i1 = blk * nj + j
        w = jnp.zeros((nk, tm), F32)
        for h in range(PEER_HEADS):
            c = s2_s[h] + s1_s[h, pl.ds(i1, 1), :]
            w = w + jnp.where(c >= tau_s[h:h + 1, :], e2_s[h] * e1_s[h, pl.ds(i1, 1), :], 0.0)
        w_s[j * nk:(j + 1) * nk, :] = w

    def act(j):
        hj = h_s[j * nk:(j + 1) * nk, :]
        g = 0.5 * hj * (1.0 + lax.erf(hj * (2.0 ** -0.5)))
        pn_s[:, j * nk:(j + 1) * nk] = jnp.transpose(w_s[j * nk:(j + 1) * nk, :] * g).astype(BF16)

    def mm2(n):
        o_ref[:, n * cp:(n + 1) * cp] += jnp.dot(p_s[...], v_ref[:, n * cp:(n + 1) * cp], preferred_element_type=F32)

    mm1(0); wbuild(0); mm1(1); wbuild(1); mm1(2); wbuild(2); mm1(3)
    mm2(0); wbuild(3); mm2(1); act(0); mm2(2); act(1); mm2(3); act(2); mm2(4); act(3); mm2(5); mm2(6); mm2(7)
    p_s[...] = pn_s[...]

    @pl.when(s == ns - 1)
    def _fin():
        y = _ln_rows(ALPHA * x_ref[...] + o_ref[...], g_ref[...], b_ref[...])
        o_ref[...] = y
        ob_ref[...] = y.astype(BF16)


def peer_ln(x, xb, wq_t, sk, u, v, g, b, *, layer=0, tm=512, te=512):
    m, d = x.shape
    kern = functools.partial(_peer_kernel, tm=tm, te=te)
    const = dict(pipeline_mode=pl.Buffered(1))
    ne = PEER_EXPERTS // te
    route = pltpu.VMEM((PEER_HEADS, PEER_KEYS, tm), F32)
    return pl.pallas_call(
        kern,
        out_shape=(jax.ShapeDtypeStruct((m, d), F32), jax.ShapeDtypeStruct((m, d), BF16)),
        grid=(m // tm, ne + 1),
        in_specs=[pl.BlockSpec((tm, d), lambda i, s: (i, 0), **const),
                  pl.BlockSpec((tm, d), lambda i, s: (i, 0), **const),
                  pl.BlockSpec(wq_t.shape, lambda i, s: (0, 0), **const),
                  pl.BlockSpec(sk.shape, lambda i, s: (0, 0, 0), **const),
                  pl.BlockSpec((None, te, d), lambda i, s: (layer, jnp.minimum(s, ne - 1), 0)),
                  pl.BlockSpec((None, te, d), lambda i, s: (layer, jnp.maximum(s - 1, 0), 0)),
                  pl.BlockSpec((1, d), lambda i, s: (0, 0), **const),
                  pl.BlockSpec((1, d), lambda i, s: (0, 0), **const)],
        out_specs=(pl.BlockSpec((tm, d), lambda i, s: (i, 0)), pl.BlockSpec((tm, d), lambda i, s: (i, 0))),
        scratch_shapes=[pltpu.VMEM((PEER_HEADS * PEER_QDIM, tm), F32), route, route, route, route,
                        pltpu.VMEM((PEER_HEADS, tm), F32), pltpu.VMEM((PEER_CAND, tm), F32),
                        pltpu.VMEM((te, tm), F32), pltpu.VMEM((te, tm), F32),
                        pltpu.VMEM((tm, te), BF16), pltpu.VMEM((tm, te), BF16)],
        compiler_params=pltpu.CompilerParams(dimension_semantics=("arbitrary", "arbitrary"),
                                             vmem_limit_bytes=PEER_VMEM_LIMIT),
        name="peer_ln",
    )(x, xb, wq_t, sk, u, v, g.reshape(1, d), b.reshape(1, d))


def _slots(a):
    return a.reshape(BATCH, SLOT_CHUNKS, CHUNK, a.shape[-1])


def _sample_rows(z4):
    zs = z4[:, P_CHUNKS:P_CHUNKS + S_CHUNKS].reshape(DEC_BATCH, DEC_SEQ, z4.shape[-1])
    return jnp.pad(zs, ((0, 0), (0, S_LEN - DEC_SEQ), (0, 0)))


def _merge_rows(buf4, sample_out):
    f = buf4.shape[-1]
    tail = jnp.concatenate([sample_out[:, :DEC_SEQ].reshape(BATCH, S_CHUNKS, CHUNK, f),
                            jnp.zeros((BATCH, 1, CHUNK, f), buf4.dtype)], axis=1)
    return lax.dynamic_update_slice(buf4, tail, (0, P_CHUNKS, 0, 0)).reshape(M_PAD, f)


def kernel(x_prompt, x_sample, state_hgrn, state_ret, state_ssm, state_conv, state_wkv, state_shift, meta_tokens, ln_g, ln_b, even_w_in, hgrn_lb_logits, hgrn_norm_g, even_w_out, odd_w_in, conv_w, conv_b, dt_bias, a_log, d_skip, ssm_norm_g, shift_mu, rwkv_w0, rwkv_w2, rwkv_a0, rwkv_a2, rwkv_g2, rwkv_k_k, rwkv_k_a, rwkv_r_k, lnx_g, lnx_b, odd_w_out, peer_w_query, peer_sub_keys, peer_u, peer_v):
    dt = x_prompt.dtype
    lead = jnp.concatenate([jnp.zeros((P_LEAD, D_MODEL), dt), meta_tokens.astype(dt)], axis=0)
    x = jnp.concatenate([jnp.broadcast_to(lead[None, None], (BATCH, 1, CHUNK, D_MODEL)),
                         x_prompt.reshape(BATCH, P_CHUNKS - 1, CHUNK, D_MODEL),
                         x_sample.reshape(BATCH, S_CHUNKS, CHUNK, D_MODEL),
                         jnp.zeros((BATCH, 1, CHUNK, D_MODEL), dt)], axis=1).reshape(M_PAD, D_MODEL)
    xb = x.astype(BF16)
    peer_ub, peer_vb = peer_u.astype(BF16), peer_v.astype(BF16)

    cos_p, sin_p = _rope_tables(jnp.arange(P_ROWS) - P_LEAD)
    cos_s, sin_s = _rope_tables(PAST_LEN + jnp.arange(S_LEN))
    lb_table = jnp.cumsum(jax.nn.softmax(hgrn_lb_logits.astype(F32), axis=0), axis=0)

    z = matmul(xb, even_w_in[0].astype(BF16), tm=TOK_TILE, tn=1024, name="even_in")
    lb = lb_table[0].reshape(1, -1)
    ng = hgrn_norm_g[0].reshape(1, -1)
    z = _slots(z)
    mix_p, hgrn_p, ret_p = even_mixer_prompt(z, lb, ng, cos_p, sin_p)
    mix_s, hgrn_s, ret_s = even_mixer_sample(_sample_rows(z), lb, ng, cos_s, sin_s, state_hgrn[0], state_ret[0])
    mix = _merge_rows(mix_p, mix_s)
    x, xb = proj_ln(x, mix, even_w_out[0].astype(BF16), ln_g[0, 0], ln_b[0, 0], tm=TOK_TILE, name="even_out")
    x, xb = peer_ln(x, xb, peer_w_query[0].T.astype(BF16),
                    peer_sub_keys[0].reshape(2 * PEER_HEADS, PEER_KEYS, PEER_QDIM // 2).astype(BF16),
                    peer_ub, peer_vb, ln_g[0, 1], ln_b[0, 1], layer=0)

    w_in1, params = _odd_params(odd_w_in[0], conv_w[0], conv_b[0], dt_bias[0], a_log[0], d_skip[0], ssm_norm_g[0],
                                shift_mu[0], rwkv_w0[0], rwkv_w2[0], rwkv_a0[0], rwkv_a2[0], rwkv_g2[0], rwkv_k_k[0],
                                rwkv_k_a[0], rwkv_r_k[0], lnx_g[0], lnx_b[0])
    z = _slots(matmul(xb, w_in1, tm=TOK_TILE, tn=1024, name="odd_in"))
    mix_p, ssm_p, wkv_p, conv_p, shift_p = odd_mixer_prompt(z, params)
    conv8 = jnp.pad(state_conv[0], ((0, 0), (8 - (CONV_W - 1), 0), (0, 0)))
    shift8 = jnp.pad(state_shift[0][:, None, :], ((0, 0), (7, 0), (0, SHIFT_W - SHIFT_DIM)))
    mix_s, ssm_s, wkv_s, conv_s, shift_s = odd_mixer_sample(_sample_rows(z), params,
                                                            (state_ssm[0], state_wkv[0], conv8, shift8))
    mix = _merge_rows(mix_p, mix_s)
    x, xb = proj_ln(x, mix, odd_w_out[0].astype(BF16), ln_g[1, 0], ln_b[1, 0], tm=TOK_TILE, name="odd_out")
    x, xb = peer_ln(x, xb, peer_w_query[1].T.astype(BF16),
                    peer_sub_keys[1].reshape(2 * PEER_HEADS, PEER_KEYS, PEER_QDIM // 2).astype(BF16),
                    peer_ub, peer_vb, ln_g[1, 1], ln_b[1, 1], layer=1)

    x4 = _slots(x)
    y_prompt = x4[:, 1:P_CHUNKS].reshape(BATCH, SEQ, D_MODEL)
    y_sample = x4[:, P_CHUNKS:P_CHUNKS + S_CHUNKS].reshape(DEC_BATCH, DEC_SEQ, D_MODEL)
    nc = CONV_W - 1
    return (y_prompt, y_sample, hgrn_p[None], hgrn_s[None], ret_p[None], ret_s[None], ssm_p[None], ssm_s[None],
            conv_p[None, :, 8 - nc:], conv_s[None, :, DEC_SEQ - nc:DEC_SEQ], wkv_p[None], wkv_s[None],
            shift_p[None, :, 7, :SHIFT_DIM], shift_s[None, :, DEC_SEQ - 1, :SHIFT_DIM])
```

```python
import functools
import math

import jax
import jax.numpy as jnp
from jax import lax
from jax.experimental import pallas as pl
from jax.experimental.pallas import tpu as pltpu

D_MODEL = 2048
BATCH = 4
SEQ = 2048
DEPTH = 2
DEC_BATCH = 128
DEC_SEQ = 4
PAST_LEN = 16384
N_META = 16
CHUNK = 64

H_A, DK_A, DV_A = 8, 128, 128
H_B, DK_B, DV_B = 4, 128, 256
EVEN_IN = 4 * H_A * DK_A + 2 * H_B * DK_B + 2 * H_B * DV_B
EVEN_MIX = H_A * DV_A + H_B * DV_B

H_C, P_C, N_C, G_C, CONV_W = 16, 64, 128, 2, 4
DI_C = H_C * P_C
CONV_DIM = DI_C + 2 * G_C * N_C
H_D, P_D = 16, 64
DI_D = H_D * P_D
R_W, R_A, R_G = 64, 64, 160
SHIFT_DIM = 3 * DI_D + R_W + R_A + R_G
ODD_IN = DI_C + CONV_DIM + H_C + SHIFT_DIM
ODD_MIX = DI_C + DI_D

PEER_KEYS = 128
PEER_EXPERTS = PEER_KEYS * PEER_KEYS
PEER_HEADS = 8
PEER_TOPK = 16
PEER_QDIM = 256

ALPHA = (2.0 * DEPTH) ** 0.25
LN_EPS = 1e-5
RMS_EPS = 1e-6
RWKV_GN_EPS = 64e-5
ROPE_BASE = 10000.0

F32 = jnp.float32
BF16 = jnp.bfloat16

P_LEAD = CHUNK - N_META
P_ROWS = CHUNK + SEQ
P_CHUNKS = P_ROWS // CHUNK
S_CHUNKS = DEC_BATCH * DEC_SEQ // (BATCH * CHUNK)
SLOT_CHUNKS = P_CHUNKS + S_CHUNKS + 1
S_LEN = 8
SAMPLE_GROUP = 4
ODD_SAMPLE_GROUP = 4
TOK_TILE = 512
M_PAD = BATCH * SLOT_CHUNKS * CHUNK
assert M_PAD % 1024 == 0 and DEC_BATCH * DEC_SEQ == BATCH * S_CHUNKS * CHUNK

VMEM_LIMIT = 56 * 1024 * 1024
PEER_VMEM_LIMIT = 60 * 1024 * 1024


def _dot_bf(a, b, dims=(((1,), (0,)), ((), ()))):
    return lax.dot_general(a.astype(BF16), b.astype(BF16), dims, preferred_element_type=F32)


def _parts(x, n):
    out, rem = [], x
    for i in range(n):
        p = rem.astype(BF16)
        out.append(p)
        if i + 1 < n:
            rem = rem - p.astype(F32)
    return out


def _dot_sel(x, sel, dims=(((1,), (0,)), ((), ())), n=3, sel_first=False):
    sel = sel.astype(BF16)
    acc = None
    for p in _parts(x, n):
        d = lax.dot_general(sel, p, dims, preferred_element_type=F32) if sel_first else \
            lax.dot_general(p, sel, dims, preferred_element_type=F32)
        acc = d if acc is None else acc + d
    return acc


NT = (((1,), (1,)), ((), ()))
TN = (((0,), (0,)), ((), ()))


def _sigmoid(x):
    return 1.0 / (1.0 + jnp.exp(-x))


def _silu(x):
    return x * _sigmoid(x)


def _mm_kernel(x_ref, w_ref, o_ref):
    o_ref[...] = jnp.dot(x_ref[...].astype(BF16), w_ref[...], preferred_element_type=F32).astype(o_ref.dtype)


def matmul(x, w, *, tm, tn, out_dtype=F32, name="matmul"):
    m, k = x.shape
    n = w.shape[1]
    return pl.pallas_call(
        _mm_kernel,
        out_shape=jax.ShapeDtypeStruct((m, n), out_dtype),
        grid=(n // tn, m // tm),
        in_specs=[pl.BlockSpec((tm, k), lambda j, i: (i, 0)),
                  pl.BlockSpec((k, tn), lambda j, i: (0, j))],
        out_specs=pl.BlockSpec((tm, tn), lambda j, i: (i, j)),
        compiler_params=pltpu.CompilerParams(dimension_semantics=("arbitrary", "arbitrary"),
                                             vmem_limit_bytes=VMEM_LIMIT),
        name=name,
    )(x, w)


def _ln_rows(v, g, b):
    mu = jnp.mean(v, -1, keepdims=True)
    c = v - mu
    var = jnp.mean(c * c, -1, keepdims=True)
    return c * lax.rsqrt(var + LN_EPS) * g + b


def _proj_ln_kernel(x_ref, m_ref, w_ref, g_ref, b_ref, o_ref, ob_ref):
    acc = jnp.dot(m_ref[...].astype(BF16), w_ref[...], preferred_element_type=F32)
    y = _ln_rows(ALPHA * x_ref[...] + acc, g_ref[...], b_ref[...])
    o_ref[...] = y
    ob_ref[...] = y.astype(BF16)


def proj_ln(x, mix, w, g, b, *, tm, name="proj_ln"):
    m, d = x.shape
    k = mix.shape[1]
    return pl.pallas_call(
        _proj_ln_kernel,
        out_shape=(jax.ShapeDtypeStruct((m, d), F32), jax.ShapeDtypeStruct((m, d), BF16)),
        grid=(m // tm,),
        in_specs=[pl.BlockSpec((tm, d), lambda i: (i, 0)),
                  pl.BlockSpec((tm, k), lambda i: (i, 0)),
                  pl.BlockSpec((k, d), lambda i: (0, 0)),
                  pl.BlockSpec((1, d), lambda i: (0, 0)),
                  pl.BlockSpec((1, d), lambda i: (0, 0))],
        out_specs=(pl.BlockSpec((tm, d), lambda i: (i, 0)), pl.BlockSpec((tm, d), lambda i: (i, 0))),
        compiler_params=pltpu.CompilerParams(dimension_semantics=("arbitrary",),
                                             vmem_limit_bytes=VMEM_LIMIT),
        name=name,
    )(x, mix, w, g.reshape(1, d), b.reshape(1, d))


def _row_info(L, lead, nvalid):
    r = lax.broadcasted_iota(jnp.int32, (L, 1), 0)
    return r, (r >= lead) & (r < lead + nvalid)


def _seq_views(ref, G):
    if len(ref.shape) == 2:
        return [ref]
    if len(ref.shape) == 3:
        return [ref.at[g] for g in range(G)]
    return [ref.at[g, 0] for g in range(G)]


def _tril(L, strict=False):
    r = lax.broadcasted_iota(jnp.int32, (L, L), 0)
    c = lax.broadcasted_iota(jnp.int32, (L, L), 1)
    return (c < r) if strict else (c <= r)


def _hgrn_heads(qs, ks, vs, logfs, s0s, L):
    n = range(len(qs))
    tril = _tril(L).astype(F32)
    ones_l = jnp.ones((L, DK_A), F32)
    ones_k = jnp.ones((DK_A, DK_A), BF16)
    b = [_dot_sel(logfs[i], tril, sel_first=True) for i in n]
    b_end_col = [_dot_sel(logfs[i], ones_l, TN) for i in n]
    o = [_dot_bf(qs[i] * jnp.exp(b[i]), s0s[i]) for i in n]
    upd = [_dot_bf(ks[i] * jnp.exp(b[i][L - 1:L, :] - b[i]), vs[i], TN) for i in n]
    s_new = [jnp.exp(b_end_col[i]) * s0s[i] + upd[i] for i in n]
    sub = min(16, L)
    t_idx = lax.broadcasted_iota(jnp.int32, (sub, 1), 0)
    outs = [[] for _ in n]
    for blk in range(L // sub):
        r0 = blk * sub
        acc = [o[i][r0:r0 + sub] for i in n]
        if blk > 0:
            a = [_dot_bf(qs[i][r0:r0 + sub] * jnp.exp(b[i][r0:r0 + sub] - b[i][r0 - 1:r0, :]),
                         ks[i][0:r0] * jnp.exp(b[i][r0 - 1:r0, :] - b[i][0:r0]), NT) for i in n]
            acc = [acc[i] + _dot_bf(a[i], vs[i][0:r0]) for i in n]
        prod = [jnp.concatenate(
            [qs[i][r0:r0 + sub] * ks[i][r0 + s:r0 + s + 1, :] *
             jnp.exp(jnp.where(t_idx >= s, b[i][r0:r0 + sub] - b[i][r0 + s:r0 + s + 1, :], -jnp.inf))
             for s in range(sub)], axis=0).astype(BF16) for i in n]
        score = [jnp.dot(prod[i], ones_k, preferred_element_type=F32) for i in n]
        for i in n:
            t = acc[i]
            for s in range(sub):
                t = t + score[i][s * sub:(s + 1) * sub, :] * vs[i][r0 + s:r0 + s + 1, :]
            outs[i].append(t)
    return [jnp.concatenate(x, axis=0) if len(x) > 1 else x[0] for x in outs], s_new


def _chunk_or_zero(chunk_fn, o_idx, *refs, chunks, **kw):
    c = pl.program_id(1)
    o_ref = refs[o_idx]

    @pl.when(c < chunks)
    def _():
        chunk_fn(*refs, chunks=chunks, **kw)

    @pl.when(c >= chunks)
    def _():
        o_ref[...] = jnp.zeros_like(o_ref)


def _even_kernel(*refs, L, nvalid, chunks, has_s0, G):
    if has_s0:
        z_ref, lb_ref, ng_ref, cos_ref, sin_ref, sh0_ref, sr0_ref, o_ref, sho_ref, sro_ref, sh_s, sr_s = refs
    else:
        z_ref, lb_ref, ng_ref, cos_ref, sin_ref, o_ref, sho_ref, sro_ref, sh_s, sr_s = refs
    c = pl.program_id(1)

    @pl.when(c == 0)
    def _():
        if has_s0:
            for g in range(G):
                sh_s[g * H_A:(g + 1) * H_A] = sh0_ref[g]
                sr_s[g * H_B:(g + 1) * H_B] = sr0_ref[g]
        else:
            sh_s[...] = jnp.zeros_like(sh_s)
            sr_s[...] = jnp.zeros_like(sr_s)

    lead = jnp.where(c == 0, P_LEAD, 0) if chunks > 1 else 0
    r, valid = _row_info(L, lead, nvalid)
    cnt_col = jnp.clip(r + 1 - lead, 0, nvalid).astype(F32)
    rr = lax.broadcasted_iota(jnp.int32, (1, L), 1)
    cnt_row = jnp.clip(rr + 1 - lead, 0, nvalid).astype(F32)

    off_q, off_f, off_i, off_g = 0, H_A * DK_A, 2 * H_A * DK_A, 2 * H_A * DK_A + H_A * DV_A
    base_b = 2 * H_A * DK_A + 2 * H_A * DV_A
    off_bq, off_bk, off_bv = base_b, base_b + H_B * DK_B, base_b + 2 * H_B * DK_B
    off_bg = off_bv + H_B * DV_B
    ng = ng_ref[...]
    zv, ov = _seq_views(z_ref, G), _seq_views(o_ref, G)
    ia = [(g, h) for g in range(G) for h in range(H_A)]
    ib = [(g, h) for g in range(G) for h in range(H_B)]
    na, nb_ = range(len(ia)), range(len(ib))
    af = [zv[g][:, off_f + h * DK_A: off_f + (h + 1) * DK_A] for g, h in ia]
    lbs = [lb_ref[:, h * DK_A:(h + 1) * DK_A] for g, h in ia]
    logf = [jnp.where(valid, jnp.log(lbs[i] + (1.0 - lbs[i]) * _sigmoid(af[i])), 0.0) for i in na]
    ka = [jnp.where(valid, (1.0 - lbs[i]) * _sigmoid(-af[i]), 0.0) for i in na]
    qa = [_silu(zv[g][:, off_q + h * DK_A: off_q + (h + 1) * DK_A]) for g, h in ia]
    va = [zv[g][:, off_i + h * DV_A: off_i + (h + 1) * DV_A] for g, h in ia]
    cosv, sinv = cos_ref[...], sin_ref[...]
    bq = [zv[g][:, off_bq + h * DK_B: off_bq + (h + 1) * DK_B] for g, h in ib]
    bk = [zv[g][:, off_bk + h * DK_B: off_bk + (h + 1) * DK_B] for g, h in ib]
    vb = [zv[g][:, off_bv + h * DV_B: off_bv + (h + 1) * DV_B] for g, h in ib]
    qb = [bq[i] * cosv + pltpu.roll(bq[i], DK_B // 2, 1) * sinv for i in nb_]
    kb = [jnp.where(valid, (bk[i] * cosv + pltpu.roll(bk[i], DK_B // 2, 1) * sinv) * (DK_B ** -0.5), 0.0) for i in nb_]
    lg = [math.log(1.0 - 2.0 ** (-5.0 - h)) for g, h in ib]

    oa, sa_new = _hgrn_heads(qa, ka, va, logf, [sh_s[i] for i in na], L)
    causal = _tril(L)
    cnt_end = cnt_col[L - 1:L, :]
    scores = [_dot_bf(qb[i], kb[i], NT) * jnp.where(causal, jnp.exp((cnt_col - cnt_row) * lg[i]), 0.0) for i in nb_]
    inter = [_dot_bf(qb[i] * jnp.exp(cnt_col * lg[i]), sr_s[i]) for i in nb_]
    ob = [_dot_bf(scores[i], vb[i]) + inter[i] for i in nb_]
    updb = [_dot_bf(kb[i] * jnp.exp((cnt_end - cnt_col) * lg[i]), vb[i], TN) for i in nb_]
    for i, (g, h) in enumerate(ia):
        sh_s[i] = sa_new[i]
        o = oa[i] * lax.rsqrt(jnp.mean(oa[i] * oa[i], -1, keepdims=True) + RMS_EPS) * ng
        ov[g][:, h * DV_A:(h + 1) * DV_A] = o * _silu(zv[g][:, off_g + h * DV_A: off_g + (h + 1) * DV_A])
    for i, (g, h) in enumerate(ib):
        sr_s[i] = jnp.exp(cnt_end * lg[i]) * sr_s[i] + updb[i]
        mu = jnp.mean(ob[i], -1, keepdims=True)
        cc = ob[i] - mu
        o = cc * lax.rsqrt(jnp.mean(cc * cc, -1, keepdims=True) + LN_EPS)
        ov[g][:, H_A * DV_A + h * DV_B: H_A * DV_A + (h + 1) * DV_B] = \
            o * _silu(zv[g][:, off_bg + h * DV_B: off_bg + (h + 1) * DV_B])

    @pl.when(c == chunks - 1)
    def _():
        for g in range(G):
            sho_ref[g] = sh_s[g * H_A:(g + 1) * H_A]
            sro_ref[g] = sr_s[g * H_B:(g + 1) * H_B]


def even_mixer_prompt(z, lb, ng, cos, sin, *, chunks=P_CHUNKS):
    nb = z.shape[0]
    kern = functools.partial(_chunk_or_zero, _even_kernel, 5, L=CHUNK, nvalid=CHUNK, chunks=chunks, has_s0=False, G=nb)
    blk = lambda b, c: (0, c, 0, 0)
    whole = lambda b, c: (0, 0, 0, 0)
    rope = lambda b, c: (jnp.minimum(c, chunks - 1), 0)
    return pl.pallas_call(
        kern,
        out_shape=(jax.ShapeDtypeStruct(z.shape[:3] + (EVEN_MIX,), F32),
                   jax.ShapeDtypeStruct((nb, H_A, DK_A, DV_A), F32),
                   jax.ShapeDtypeStruct((nb, H_B, DK_B, DV_B), F32)),
        grid=(1, z.shape[1]),
        in_specs=[pl.BlockSpec((nb, 1, CHUNK, EVEN_IN), blk),
                  pl.BlockSpec((1, H_A * DK_A), lambda b, c: (0, 0)),
                  pl.BlockSpec((1, DV_A), lambda b, c: (0, 0)),
                  pl.BlockSpec((CHUNK, DK_B), rope),
                  pl.BlockSpec((CHUNK, DK_B), rope)],
        out_specs=(pl.BlockSpec((nb, 1, CHUNK, EVEN_MIX), blk),
                   pl.BlockSpec((nb, H_A, DK_A, DV_A), whole),
                   pl.BlockSpec((nb, H_B, DK_B, DV_B), whole)),
        scratch_shapes=[pltpu.VMEM((nb * H_A, DK_A, DV_A), F32), pltpu.VMEM((nb * H_B, DK_B, DV_B), F32)],
        compiler_params=pltpu.CompilerParams(dimension_semantics=("arbitrary", "arbitrary"),
                                             vmem_limit_bytes=VMEM_LIMIT),
        name="even_mixer_prompt",
    )(z, lb, ng, cos, sin)


def even_mixer_sample(z, lb, ng, cos, sin, s_hgrn, s_ret):
    nb = z.shape[0]
    G = SAMPLE_GROUP
    kern = functools.partial(_even_kernel, L=S_LEN, nvalid=DEC_SEQ, chunks=1, has_s0=True, G=G)
    return pl.pallas_call(
        kern,
        out_shape=(jax.ShapeDtypeStruct((nb, S_LEN, EVEN_MIX), F32),
                   jax.ShapeDtypeStruct((nb, H_A, DK_A, DV_A), F32),
                   jax.ShapeDtypeStruct((nb, H_B, DK_B, DV_B), F32)),
        grid=(nb // G, 1),
        in_specs=[pl.BlockSpec((G, S_LEN, EVEN_IN), lambda b, c: (b, 0, 0)),
                  pl.BlockSpec((1, H_A * DK_A), lambda b, c: (0, 0)),
                  pl.BlockSpec((1, DV_A), lambda b, c: (0, 0)),
                  pl.BlockSpec((S_LEN, DK_B), lambda b, c: (0, 0)),
                  pl.BlockSpec((S_LEN, DK_B), lambda b, c: (0, 0)),
                  pl.BlockSpec((G, H_A, DK_A, DV_A), lambda b, c: (b, 0, 0, 0)),
                  pl.BlockSpec((G, H_B, DK_B, DV_B), lambda b, c: (b, 0, 0, 0))],
        out_specs=(pl.BlockSpec((G, S_LEN, EVEN_MIX), lambda b, c: (b, 0, 0)),
                   pl.BlockSpec((G, H_A, DK_A, DV_A), lambda b, c: (b, 0, 0, 0)),
                   pl.BlockSpec((G, H_B, DK_B, DV_B), lambda b, c: (b, 0, 0, 0))),
        scratch_shapes=[pltpu.VMEM((G * H_A, DK_A, DV_A), F32), pltpu.VMEM((G * H_B, DK_B, DV_B), F32)],
        compiler_params=pltpu.CompilerParams(dimension_semantics=("arbitrary", "arbitrary"),
                                             vmem_limit_bytes=VMEM_LIMIT),
        name="even_mixer_sample",
    )(z, lb, ng, cos, sin, s_hgrn, s_ret)


def _rope_tables(pos):
    half = DK_B // 2
    inv = ROPE_BASE ** (-jnp.arange(half, dtype=F32) / half)
    ang = pos.astype(F32)[:, None] * inv
    cos, sin = jnp.cos(ang), jnp.sin(ang)
    return jnp.concatenate([cos, cos], -1), jnp.concatenate([-sin, sin], -1)


ODD_PAD = 6144
O_Z, O_X, O_R, O_K, O_V, O_T = 0, DI_C, DI_C + CONV_DIM, DI_C + CONV_DIM + DI_D, DI_C + CONV_DIM + 2 * DI_D, \
    DI_C + CONV_DIM + 3 * DI_D
T_W = 384
T_DT = R_W + R_A + R_G
SHIFT_W = 3 * DI_D + T_W
HALF = 64


def _softplus(x):
    return jnp.maximum(x, 0.0) + jnp.log1p(jnp.exp(-jnp.abs(x)))


def _lane_lt(n, width):
    return lax.broadcasted_iota(jnp.int32, (1, width), 1) < n


def _pair_ones():
    r = lax.broadcasted_iota(jnp.int32, (2 * HALF, 2 * HALF), 0) < HALF
    c = lax.broadcasted_iota(jnp.int32, (2 * HALF, 2 * HALF), 1) < HALF
    return r == c


def _stack_pair(x, m_a):
    return jnp.concatenate([jnp.where(m_a, x, 0.0), jnp.where(m_a, 0.0, x)], axis=0)


def _rwkv_pairs(rs, ks_, vs_, als, bes, logws, Gs, bds, L):
    n = range(len(rs))
    m_a = _lane_lt(HALF, 2 * HALF)
    m_l = _lane_lt(L, 2 * L)
    ri = lax.broadcasted_iota(jnp.int32, (2 * L, 2 * L), 0)
    ci = lax.broadcasted_iota(jnp.int32, (2 * L, 2 * L), 1)
    cm = jnp.where(ci >= L, ci - L, ci)
    keep = cm < jnp.where(ri >= L, ri - L + 1, ri)
    eye = (ri == ci).astype(F32)
    blk4 = (ri // 4) == (ci // 4)
    pair_blk = _pair_ones()

    e_inv = [jnp.exp(-Gs[i]) for i in n]
    lhs = [jnp.concatenate([als[i] * jnp.exp(Gs[i] - logws[i]), rs[i] * jnp.exp(Gs[i])], axis=0).astype(BF16) for i in n]
    ks = [_stack_pair(ks_[i] * e_inv[i], m_a).astype(BF16) for i in n]
    bs = [_stack_pair(bes[i] * e_inv[i], m_a).astype(BF16) for i in n]
    vs = [_stack_pair(vs_[i], m_a).astype(BF16) for i in n]
    a_k = [jnp.where(keep, _dot_bf(lhs[i], ks[i], NT), 0.0) for i in n]
    a_b = [jnp.where(keep, _dot_bf(lhs[i], bs[i], NT), 0.0) for i in n]
    sb = [_dot_bf(lhs[i], bds[i], NT) for i in n]
    base = [sb[i] + _dot_bf(a_k[i], vs[i]) for i in n]
    n_bd = [jnp.concatenate([jnp.where(m_l, a_b[i][0:L], 0.0), jnp.where(m_l, 0.0, a_b[i][0:L])], axis=0) for i in n]
    n4 = [jnp.where(blk4, n_bd[i], 0.0) for i in n]
    n4sq = [_dot_bf(n4[i], n4[i]) for i in n]
    t_inv = [eye - n4[i] for i in n]
    t_inv = [t_inv[i] + _dot_bf(t_inv[i], n4sq[i]) for i in n]
    blk = 4
    while blk < L:
        off = ((ri // (2 * blk)) == (ci // (2 * blk))) & ((ri % (2 * blk)) >= blk) & ((ci % (2 * blk)) < blk)
        ct = [_dot_bf(jnp.where(off, n_bd[i], 0.0), t_inv[i]) for i in n]
        t_inv = [t_inv[i] - _dot_bf(t_inv[i], ct[i]) for i in n]
        blk *= 2
    u = [_dot_bf(t_inv[i][0:L, :] + t_inv[i][L:2 * L, :], _stack_pair(base[i][0:L], m_a)) for i in n]
    y = [base[i][L:2 * L] - _dot_bf(a_b[i][L:2 * L], _stack_pair(u[i], m_a)) for i in n]
    gl = [Gs[i][L - 1:L, :] for i in n]
    dec = [jnp.exp(gl[i] - Gs[i]) for i in n]
    upd = [_dot_bf(jnp.concatenate([vs_[i], u[i]], axis=0),
                   jnp.concatenate([ks_[i] * dec[i], -(bes[i] * dec[i])], axis=0), TN) for i in n]
    bd_new = [bds[i] * jnp.exp(gl[i]) + jnp.where(pair_blk, upd[i], 0.0) for i in n]
    return y, bd_new


def _ssd_pair(xdt, cg, bg, cb, bcs, brows, sp, L):
    m_a = _lane_lt(HALF, 2 * HALF)
    causal = _tril(L)
    out = jnp.zeros((L, 2 * HALF), F32)
    ends = [bc[L - 1:L, :] for bc in bcs]
    sp_new = sp * jnp.where(m_a, jnp.exp(ends[0]), jnp.exp(ends[1]))
    for x in range(2):
        keep = m_a if x == 0 else jnp.logical_not(m_a)
        seg = jnp.where(causal, jnp.exp(bcs[x] - brows[x]), 0.0)
        xm = jnp.where(keep, xdt, 0.0)
        out = out + _dot_bf(cb * seg, xm) + _dot_bf(cg * jnp.exp(bcs[x]), jnp.where(keep, sp, 0.0))
        sp_new = sp_new + _dot_bf(bg * jnp.exp(ends[x] - bcs[x]), xm, TN)
    return out, sp_new


def _odd_kernel(*refs, L, nvalid, chunks, has_s0, G):
    if has_s0:
        (z_ref, vec_ref, cp_ref, tp_ref, w2_ref, a2_ref, g2_ref, ssm0_ref, wkv0_ref, conv0_ref, shift0_ref,
         o_ref, ssmo_ref, wkvo_ref, convo_ref, shifto_ref, ssm_s, wkv_s, conv_c, shift_c) = refs
    else:
        (z_ref, vec_ref, cp_ref, tp_ref, w2_ref, a2_ref, g2_ref,
         o_ref, ssmo_ref, wkvo_ref, convo_ref, shifto_ref, ssm_s, wkv_s, conv_c, shift_c) = refs
    c = pl.program_id(1)
    npair = H_C // 2

    @pl.when(c == 0)
    def _():
        if has_s0:
            zpad = jnp.zeros((HALF, HALF), F32)
            for g in range(G):
                for p in range(npair):
                    ssm_s[g * npair + p] = jnp.concatenate([ssm0_ref[g, 2 * p], ssm0_ref[g, 2 * p + 1]], axis=1)
                    wkv_s[g * npair + p] = jnp.concatenate(
                        [jnp.concatenate([wkv0_ref[g, 2 * p], zpad], axis=1),
                         jnp.concatenate([zpad, wkv0_ref[g, 2 * p + 1]], axis=1)], axis=0)
                conv_c[g * 8:(g + 1) * 8, :] = conv0_ref[g]
                shift_c[g * 8:(g + 1) * 8, :] = shift0_ref[g]
        else:
            ssm_s[...] = jnp.zeros_like(ssm_s)
            wkv_s[...] = jnp.zeros_like(wkv_s)
            conv_c[...] = jnp.zeros_like(conv_c)
            shift_c[...] = jnp.zeros_like(shift_c)

    lead = jnp.where(c == 0, P_LEAD, 0) if chunks > 1 else 0
    r_idx, valid = _row_info(L, lead, nvalid)
    m_a = _lane_lt(HALF, 2 * HALF)
    tril = _tril(L).astype(F32)
    triu = (lax.broadcasted_iota(jnp.int32, (L, L), 0) <= lax.broadcasted_iota(jnp.int32, (L, L), 1)).astype(F32)
    ones_blk = _pair_ones().astype(F32)
    r8 = lax.broadcasted_iota(jnp.int32, (8, 1), 0)
    zv, ov = _seq_views(z_ref, G), _seq_views(o_ref, G)
    pairs = range(npair)
    sls = [slice(p * 2 * HALF, (p + 1) * 2 * HALF) for p in pairs]
    w2_hi, w2_lo = _parts(w2_ref[...], 2)

    def shift_mix(p_raw, carry_row, mu):
        pv = jnp.where(valid, p_raw, 0.0)
        prev = jnp.where(r_idx == 0, carry_row, pltpu.roll(pv, 1, 0))
        return pv, pv + (prev - pv) * mu

    rw = []
    for g in range(G):
        zg, og = zv[g], ov[g]
        cs = slice(g * 8, (g + 1) * 8)
        xbc = jnp.where(valid, zg[:, O_X:O_X + CONV_DIM], 0.0)
        c8 = conv_c[cs, :]
        conv = cp_ref[CONV_W:CONV_W + 1, :] + xbc * cp_ref[CONV_W - 1:CONV_W, :]
        for j in range(1, CONV_W):
            rolled = pltpu.roll(xbc, j, 0)
            head = jnp.where(r8 < j, pltpu.roll(c8, j, 0), rolled[0:8])
            sh = jnp.concatenate([head, rolled[8:]], axis=0) if L > 8 else head
            conv = conv + sh * cp_ref[CONV_W - 1 - j:CONV_W - j, :]
        conv_c[cs, :] = xbc[L - 8:L]
        act = _silu(conv)
        xc, bm, cm = act[:, :DI_C], act[:, DI_C:DI_C + G_C * N_C], act[:, DI_C + G_C * N_C:]

        t_raw = zg[:, O_T:O_T + T_W]
        dt = jnp.where(valid, _softplus(t_raw + tp_ref[1:2, :])[:, T_DT:T_DT + H_C], 0.0)
        logf = dt * tp_ref[2:3, T_DT:T_DT + H_C]
        bc_all = _dot_sel(logf, tril, sel_first=True)
        brow_all = _dot_sel(logf, triu, TN)
        per_g = npair // G_C
        for grp in range(G_C):
            cg = cm[:, grp * N_C:(grp + 1) * N_C]
            bg = bm[:, grp * N_C:(grp + 1) * N_C]
            cb = _dot_bf(cg, bg, NT)
            ys, ss = [], jnp.zeros((L, 1), F32)
            for pp in range(per_g):
                p = grp * per_g + pp
                sl = sls[p]
                h_a, h_b = 2 * p, 2 * p + 1
                xcp = xc[:, sl]
                xdt = xcp * jnp.where(m_a, dt[:, h_a:h_a + 1], dt[:, h_b:h_b + 1])
                o, sp_new = _ssd_pair(xdt, cg, bg, cb, [bc_all[:, h_a:h_a + 1], bc_all[:, h_b:h_b + 1]],
                                      [brow_all[h_a:h_a + 1, :], brow_all[h_b:h_b + 1, :]], ssm_s[g * npair + p], L)
                ssm_s[g * npair + p] = sp_new
                y = (o + xcp * vec_ref[0:1, sl]) * _silu(zg[:, O_Z + p * 2 * HALF:O_Z + (p + 1) * 2 * HALF])
                ys.append(y)
                ss = ss + jnp.sum(y * y, axis=-1, keepdims=True)
            scale = lax.rsqrt(ss / (DI_C // G_C) + RMS_EPS)
            for pp in range(per_g):
                sl = sls[grp * per_g + pp]
                og[:, sl] = ys[pp] * scale * vec_ref[1:2, sl]

        r_raw, r = shift_mix(zg[:, O_R:O_R + DI_D], shift_c[g * 8 + 7:g * 8 + 8, 0:DI_D], vec_ref[2:3, :])
        k_raw, k = shift_mix(zg[:, O_K:O_K + DI_D], shift_c[g * 8 + 7:g * 8 + 8, DI_D:2 * DI_D], vec_ref[3:4, :])
        v_raw, v = shift_mix(zg[:, O_V:O_V + DI_D], shift_c[g * 8 + 7:g * 8 + 8, 2 * DI_D:3 * DI_D], vec_ref[4:5, :])
        t_rawm, tm = shift_mix(t_raw, shift_c[g * 8 + 7:g * 8 + 8, 3 * DI_D:SHIFT_W], tp_ref[0:1, :])
        shift_c[cs, 0:DI_D] = r_raw[L - 8:L]
        shift_c[cs, DI_D:2 * DI_D] = k_raw[L - 8:L]
        shift_c[cs, 2 * DI_D:3 * DI_D] = v_raw[L - 8:L]
        shift_c[cs, 3 * DI_D:SHIFT_W] = t_rawm[L - 8:L]

        th_hi, th_lo = _parts(jnp.tanh(tm), 2)
        w_pre = vec_ref[5:6, :] + (jnp.dot(th_hi, w2_hi, preferred_element_type=F32) +
                                   jnp.dot(th_lo, w2_hi, preferred_element_type=F32) +
                                   jnp.dot(th_hi, w2_lo, preferred_element_type=F32))
        logw = jnp.where(valid, -jnp.exp(-_softplus(-w_pre) - 0.5), 0.0)
        g_cum = _dot_sel(logw, tril, sel_first=True)
        a = _sigmoid(vec_ref[6:7, :] + _dot_bf(tm, a2_ref[...]))
        gate = _dot_bf(_sigmoid(tm), g2_ref[...])
        kkr = k * vec_ref[7:8, :]
        k2 = k * (1.0 + (a - 1.0) * vec_ref[8:9, :])
        rw.append(dict(r=r, v=v, a=a, gate=gate, kkr=kkr, k2=k2, rk=r * k2 * vec_ref[9:10, :], logw=logw, g_cum=g_cum))

    gp = [(g, p) for g in range(G) for p in pairs]
    n = range(len(gp))
    nrm = [jnp.sqrt(_dot_sel(rw[g]["kkr"][:, sls[p]] * rw[g]["kkr"][:, sls[p]], ones_blk, n=2)) for g, p in gp]
    al = [jnp.where(valid, rw[g]["kkr"][:, sls[p]] / jnp.maximum(nrm[i], 1e-12), 0.0) for i, (g, p) in enumerate(gp)]
    be = [al[i] * rw[g]["a"][:, sls[p]] for i, (g, p) in enumerate(gp)]
    ys, bd_new = _rwkv_pairs([rw[g]["r"][:, sls[p]] for g, p in gp],
                             [jnp.where(valid, rw[g]["k2"][:, sls[p]], 0.0) for g, p in gp],
                             [rw[g]["v"][:, sls[p]] for g, p in gp], al, be,
                             [rw[g]["logw"][:, sls[p]] for g, p in gp], [rw[g]["g_cum"][:, sls[p]] for g, p in gp],
                             [wkv_s[i] for i in n], L)
    for i in n:
        wkv_s[i] = bd_new[i]
    mu = [_dot_sel(ys[i], ones_blk, n=2) * (1.0 / P_D) for i in n]
    yc = [ys[i] - mu[i] for i in n]
    var = [_dot_sel(yc[i] * yc[i], ones_blk, n=2) * (1.0 / P_D) for i in n]
    bonus = [_dot_sel(rw[g]["rk"][:, sls[p]], ones_blk, n=2) for g, p in gp]
    for i, (g, p) in enumerate(gp):
        sl = sls[p]
        yn = yc[i] * lax.rsqrt(var[i] + RWKV_GN_EPS) * vec_ref[10:11, sl] + vec_ref[11:12, sl]
        ov[g][:, DI_C + p * 2 * HALF:DI_C + (p + 1) * 2 * HALF] = (yn + bonus[i] * rw[g]["v"][:, sl]) * rw[g]["gate"][:, sl]

    @pl.when(c == chunks - 1)
    def _():
        for g in range(G):
            for p in range(npair):
                sp = ssm_s[g * npair + p]
                ssmo_ref[g, 2 * p] = sp[:, 0:HALF]
                ssmo_ref[g, 2 * p + 1] = sp[:, HALF:2 * HALF]
                bd = wkv_s[g * npair + p]
                wkvo_ref[g, 2 * p] = bd[0:HALF, 0:HALF]
                wkvo_ref[g, 2 * p + 1] = bd[HALF:2 * HALF, HALF:2 * HALF]
            convo_ref[g] = conv_c[g * 8:(g + 1) * 8, :]
            shifto_ref[g] = shift_c[g * 8:(g + 1) * 8, :]


def _odd_call(z, params, states, *, L, nvalid, nb, chunks, name, G):
    has_s0 = states is not None
    kern = functools.partial(_chunk_or_zero, _odd_kernel, 11 if has_s0 else 7, L=L, nvalid=nvalid, chunks=chunks,
                             has_s0=has_s0, G=G)
    npair = H_C // 2
    steps = z.shape[1] if z.ndim == 4 else chunks
    if z.ndim == 4:
        z_spec = pl.BlockSpec((G, 1, L, ODD_PAD), lambda b, c: (0, c, 0, 0))
        o_spec = pl.BlockSpec((G, 1, L, ODD_MIX), lambda b, c: (0, c, 0, 0))
        o_shape = jax.ShapeDtypeStruct(z.shape[:3] + (ODD_MIX,), F32)
    else:
        z_spec = pl.BlockSpec((G, L, ODD_PAD), lambda b, c: (b, 0, 0))
        o_spec = pl.BlockSpec((G, L, ODD_MIX), lambda b, c: (b, 0, 0))
        o_shape = jax.ShapeDtypeStruct((nb, L, ODD_MIX), F32)
    const2 = lambda b, c: (0, 0)
    per_b = lambda b, c: (b, 0, 0, 0)
    per_b3 = lambda b, c: (b, 0, 0)
    in_specs = [z_spec] + [pl.BlockSpec(p.shape, const2) for p in params]
    args = [z] + list(params)
    st_specs = [pl.BlockSpec((G, H_C, N_C, P_C), per_b), pl.BlockSpec((G, H_D, P_D, P_D), per_b),
                pl.BlockSpec((G, 8, CONV_DIM), per_b3), pl.BlockSpec((G, 8, SHIFT_W), per_b3)]
    if has_s0:
        in_specs += st_specs
        args += list(states)
    return pl.pallas_call(
        kern,
        out_shape=(o_shape,
                   jax.ShapeDtypeStruct((nb, H_C, N_C, P_C), F32), jax.ShapeDtypeStruct((nb, H_D, P_D, P_D), F32),
                   jax.ShapeDtypeStruct((nb, 8, CONV_DIM), F32), jax.ShapeDtypeStruct((nb, 8, SHIFT_W), F32)),
        grid=(nb // G, steps),
        in_specs=in_specs,
        out_specs=tuple([o_spec] + st_specs),
        scratch_shapes=[pltpu.VMEM((G * npair, N_C, 2 * HALF), F32), pltpu.VMEM((G * npair, 2 * HALF, 2 * HALF), F32),
                        pltpu.VMEM((G * 8, CONV_DIM), F32), pltpu.VMEM((G * 8, SHIFT_W), F32)],
        compiler_params=pltpu.CompilerParams(dimension_semantics=("arbitrary", "arbitrary"),
                                             vmem_limit_bytes=VMEM_LIMIT),
        name=name,
    )(*args)


def odd_mixer_prompt(z, params, *, chunks=P_CHUNKS):
    return _odd_call(z, params, None, L=CHUNK, nvalid=CHUNK, nb=z.shape[0], chunks=chunks,
                     name="odd_mixer_prompt", G=z.shape[0])


def odd_mixer_sample(z, params, states):
    return _odd_call(z, params, states, L=S_LEN, nvalid=DEC_SEQ, nb=z.shape[0], chunks=1,
                     name="odd_mixer_sample", G=ODD_SAMPLE_GROUP)


def _odd_params(odd_w_in, conv_w, conv_b, dt_bias, a_log, d_skip, ssm_norm_g, shift_mu, rwkv_w0, rwkv_w2, rwkv_a0,
                rwkv_a2, rwkv_g2, rwkv_k_k, rwkv_k_a, rwkv_r_k, lnx_g, lnx_b):
    o_dt = DI_C + CONV_DIM
    o_rw = o_dt + H_C
    w = jnp.concatenate([odd_w_in[:, :o_dt], odd_w_in[:, o_rw:], odd_w_in[:, o_dt:o_rw],
                         jnp.zeros((D_MODEL, ODD_PAD - ODD_IN), odd_w_in.dtype)], axis=1)
    vec = jnp.stack([jnp.repeat(d_skip, P_C), ssm_norm_g, shift_mu[:DI_D], shift_mu[DI_D:2 * DI_D],
                     shift_mu[2 * DI_D:3 * DI_D], rwkv_w0, rwkv_a0, rwkv_k_k, rwkv_k_a, rwkv_r_k.reshape(-1),
                     lnx_g, lnx_b] + [jnp.zeros((DI_D,), F32)] * 4)
    cpack = jnp.concatenate([conv_w, conv_b[None], jnp.zeros((3, CONV_DIM), F32)], axis=0)
    zt = jnp.zeros((T_W,), F32)
    tpack = jnp.stack([zt.at[:T_DT].set(shift_mu[3 * DI_D:]), zt.at[T_DT:T_DT + H_C].set(dt_bias),
                       zt.at[T_DT:T_DT + H_C].set(-jnp.exp(a_log.astype(F32)))] + [zt] * 5)
    zw = jnp.zeros((T_W, DI_D), F32)
    w2p = zw.at[:R_W].set(rwkv_w2)
    a2p = zw.at[R_W:R_W + R_A].set(rwkv_a2).astype(BF16)
    g2p = zw.at[R_W + R_A:T_DT].set(rwkv_g2).astype(BF16)
    return w.astype(BF16), (vec, cpack, tpack, w2p, a2p, g2p)


def _top16_desc(cur):
    vals = []
    for _ in range(PEER_TOPK):
        m = jnp.max(cur, axis=0, keepdims=True)
        vals.append(m)
        cur = jnp.where(cur == m, -jnp.inf, cur)
    return vals


PEER_CAND = 112


def _peer_kernel(x_ref, xb_ref, wq_ref, sk_ref, u_ref, v_ref, g_ref, b_ref, o_ref, ob_ref,
                 q_s, s1_s, s2_s, e1_s, e2_s, tau_s, cand_s, w_s, h_s, p_s, pn_s, *, tm, te):
    assert te == 4 * PEER_KEYS
    s = pl.program_id(1)
    ns = pl.num_programs(1)
    nk = PEER_KEYS
    neg = -jnp.inf

    @pl.when(s == 0)
    def _route():
        q_s[...] = lax.dot_general(wq_ref[...], xb_ref[...], NT, preferred_element_type=F32)
        a_idx = lax.broadcasted_iota(jnp.int32, (PEER_TOPK, 1), 0)

        def head(h, carry):
            tops = []
            for c in range(2):
                row0 = pl.multiple_of((2 * h + c) * nk, nk)
                sc = jnp.dot(sk_ref[2 * h + c], q_s[pl.ds(row0, nk), :].astype(BF16), preferred_element_type=F32)
                if c == 0:
                    s1_s[h] = sc
                else:
                    s2_s[h] = sc
                tops.append(_top16_desc(sc))
            t1, t2 = tops
            t1_all = jnp.concatenate(t1, axis=0)
            t2_all = jnp.concatenate(t2, axis=0)
            for b in range(4):
                cand_s[b * PEER_TOPK:(b + 1) * PEER_TOPK, :] = jnp.where(a_idx < PEER_TOPK // (b + 1), t1_all + t2[b], neg)
            for a in range(3):
                ok = (a_idx >= 4) & (a_idx < PEER_TOPK // (a + 1))
                cand_s[(4 + a) * PEER_TOPK:(5 + a) * PEER_TOPK, :] = jnp.where(ok, t1[a] + t2_all, neg)
            best = _top16_desc(cand_s[...])
            mx = t1[0] + t2[0]
            z = jnp.zeros_like(mx)
            for m in best:
                z = z + jnp.exp(m - mx)
            tau_s[pl.ds(h, 1), :] = best[-1]
            e1_s[h] = jnp.exp(s1_s[h] - t1[0])
            e2_s[h] = jnp.exp(s2_s[h] - t2[0]) / z
            return carry

        lax.fori_loop(0, PEER_HEADS, head, 0)
        o_ref[...] = jnp.zeros_like(o_ref)
        p_s[...] = jnp.zeros_like(p_s)

    blk = jnp.minimum(s, ns - 2)
    nj = te // nk
    d = o_ref.shape[1]
    kp, cp = d // 4, d // 8

    def mm1(k):
        part = lax.dot_general(u_ref[:, k * kp:(k + 1) * kp], xb_ref[:, k * kp:(k + 1) * kp], NT,
                               preferred_element_type=F32)
        if k == 0:
            h_s[...] = part
        else:
            h_s[...] += part

    def wbuild(j):
        i1 = blk * nj + j
        w = jnp.zeros((nk, tm), F32)
        for h in range(PEER_HEADS):
            c = s2_s[h] + s1_s[h, pl.ds(i1, 1), :]
            w = w + jnp.where(c >= tau_s[h:h + 1, :], e2_s[h] * e1_s[h, pl.ds(i1, 1), :], 0.0)
        w_s[j * nk:(j + 1) * nk, :] = w

    def act(j):
        hj = h_s[j * nk:(j + 1) * nk, :]
        g = 0.5 * hj * (1.0 + lax.erf(hj * (2.0 ** -0.5)))
        pn_s[:, j * nk:(j + 1) * nk] = jnp.transpose(w_s[j * nk:(j + 1) * nk, :] * g).astype(BF16)

    def mm2(n):
        o_ref[:, n * cp:(n + 1) * cp] += jnp.dot(p_s[...], v_ref[:, n * cp:(n + 1) * cp], preferred_element_type=F32)

    mm1(0); wbuild(0); mm1(1); wbuild(1); mm1(2); wbuild(2); mm1(3)
    mm2(0); wbuild(3); mm2(1); act(0); mm2(2); act(1); mm2(3); act(2); mm2(4); act(3); mm2(5); mm2(6); mm2(7)
    p_s[...] = pn_s[...]

    @pl.when(s == ns - 1)
    def _fin():
        y = _ln_rows(ALPHA * x_ref[...] + o_ref[...], g_ref[...], b_ref[...])
        o_ref[...] = y
        ob_ref[...] = y.astype(BF16)


def peer_ln(x, xb, wq_t, sk, u, v, g, b, *, layer=0, tm=512, te=512):
    m, d = x.shape
    kern = functools.partial(_peer_kernel, tm=tm, te=te)
    const = dict(pipeline_mode=pl.Buffered(1))
    ne = PEER_EXPERTS // te
    route = pltpu.VMEM((PEER_HEADS, PEER_KEYS, tm), F32)
    return pl.pallas_call(
        kern,
        out_shape=(jax.ShapeDtypeStruct((m, d), F32), jax.ShapeDtypeStruct((m, d), BF16)),
        grid=(m // tm, ne + 1),
        in_specs=[pl.BlockSpec((tm, d), lambda i, s: (i, 0), **const),
                  pl.BlockSpec((tm, d), lambda i, s: (i, 0), **const),
                  pl.BlockSpec(wq_t.shape, lambda i, s: (0, 0), **const),
                  pl.BlockSpec(sk.shape, lambda i, s: (0, 0, 0), **const),
                  pl.BlockSpec((None, te, d), lambda i, s: (layer, jnp.minimum(s, ne - 1), 0)),
                  pl.BlockSpec((None, te, d), lambda i, s: (layer, jnp.maximum(s - 1, 0), 0)),
                  pl.BlockSpec((1, d), lambda i, s: (0, 0), **const),
                  pl.BlockSpec((1, d), lambda i, s: (0, 0), **const)],
        out_specs=(pl.BlockSpec((tm, d), lambda i, s: (i, 0)), pl.BlockSpec((tm, d), lambda i, s: (i, 0))),
        scratch_shapes=[pltpu.VMEM((PEER_HEADS * PEER_QDIM, tm), F32), route, route, route, route,
                        pltpu.VMEM((PEER_HEADS, tm), F32), pltpu.VMEM((PEER_CAND, tm), F32),
                        pltpu.VMEM((te, tm), F32), pltpu.VMEM((te, tm), F32),
                        pltpu.VMEM((tm, te), BF16), pltpu.VMEM((tm, te), BF16)],
        compiler_params=pltpu.CompilerParams(dimension_semantics=("arbitrary", "arbitrary"),
                                             vmem_limit_bytes=PEER_VMEM_LIMIT),
        name="peer_ln",
    )(x, xb, wq_t, sk, u, v, g.reshape(1, d), b.reshape(1, d))


def _slots(a):
    return a.reshape(BATCH, SLOT_CHUNKS, CHUNK, a.shape[-1])


def _sample_rows(z4):
    zs = z4[:, P_CHUNKS:P_CHUNKS + S_CHUNKS].reshape(DEC_BATCH, DEC_SEQ, z4.shape[-1])
    return jnp.pad(zs, ((0, 0), (0, S_LEN - DEC_SEQ), (0, 0)))


def _merge_rows(buf4, sample_out):
    f = buf4.shape[-1]
    rows = sample_out[:, :DEC_SEQ].reshape(BATCH, S_CHUNKS, CHUNK, f)
    return lax.dynamic_update_slice(buf4, rows, (0, P_CHUNKS, 0, 0)).reshape(M_PAD, f)


def kernel(x_prompt, x_sample, state_hgrn, state_ret, state_ssm, state_conv, state_wkv, state_shift, meta_tokens, ln_g, ln_b, even_w_in, hgrn_lb_logits, hgrn_norm_g, even_w_out, odd_w_in, conv_w, conv_b, dt_bias, a_log, d_skip, ssm_norm_g, shift_mu, rwkv_w0, rwkv_w2, rwkv_a0, rwkv_a2, rwkv_g2, rwkv_k_k, rwkv_k_a, rwkv_r_k, lnx_g, lnx_b, odd_w_out, peer_w_query, peer_sub_keys, peer_u, peer_v):
    dt = x_prompt.dtype
    lead = jnp.concatenate([jnp.zeros((P_LEAD, D_MODEL), dt), meta_tokens.astype(dt)], axis=0)
    x = jnp.concatenate([jnp.broadcast_to(lead[None, None], (BATCH, 1, CHUNK, D_MODEL)),
                         x_prompt.reshape(BATCH, P_CHUNKS - 1, CHUNK, D_MODEL),
                         x_sample.reshape(BATCH, S_CHUNKS, CHUNK, D_MODEL),
                         jnp.zeros((BATCH, 1, CHUNK, D_MODEL), dt)], axis=1).reshape(M_PAD, D_MODEL)
    xb = x.astype(BF16)
    peer_ub, peer_vb = peer_u.astype(BF16), peer_v.astype(BF16)

    cos_p, sin_p = _rope_tables(jnp.arange(P_ROWS) - P_LEAD)
    cos_s, sin_s = _rope_tables(PAST_LEN + jnp.arange(S_LEN))
    lb_table = jnp.cumsum(jax.nn.softmax(hgrn_lb_logits.astype(F32), axis=0), axis=0)

    z = matmul(xb, even_w_in[0].astype(BF16), tm=TOK_TILE, tn=1024, name="even_in")
    lb = lb_table[0].reshape(1, -1)
    ng = hgrn_norm_g[0].reshape(1, -1)
    z = _slots(z)
    mix_p, hgrn_p, ret_p = even_mixer_prompt(z, lb, ng, cos_p, sin_p)
    mix_s, hgrn_s, ret_s = even_mixer_sample(_sample_rows(z), lb, ng, cos_s, sin_s, state_hgrn[0], state_ret[0])
    mix = _merge_rows(mix_p, mix_s)
    x, xb = proj_ln(x, mix, even_w_out[0].astype(BF16), ln_g[0, 0], ln_b[0, 0], tm=TOK_TILE, name="even_out")
    x, xb = peer_ln(x, xb, peer_w_query[0].T.astype(BF16),
                    peer_sub_keys[0].reshape(2 * PEER_HEADS, PEER_KEYS, PEER_QDIM // 2).astype(BF16),
                    peer_ub, peer_vb, ln_g[0, 1], ln_b[0, 1], layer=0)

    w_in1, params = _odd_params(odd_w_in[0], conv_w[0], conv_b[0], dt_bias[0], a_log[0], d_skip[0], ssm_norm_g[0],
                                shift_mu[0], rwkv_w0[0], rwkv_w2[0], rwkv_a0[0], rwkv_a2[0], rwkv_g2[0], rwkv_k_k[0],
                                rwkv_k_a[0], rwkv_r_k[0], lnx_g[0], lnx_b[0])
    z = _slots(matmul(xb, w_in1, tm=TOK_TILE, tn=1024, name="odd_in"))
    mix_p, ssm_p, wkv_p, conv_p, shift_p = odd_mixer_prompt(z, params)
    conv8 = jnp.pad(state_conv[0], ((0, 0), (8 - (CONV_W - 1), 0), (0, 0)))
    shift8 = jnp.pad(state_shift[0][:, None, :], ((0, 0), (7, 0), (0, SHIFT_W - SHIFT_DIM)))
    mix_s, ssm_s, wkv_s, conv_s, shift_s = odd_mixer_sample(_sample_rows(z), params,
                                                            (state_ssm[0], state_wkv[0], conv8, shift8))
    mix = _merge_rows(mix_p, mix_s)
    x, xb = proj_ln(x, mix, odd_w_out[0].astype(BF16), ln_g[1, 0], ln_b[1, 0], tm=TOK_TILE, name="odd_out")
    x, xb = peer_ln(x, xb, peer_w_query[1].T.astype(BF16),
                    peer_sub_keys[1].reshape(2 * PEER_HEADS, PEER_KEYS, PEER_QDIM // 2).astype(BF16),
                    peer_ub, peer_vb, ln_g[1, 1], ln_b[1, 1], layer=1)

    x4 = _slots(x)
    y_prompt = x4[:, 1:P_CHUNKS].reshape(BATCH, SEQ, D_MODEL)
    y_sample = x4[:, P_CHUNKS:P_CHUNKS + S_CHUNKS].reshape(DEC_BATCH, DEC_SEQ, D_MODEL)
    nc = CONV_W - 1
    return (y_prompt, y_sample, hgrn_p[None], hgrn_s[None], ret_p[None], ret_s[None], ssm_p[None], ssm_s[None],
            conv_p[None, :, 8 - nc:], conv_s[None, :, DEC_SEQ - nc:DEC_SEQ], wkv_p[None], wkv_s[None],
            shift_p[None, :, 7, :SHIFT_DIM], shift_s[None, :, DEC_SEQ - 1, :SHIFT_DIM])
```

```python
import functools
import math

import jax
import jax.numpy as jnp
from jax import lax
from jax.experimental import pallas as pl
from jax.experimental.pallas import tpu as pltpu

D_MODEL = 2048
BATCH = 4
SEQ = 2048
DEPTH = 2
DEC_BATCH = 128
DEC_SEQ = 4
PAST_LEN = 16384
N_META = 16
CHUNK = 64

H_A, DK_A, DV_A = 8, 128, 128
H_B, DK_B, DV_B = 4, 128, 256
EVEN_IN = 4 * H_A * DK_A + 2 * H_B * DK_B + 2 * H_B * DV_B
EVEN_MIX = H_A * DV_A + H_B * DV_B

H_C, P_C, N_C, G_C, CONV_W = 16, 64, 128, 2, 4
DI_C = H_C * P_C
CONV_DIM = DI_C + 2 * G_C * N_C
H_D, P_D = 16, 64
DI_D = H_D * P_D
R_W, R_A, R_G = 64, 64, 160
SHIFT_DIM = 3 * DI_D + R_W + R_A + R_G
ODD_IN = DI_C + CONV_DIM + H_C + SHIFT_DIM
ODD_MIX = DI_C + DI_D

PEER_KEYS = 128
PEER_EXPERTS = PEER_KEYS * PEER_KEYS
PEER_HEADS = 8
PEER_TOPK = 16
PEER_QDIM = 256

ALPHA = (2.0 * DEPTH) ** 0.25
LN_EPS = 1e-5
RMS_EPS = 1e-6
RWKV_GN_EPS = 64e-5
ROPE_BASE = 10000.0

F32 = jnp.float32
BF16 = jnp.bfloat16

P_LEAD = CHUNK - N_META
P_ROWS = CHUNK + SEQ
P_CHUNKS = P_ROWS // CHUNK
S_CHUNKS = DEC_BATCH * DEC_SEQ // (BATCH * CHUNK)
SLOT_CHUNKS = P_CHUNKS + S_CHUNKS + 1
S_LEN = 8
SAMPLE_GROUP = 4
ODD_SAMPLE_GROUP = 4
TOK_TILE = 512
M_PAD = BATCH * SLOT_CHUNKS * CHUNK
assert M_PAD % 1024 == 0 and DEC_BATCH * DEC_SEQ == BATCH * S_CHUNKS * CHUNK

VMEM_LIMIT = 56 * 1024 * 1024
PEER_VMEM_LIMIT = 60 * 1024 * 1024


def _dot_bf(a, b, dims=(((1,), (0,)), ((), ()))):
    return lax.dot_general(a.astype(BF16), b.astype(BF16), dims, preferred_element_type=F32)


def _parts(x, n):
    out, rem = [], x
    for i in range(n):
        p = rem.astype(BF16)
        out.append(p)
        if i + 1 < n:
            rem = rem - p.astype(F32)
    return out


def _dot_sel(x, sel, dims=(((1,), (0,)), ((), ())), n=3, sel_first=False):
    sel = sel.astype(BF16)
    acc = None
    for p in _parts(x, n):
        d = lax.dot_general(sel, p, dims, preferred_element_type=F32) if sel_first else \
            lax.dot_general(p, sel, dims, preferred_element_type=F32)
        acc = d if acc is None else acc + d
    return acc


NT = (((1,), (1,)), ((), ()))
TN = (((0,), (0,)), ((), ()))


def _sigmoid(x):
    return 1.0 / (1.0 + jnp.exp(-x))


def _silu(x):
    return x * _sigmoid(x)


def _mm_kernel(x_ref, w_ref, o_ref):
    o_ref[...] = jnp.dot(x_ref[...].astype(BF16), w_ref[...], preferred_element_type=F32).astype(o_ref.dtype)


def matmul(x, w, *, tm, tn, out_dtype=F32, name="matmul"):
    m, k = x.shape
    n = w.shape[1]
    return pl.pallas_call(
        _mm_kernel,
        out_shape=jax.ShapeDtypeStruct((m, n), out_dtype),
        grid=(n // tn, m // tm),
        in_specs=[pl.BlockSpec((tm, k), lambda j, i: (i, 0)),
                  pl.BlockSpec((k, tn), lambda j, i: (0, j))],
        out_specs=pl.BlockSpec((tm, tn), lambda j, i: (i, j)),
        compiler_params=pltpu.CompilerParams(dimension_semantics=("arbitrary", "arbitrary"),
                                             vmem_limit_bytes=VMEM_LIMIT),
        name=name,
    )(x, w)


def _ln_rows(v, g, b):
    mu = jnp.mean(v, -1, keepdims=True)
    c = v - mu
    var = jnp.mean(c * c, -1, keepdims=True)
    return c * lax.rsqrt(var + LN_EPS) * g + b


def _proj_ln_kernel(x_ref, m_ref, w_ref, g_ref, b_ref, o_ref, ob_ref):
    acc = jnp.dot(m_ref[...].astype(BF16), w_ref[...], preferred_element_type=F32)
    y = _ln_rows(ALPHA * x_ref[...] + acc, g_ref[...], b_ref[...])
    o_ref[...] = y
    ob_ref[...] = y.astype(BF16)


def proj_ln(x, mix, w, g, b, *, tm, name="proj_ln"):
    m, d = x.shape
    k = mix.shape[1]
    return pl.pallas_call(
        _proj_ln_kernel,
        out_shape=(jax.ShapeDtypeStruct((m, d), F32), jax.ShapeDtypeStruct((m, d), BF16)),
        grid=(m // tm,),
        in_specs=[pl.BlockSpec((tm, d), lambda i: (i, 0)),
                  pl.BlockSpec((tm, k), lambda i: (i, 0)),
                  pl.BlockSpec((k, d), lambda i: (0, 0)),
                  pl.BlockSpec((1, d), lambda i: (0, 0)),
                  pl.BlockSpec((1, d), lambda i: (0, 0))],
        out_specs=(pl.BlockSpec((tm, d), lambda i: (i, 0)), pl.BlockSpec((tm, d), lambda i: (i, 0))),
        compiler_params=pltpu.CompilerParams(dimension_semantics=("arbitrary",),
                                             vmem_limit_bytes=VMEM_LIMIT),
        name=name,
    )(x, mix, w, g.reshape(1, d), b.reshape(1, d))


def _row_info(L, lead, nvalid):
    r = lax.broadcasted_iota(jnp.int32, (L, 1), 0)
    return r, (r >= lead) & (r < lead + nvalid)


def _seq_views(ref, G):
    if len(ref.shape) == 2:
        return [ref]
    if len(ref.shape) == 3:
        return [ref.at[g] for g in range(G)]
    return [ref.at[g, 0] for g in range(G)]


def _tril(L, strict=False):
    r = lax.broadcasted_iota(jnp.int32, (L, L), 0)
    c = lax.broadcasted_iota(jnp.int32, (L, L), 1)
    return (c < r) if strict else (c <= r)


def _hgrn_heads(qs, ks, vs, logfs, s0s, L):
    n = range(len(qs))
    tril = _tril(L).astype(F32)
    ones_l = jnp.ones((L, DK_A), F32)
    ones_k = jnp.ones((DK_A, DK_A), BF16)
    b = [_dot_sel(logfs[i], tril, sel_first=True) for i in n]
    b_end_col = [_dot_sel(logfs[i], ones_l, TN) for i in n]
    o = [_dot_bf(qs[i] * jnp.exp(b[i]), s0s[i]) for i in n]
    upd = [_dot_bf(ks[i] * jnp.exp(b[i][L - 1:L, :] - b[i]), vs[i], TN) for i in n]
    s_new = [jnp.exp(b_end_col[i]) * s0s[i] + upd[i] for i in n]
    sub = min(16, L)
    t_idx = lax.broadcasted_iota(jnp.int32, (sub, 1), 0)
    outs = [[] for _ in n]
    for blk in range(L // sub):
        r0 = blk * sub
        acc = [o[i][r0:r0 + sub] for i in n]
        if blk > 0:
            a = [_dot_bf(qs[i][r0:r0 + sub] * jnp.exp(b[i][r0:r0 + sub] - b[i][r0 - 1:r0, :]),
                         ks[i][0:r0] * jnp.exp(b[i][r0 - 1:r0, :] - b[i][0:r0]), NT) for i in n]
            acc = [acc[i] + _dot_bf(a[i], vs[i][0:r0]) for i in n]
        prod = [jnp.concatenate(
            [qs[i][r0:r0 + sub] * ks[i][r0 + s:r0 + s + 1, :] *
             jnp.exp(jnp.where(t_idx >= s, b[i][r0:r0 + sub] - b[i][r0 + s:r0 + s + 1, :], -jnp.inf))
             for s in range(sub)], axis=0).astype(BF16) for i in n]
        score = [jnp.dot(prod[i], ones_k, preferred_element_type=F32) for i in n]
        for i in n:
            t = acc[i]
            for s in range(sub):
                t = t + score[i][s * sub:(s + 1) * sub, :] * vs[i][r0 + s:r0 + s + 1, :]
            outs[i].append(t)
    return [jnp.concatenate(x, axis=0) if len(x) > 1 else x[0] for x in outs], s_new


def _chunk_or_zero(chunk_fn, o_idx, *refs, chunks, **kw):
    c = pl.program_id(1)
    o_ref = refs[o_idx]

    @pl.when(c < chunks)
    def _():
        chunk_fn(*refs, chunks=chunks, **kw)

    @pl.when(c >= chunks)
    def _():
        o_ref[...] = jnp.zeros_like(o_ref)


def _even_kernel(*refs, L, nvalid, chunks, has_s0, G):
    if has_s0:
        z_ref, lb_ref, ng_ref, cos_ref, sin_ref, sh0_ref, sr0_ref, o_ref, sho_ref, sro_ref, sh_s, sr_s = refs
    else:
        z_ref, lb_ref, ng_ref, cos_ref, sin_ref, o_ref, sho_ref, sro_ref, sh_s, sr_s = refs
    c = pl.program_id(1)

    @pl.when(c == 0)
    def _():
        if has_s0:
            for g in range(G):
                sh_s[g * H_A:(g + 1) * H_A] = sh0_ref[g]
                sr_s[g * H_B:(g + 1) * H_B] = sr0_ref[g]
        else:
            sh_s[...] = jnp.zeros_like(sh_s)
            sr_s[...] = jnp.zeros_like(sr_s)

    lead = jnp.where(c == 0, P_LEAD, 0) if chunks > 1 else 0
    r, valid = _row_info(L, lead, nvalid)
    cnt_col = jnp.clip(r + 1 - lead, 0, nvalid).astype(F32)
    rr = lax.broadcasted_iota(jnp.int32, (1, L), 1)
    cnt_row = jnp.clip(rr + 1 - lead, 0, nvalid).astype(F32)

    off_q, off_f, off_i, off_g = 0, H_A * DK_A, 2 * H_A * DK_A, 2 * H_A * DK_A + H_A * DV_A
    base_b = 2 * H_A * DK_A + 2 * H_A * DV_A
    off_bq, off_bk, off_bv = base_b, base_b + H_B * DK_B, base_b + 2 * H_B * DK_B
    off_bg = off_bv + H_B * DV_B
    ng = ng_ref[...]
    zv, ov = _seq_views(z_ref, G), _seq_views(o_ref, G)
    ia = [(g, h) for g in range(G) for h in range(H_A)]
    ib = [(g, h) for g in range(G) for h in range(H_B)]
    na, nb_ = range(len(ia)), range(len(ib))
    af = [zv[g][:, off_f + h * DK_A: off_f + (h + 1) * DK_A] for g, h in ia]
    lbs = [lb_ref[:, h * DK_A:(h + 1) * DK_A] for g, h in ia]
    logf = [jnp.where(valid, jnp.log(lbs[i] + (1.0 - lbs[i]) * _sigmoid(af[i])), 0.0) for i in na]
    ka = [jnp.where(valid, (1.0 - lbs[i]) * _sigmoid(-af[i]), 0.0) for i in na]
    qa = [_silu(zv[g][:, off_q + h * DK_A: off_q + (h + 1) * DK_A]) for g, h in ia]
    va = [zv[g][:, off_i + h * DV_A: off_i + (h + 1) * DV_A] for g, h in ia]
    cosv, sinv = cos_ref[...], sin_ref[...]
    bq = [zv[g][:, off_bq + h * DK_B: off_bq + (h + 1) * DK_B] for g, h in ib]
    bk = [zv[g][:, off_bk + h * DK_B: off_bk + (h + 1) * DK_B] for g, h in ib]
    vb = [zv[g][:, off_bv + h * DV_B: off_bv + (h + 1) * DV_B] for g, h in ib]
    qb = [bq[i] * cosv + pltpu.roll(bq[i], DK_B // 2, 1) * sinv for i in nb_]
    kb = [jnp.where(valid, (bk[i] * cosv + pltpu.roll(bk[i], DK_B // 2, 1) * sinv) * (DK_B ** -0.5), 0.0) for i in nb_]
    lg = [math.log(1.0 - 2.0 ** (-5.0 - h)) for g, h in ib]

    oa, sa_new = _hgrn_heads(qa, ka, va, logf, [sh_s[i] for i in na], L)
    causal = _tril(L)
    cnt_end = cnt_col[L - 1:L, :]
    scores = [_dot_bf(qb[i], kb[i], NT) * jnp.where(causal, jnp.exp((cnt_col - cnt_row) * lg[i]), 0.0) for i in nb_]
    inter = [_dot_bf(qb[i] * jnp.exp(cnt_col * lg[i]), sr_s[i]) for i in nb_]
    ob = [_dot_bf(scores[i], vb[i]) + inter[i] for i in nb_]
    updb = [_dot_bf(kb[i] * jnp.exp((cnt_end - cnt_col) * lg[i]), vb[i], TN) for i in nb_]
    for i, (g, h) in enumerate(ia):
        sh_s[i] = sa_new[i]
        o = oa[i] * lax.rsqrt(jnp.mean(oa[i] * oa[i], -1, keepdims=True) + RMS_EPS) * ng
        ov[g][:, h * DV_A:(h + 1) * DV_A] = o * _silu(zv[g][:, off_g + h * DV_A: off_g + (h + 1) * DV_A])
    for i, (g, h) in enumerate(ib):
        sr_s[i] = jnp.exp(cnt_end * lg[i]) * sr_s[i] + updb[i]
        mu = jnp.mean(ob[i], -1, keepdims=True)
        cc = ob[i] - mu
        o = cc * lax.rsqrt(jnp.mean(cc * cc, -1, keepdims=True) + LN_EPS)
        ov[g][:, H_A * DV_A + h * DV_B: H_A * DV_A + (h + 1) * DV_B] = \
            o * _silu(zv[g][:, off_bg + h * DV_B: off_bg + (h + 1) * DV_B])

    @pl.when(c == chunks - 1)
    def _():
        for g in range(G):
            sho_ref[g] = sh_s[g * H_A:(g + 1) * H_A]
            sro_ref[g] = sr_s[g * H_B:(g + 1) * H_B]


def even_mixer_prompt(z, lb, ng, cos, sin, *, chunks=P_CHUNKS):
    nb = z.shape[0]
    kern = functools.partial(_chunk_or_zero, _even_kernel, 5, L=CHUNK, nvalid=CHUNK, chunks=chunks, has_s0=False, G=nb)
    blk = lambda b, c: (0, c, 0, 0)
    whole = lambda b, c: (0, 0, 0, 0)
    rope = lambda b, c: (jnp.minimum(c, chunks - 1), 0)
    return pl.pallas_call(
        kern,
        out_shape=(jax.ShapeDtypeStruct(z.shape[:3] + (EVEN_MIX,), F32),
                   jax.ShapeDtypeStruct((nb, H_A, DK_A, DV_A), F32),
                   jax.ShapeDtypeStruct((nb, H_B, DK_B, DV_B), F32)),
        grid=(1, z.shape[1]),
        in_specs=[pl.BlockSpec((nb, 1, CHUNK, EVEN_IN), blk),
                  pl.BlockSpec((1, H_A * DK_A), lambda b, c: (0, 0)),
                  pl.BlockSpec((1, DV_A), lambda b, c: (0, 0)),
                  pl.BlockSpec((CHUNK, DK_B), rope),
                  pl.BlockSpec((CHUNK, DK_B), rope)],
        out_specs=(pl.BlockSpec((nb, 1, CHUNK, EVEN_MIX), blk),
                   pl.BlockSpec((nb, H_A, DK_A, DV_A), whole),
                   pl.BlockSpec((nb, H_B, DK_B, DV_B), whole)),
        scratch_shapes=[pltpu.VMEM((nb * H_A, DK_A, DV_A), F32), pltpu.VMEM((nb * H_B, DK_B, DV_B), F32)],
        compiler_params=pltpu.CompilerParams(dimension_semantics=("arbitrary", "arbitrary"),
                                             vmem_limit_bytes=VMEM_LIMIT),
        name="even_mixer_prompt",
    )(z, lb, ng, cos, sin)


def even_mixer_sample(z, lb, ng, cos, sin, s_hgrn, s_ret):
    nb = z.shape[0]
    G = SAMPLE_GROUP
    kern = functools.partial(_even_kernel, L=S_LEN, nvalid=DEC_SEQ, chunks=1, has_s0=True, G=G)
    return pl.pallas_call(
        kern,
        out_shape=(jax.ShapeDtypeStruct((nb, S_LEN, EVEN_MIX), F32),
                   jax.ShapeDtypeStruct((nb, H_A, DK_A, DV_A), F32),
                   jax.ShapeDtypeStruct((nb, H_B, DK_B, DV_B), F32)),
        grid=(nb // G, 1),
        in_specs=[pl.BlockSpec((G, S_LEN, EVEN_IN), lambda b, c: (b, 0, 0)),
                  pl.BlockSpec((1, H_A * DK_A), lambda b, c: (0, 0)),
                  pl.BlockSpec((1, DV_A), lambda b, c: (0, 0)),
                  pl.BlockSpec((S_LEN, DK_B), lambda b, c: (0, 0)),
                  pl.BlockSpec((S_LEN, DK_B), lambda b, c: (0, 0)),
                  pl.BlockSpec((G, H_A, DK_A, DV_A), lambda b, c: (b, 0, 0, 0)),
                  pl.BlockSpec((G, H_B, DK_B, DV_B), lambda b, c: (b, 0, 0, 0))],
        out_specs=(pl.BlockSpec((G, S_LEN, EVEN_MIX), lambda b, c: (b, 0, 0)),
                   pl.BlockSpec((G, H_A, DK_A, DV_A), lambda b, c: (b, 0, 0, 0)),
                   pl.BlockSpec((G, H_B, DK_B, DV_B), lambda b, c: (b, 0, 0, 0))),
        scratch_shapes=[pltpu.VMEM((G * H_A, DK_A, DV_A), F32), pltpu.VMEM((G * H_B, DK_B, DV_B), F32)],
        compiler_params=pltpu.CompilerParams(dimension_semantics=("arbitrary", "arbitrary"),
                                             vmem_limit_bytes=VMEM_LIMIT),
        name="even_mixer_sample",
    )(z, lb, ng, cos, sin, s_hgrn, s_ret)


def _rope_tables(pos):
    half = DK_B // 2
    inv = ROPE_BASE ** (-jnp.arange(half, dtype=F32) / half)
    ang = pos.astype(F32)[:, None] * inv
    cos, sin = jnp.cos(ang), jnp.sin(ang)
    return jnp.concatenate([cos, cos], -1), jnp.concatenate([-sin, sin], -1)


ODD_PAD = 6144
O_Z, O_X, O_R, O_K, O_V, O_T = 0, DI_C, DI_C + CONV_DIM, DI_C + CONV_DIM + DI_D, DI_C + CONV_DIM + 2 * DI_D, \
    DI_C + CONV_DIM + 3 * DI_D
T_W = 384
T_DT = R_W + R_A + R_G
SHIFT_W = 3 * DI_D + T_W
HALF = 64


def _softplus(x):
    return jnp.maximum(x, 0.0) + jnp.log1p(jnp.exp(-jnp.abs(x)))


def _lane_lt(n, width):
    return lax.broadcasted_iota(jnp.int32, (1, width), 1) < n


def _pair_ones():
    r = lax.broadcasted_iota(jnp.int32, (2 * HALF, 2 * HALF), 0) < HALF
    c = lax.broadcasted_iota(jnp.int32, (2 * HALF, 2 * HALF), 1) < HALF
    return r == c


def _stack_pair(x, m_a):
    return jnp.concatenate([jnp.where(m_a, x, 0.0), jnp.where(m_a, 0.0, x)], axis=0)


def _rwkv_pairs(rs, ks_, vs_, als, bes, logws, Gs, bds, L):
    n = range(len(rs))
    m_a = _lane_lt(HALF, 2 * HALF)
    m_l = _lane_lt(L, 2 * L)
    ri = lax.broadcasted_iota(jnp.int32, (2 * L, 2 * L), 0)
    ci = lax.broadcasted_iota(jnp.int32, (2 * L, 2 * L), 1)
    cm = jnp.where(ci >= L, ci - L, ci)
    keep = cm < jnp.where(ri >= L, ri - L + 1, ri)
    eye = (ri == ci).astype(F32)
    blk4 = (ri // 4) == (ci // 4)
    pair_blk = _pair_ones()

    e_inv = [jnp.exp(-Gs[i]) for i in n]
    lhs = [jnp.concatenate([als[i] * jnp.exp(Gs[i] - logws[i]), rs[i] * jnp.exp(Gs[i])], axis=0).astype(BF16) for i in n]
    ks = [_stack_pair(ks_[i] * e_inv[i], m_a).astype(BF16) for i in n]
    bs = [_stack_pair(bes[i] * e_inv[i], m_a).astype(BF16) for i in n]
    vs = [_stack_pair(vs_[i], m_a).astype(BF16) for i in n]
    a_k = [jnp.where(keep, _dot_bf(lhs[i], ks[i], NT), 0.0) for i in n]
    a_b = [jnp.where(keep, _dot_bf(lhs[i], bs[i], NT), 0.0) for i in n]
    sb = [_dot_bf(lhs[i], bds[i], NT) for i in n]
    base = [sb[i] + _dot_bf(a_k[i], vs[i]) for i in n]
    n_bd = [jnp.concatenate([jnp.where(m_l, a_b[i][0:L], 0.0), jnp.where(m_l, 0.0, a_b[i][0:L])], axis=0) for i in n]
    n4 = [jnp.where(blk4, n_bd[i], 0.0) for i in n]
    n4sq = [_dot_bf(n4[i], n4[i]) for i in n]
    t_inv = [eye - n4[i] for i in n]
    t_inv = [t_inv[i] + _dot_bf(t_inv[i], n4sq[i]) for i in n]
    blk = 4
    while blk < L:
        off = ((ri // (2 * blk)) == (ci // (2 * blk))) & ((ri % (2 * blk)) >= blk) & ((ci % (2 * blk)) < blk)
        ct = [_dot_bf(jnp.where(off, n_bd[i], 0.0), t_inv[i]) for i in n]
        t_inv = [t_inv[i] - _dot_bf(t_inv[i], ct[i]) for i in n]
        blk *= 2
    u = [_dot_bf(t_inv[i][0:L, :] + t_inv[i][L:2 * L, :], _stack_pair(base[i][0:L], m_a)) for i in n]
    y = [base[i][L:2 * L] - _dot_bf(a_b[i][L:2 * L], _stack_pair(u[i], m_a)) for i in n]
    gl = [Gs[i][L - 1:L, :] for i in n]
    dec = [jnp.exp(gl[i] - Gs[i]) for i in n]
    upd = [_dot_bf(jnp.concatenate([vs_[i], u[i]], axis=0),
                   jnp.concatenate([ks_[i] * dec[i], -(bes[i] * dec[i])], axis=0), TN) for i in n]
    bd_new = [bds[i] * jnp.exp(gl[i]) + jnp.where(pair_blk, upd[i], 0.0) for i in n]
    return y, bd_new


def _ssd_pair(xdt, cg, bg, cb, bcs, brows, sp, L):
    m_a = _lane_lt(HALF, 2 * HALF)
    causal = _tril(L)
    out = jnp.zeros((L, 2 * HALF), F32)
    ends = [bc[L - 1:L, :] for bc in bcs]
    sp_new = sp * jnp.where(m_a, jnp.exp(ends[0]), jnp.exp(ends[1]))
    for x in range(2):
        keep = m_a if x == 0 else jnp.logical_not(m_a)
        seg = jnp.where(causal, jnp.exp(bcs[x] - brows[x]), 0.0)
        xm = jnp.where(keep, xdt, 0.0)
        out = out + _dot_bf(cb * seg, xm) + _dot_bf(cg * jnp.exp(bcs[x]), jnp.where(keep, sp, 0.0))
        sp_new = sp_new + _dot_bf(bg * jnp.exp(ends[x] - bcs[x]), xm, TN)
    return out, sp_new


def _odd_kernel(*refs, L, nvalid, chunks, has_s0, G):
    if has_s0:
        (z_ref, vec_ref, cp_ref, tp_ref, w2_ref, a2_ref, g2_ref, ssm0_ref, wkv0_ref, conv0_ref, shift0_ref,
         o_ref, ssmo_ref, wkvo_ref, convo_ref, shifto_ref, ssm_s, wkv_s, conv_c, shift_c) = refs
    else:
        (z_ref, vec_ref, cp_ref, tp_ref, w2_ref, a2_ref, g2_ref,
         o_ref, ssmo_ref, wkvo_ref, convo_ref, shifto_ref, ssm_s, wkv_s, conv_c, shift_c) = refs
    c = pl.program_id(1)
    npair = H_C // 2

    @pl.when(c == 0)
    def _():
        if has_s0:
            zpad = jnp.zeros((HALF, HALF), F32)
            for g in range(G):
                for p in range(npair):
                    ssm_s[g * npair + p] = jnp.concatenate([ssm0_ref[g, 2 * p], ssm0_ref[g, 2 * p + 1]], axis=1)
                    wkv_s[g * npair + p] = jnp.concatenate(
                        [jnp.concatenate([wkv0_ref[g, 2 * p], zpad], axis=1),
                         jnp.concatenate([zpad, wkv0_ref[g, 2 * p + 1]], axis=1)], axis=0)
                conv_c[g * 8:(g + 1) * 8, :] = conv0_ref[g]
                shift_c[g * 8:(g + 1) * 8, :] = shift0_ref[g]
        else:
            ssm_s[...] = jnp.zeros_like(ssm_s)
            wkv_s[...] = jnp.zeros_like(wkv_s)
            conv_c[...] = jnp.zeros_like(conv_c)
            shift_c[...] = jnp.zeros_like(shift_c)

    lead = jnp.where(c == 0, P_LEAD, 0) if chunks > 1 else 0
    r_idx, valid = _row_info(L, lead, nvalid)
    m_a = _lane_lt(HALF, 2 * HALF)
    tril = _tril(L).astype(F32)
    triu = (lax.broadcasted_iota(jnp.int32, (L, L), 0) <= lax.broadcasted_iota(jnp.int32, (L, L), 1)).astype(F32)
    ones_blk = _pair_ones().astype(F32)
    r8 = lax.broadcasted_iota(jnp.int32, (8, 1), 0)
    zv, ov = _seq_views(z_ref, G), _seq_views(o_ref, G)
    pairs = range(npair)
    sls = [slice(p * 2 * HALF, (p + 1) * 2 * HALF) for p in pairs]
    w2_hi, w2_lo = _parts(w2_ref[...], 2)

    def shift_mix(p_raw, carry_row, mu):
        pv = jnp.where(valid, p_raw, 0.0)
        prev = jnp.where(r_idx == 0, carry_row, pltpu.roll(pv, 1, 0))
        return pv, pv + (prev - pv) * mu

    rw = []
    for g in range(G):
        zg, og = zv[g], ov[g]
        cs = slice(g * 8, (g + 1) * 8)
        xbc = jnp.where(valid, zg[:, O_X:O_X + CONV_DIM], 0.0)
        c8 = conv_c[cs, :]
        conv = cp_ref[CONV_W:CONV_W + 1, :] + xbc * cp_ref[CONV_W - 1:CONV_W, :]
        for j in range(1, CONV_W):
            rolled = pltpu.roll(xbc, j, 0)
            head = jnp.where(r8 < j, pltpu.roll(c8, j, 0), rolled[0:8])
            sh = jnp.concatenate([head, rolled[8:]], axis=0) if L > 8 else head
            conv = conv + sh * cp_ref[CONV_W - 1 - j:CONV_W - j, :]
        conv_c[cs, :] = xbc[L - 8:L]
        act = _silu(conv)
        xc, bm, cm = act[:, :DI_C], act[:, DI_C:DI_C + G_C * N_C], act[:, DI_C + G_C * N_C:]

        t_raw = zg[:, O_T:O_T + T_W]
        dt = jnp.where(valid, _softplus(t_raw + tp_ref[1:2, :])[:, T_DT:T_DT + H_C], 0.0)
        logf = dt * tp_ref[2:3, T_DT:T_DT + H_C]
        bc_all = _dot_sel(logf, tril, sel_first=True)
        brow_all = _dot_sel(logf, triu, TN)
        per_g = npair // G_C
        for grp in range(G_C):
            cg = cm[:, grp * N_C:(grp + 1) * N_C]
            bg = bm[:, grp * N_C:(grp + 1) * N_C]
            cb = _dot_bf(cg, bg, NT)
            ys, ss = [], jnp.zeros((L, 1), F32)
            for pp in range(per_g):
                p = grp * per_g + pp
                sl = sls[p]
                h_a, h_b = 2 * p, 2 * p + 1
                xcp = xc[:, sl]
                xdt = xcp * jnp.where(m_a, dt[:, h_a:h_a + 1], dt[:, h_b:h_b + 1])
                o, sp_new = _ssd_pair(xdt, cg, bg, cb, [bc_all[:, h_a:h_a + 1], bc_all[:, h_b:h_b + 1]],
                                      [brow_all[h_a:h_a + 1, :], brow_all[h_b:h_b + 1, :]], ssm_s[g * npair + p], L)
                ssm_s[g * npair + p] = sp_new
                y = (o + xcp * vec_ref[0:1, sl]) * _silu(zg[:, O_Z + p * 2 * HALF:O_Z + (p + 1) * 2 * HALF])
                ys.append(y)
                ss = ss + jnp.sum(y * y, axis=-1, keepdims=True)
            scale = lax.rsqrt(ss / (DI_C // G_C) + RMS_EPS)
            for pp in range(per_g):
                sl = sls[grp * per_g + pp]
                og[:, sl] = ys[pp] * scale * vec_ref[1:2, sl]

        r_raw, r = shift_mix(zg[:, O_R:O_R + DI_D], shift_c[g * 8 + 7:g * 8 + 8, 0:DI_D], vec_ref[2:3, :])
        k_raw, k = shift_mix(zg[:, O_K:O_K + DI_D], shift_c[g * 8 + 7:g * 8 + 8, DI_D:2 * DI_D], vec_ref[3:4, :])
        v_raw, v = shift_mix(zg[:, O_V:O_V + DI_D], shift_c[g * 8 + 7:g * 8 + 8, 2 * DI_D:3 * DI_D], vec_ref[4:5, :])
        t_rawm, tm = shift_mix(t_raw, shift_c[g * 8 + 7:g * 8 + 8, 3 * DI_D:SHIFT_W], tp_ref[0:1, :])
        shift_c[cs, 0:DI_D] = r_raw[L - 8:L]
        shift_c[cs, DI_D:2 * DI_D] = k_raw[L - 8:L]
        shift_c[cs, 2 * DI_D:3 * DI_D] = v_raw[L - 8:L]
        shift_c[cs, 3 * DI_D:SHIFT_W] = t_rawm[L - 8:L]

        th_hi, th_lo = _parts(jnp.tanh(tm), 2)
        w_pre = vec_ref[5:6, :] + (jnp.dot(th_hi, w2_hi, preferred_element_type=F32) +
                                   jnp.dot(th_lo, w2_hi, preferred_element_type=F32) +
                                   jnp.dot(th_hi, w2_lo, preferred_element_type=F32))
        logw = jnp.where(valid, -jnp.exp(-_softplus(-w_pre) - 0.5), 0.0)
        g_cum = _dot_sel(logw, tril, sel_first=True)
        a = _sigmoid(vec_ref[6:7, :] + _dot_bf(tm, a2_ref[...]))
        gate = _dot_bf(_sigmoid(tm), g2_ref[...])
        kkr = k * vec_ref[7:8, :]
        k2 = k * (1.0 + (a - 1.0) * vec_ref[8:9, :])
        rw.append(dict(r=r, v=v, a=a, gate=gate, kkr=kkr, k2=k2, rk=r * k2 * vec_ref[9:10, :], logw=logw, g_cum=g_cum))

    gp = [(g, p) for g in range(G) for p in pairs]
    n = range(len(gp))
    nrm = [jnp.sqrt(_dot_sel(rw[g]["kkr"][:, sls[p]] * rw[g]["kkr"][:, sls[p]], ones_blk, n=2)) for g, p in gp]
    al = [jnp.where(valid, rw[g]["kkr"][:, sls[p]] / jnp.maximum(nrm[i], 1e-12), 0.0) for i, (g, p) in enumerate(gp)]
    be = [al[i] * rw[g]["a"][:, sls[p]] for i, (g, p) in enumerate(gp)]
    ys, bd_new = _rwkv_pairs([rw[g]["r"][:, sls[p]] for g, p in gp],
                             [jnp.where(valid, rw[g]["k2"][:, sls[p]], 0.0) for g, p in gp],
                             [rw[g]["v"][:, sls[p]] for g, p in gp], al, be,
                             [rw[g]["logw"][:, sls[p]] for g, p in gp], [rw[g]["g_cum"][:, sls[p]] for g, p in gp],
                             [wkv_s[i] for i in n], L)
    for i in n:
        wkv_s[i] = bd_new[i]
    mu = [_dot_sel(ys[i], ones_blk, n=2) * (1.0 / P_D) for i in n]
    yc = [ys[i] - mu[i] for i in n]
    var = [_dot_sel(yc[i] * yc[i], ones_blk, n=2) * (1.0 / P_D) for i in n]
    bonus = [_dot_sel(rw[g]["rk"][:, sls[p]], ones_blk, n=2) for g, p in gp]
    for i, (g, p) in enumerate(gp):
        sl = sls[p]
        yn = yc[i] * lax.rsqrt(var[i] + RWKV_GN_EPS) * vec_ref[10:11, sl] + vec_ref[11:12, sl]
        ov[g][:, DI_C + p * 2 * HALF:DI_C + (p + 1) * 2 * HALF] = (yn + bonus[i] * rw[g]["v"][:, sl]) * rw[g]["gate"][:, sl]

    @pl.when(c == chunks - 1)
    def _():
        for g in range(G):
            for p in range(npair):
                sp = ssm_s[g * npair + p]
                ssmo_ref[g, 2 * p] = sp[:, 0:HALF]
                ssmo_ref[g, 2 * p + 1] = sp[:, HALF:2 * HALF]
                bd = wkv_s[g * npair + p]
                wkvo_ref[g, 2 * p] = bd[0:HALF, 0:HALF]
                wkvo_ref[g, 2 * p + 1] = bd[HALF:2 * HALF, HALF:2 * HALF]
            convo_ref[g] = conv_c[g * 8:(g + 1) * 8, :]
            shifto_ref[g] = shift_c[g * 8:(g + 1) * 8, :]


def _odd_call(z, params, states, *, L, nvalid, nb, chunks, name, G):
    has_s0 = states is not None
    kern = functools.partial(_chunk_or_zero, _odd_kernel, 11 if has_s0 else 7, L=L, nvalid=nvalid, chunks=chunks,
                             has_s0=has_s0, G=G)
    npair = H_C // 2
    steps = z.shape[1] if z.ndim == 4 else chunks
    if z.ndim == 4:
        z_spec = pl.BlockSpec((G, 1, L, ODD_PAD), lambda b, c: (0, c, 0, 0))
        o_spec = pl.BlockSpec((G, 1, L, ODD_MIX), lambda b, c: (0, c, 0, 0))
        o_shape = jax.ShapeDtypeStruct(z.shape[:3] + (ODD_MIX,), F32)
    else:
        z_spec = pl.BlockSpec((G, L, ODD_PAD), lambda b, c: (b, 0, 0))
        o_spec = pl.BlockSpec((G, L, ODD_MIX), lambda b, c: (b, 0, 0))
        o_shape = jax.ShapeDtypeStruct((nb, L, ODD_MIX), F32)
    const2 = lambda b, c: (0, 0)
    per_b = lambda b, c: (b, 0, 0, 0)
    per_b3 = lambda b, c: (b, 0, 0)
    in_specs = [z_spec] + [pl.BlockSpec(p.shape, const2) for p in params]
    args = [z] + list(params)
    st_specs = [pl.BlockSpec((G, H_C, N_C, P_C), per_b), pl.BlockSpec((G, H_D, P_D, P_D), per_b),
                pl.BlockSpec((G, 8, CONV_DIM), per_b3), pl.BlockSpec((G, 8, SHIFT_W), per_b3)]
    if has_s0:
        in_specs += st_specs
        args += list(states)
    return pl.pallas_call(
        kern,
        out_shape=(o_shape,
                   jax.ShapeDtypeStruct((nb, H_C, N_C, P_C), F32), jax.ShapeDtypeStruct((nb, H_D, P_D, P_D), F32),
                   jax.ShapeDtypeStruct((nb, 8, CONV_DIM), F32), jax.ShapeDtypeStruct((nb, 8, SHIFT_W), F32)),
        grid=(nb // G, steps),
        in_specs=in_specs,
        out_specs=tuple([o_spec] + st_specs),
        scratch_shapes=[pltpu.VMEM((G * npair, N_C, 2 * HALF), F32), pltpu.VMEM((G * npair, 2 * HALF, 2 * HALF), F32),
                        pltpu.VMEM((G * 8, CONV_DIM), F32), pltpu.VMEM((G * 8, SHIFT_W), F32)],
        compiler_params=pltpu.CompilerParams(dimension_semantics=("arbitrary", "arbitrary"),
                                             vmem_limit_bytes=VMEM_LIMIT),
        name=name,
    )(*args)


def odd_mixer_prompt(z, params, *, chunks=P_CHUNKS):
    return _odd_call(z, params, None, L=CHUNK, nvalid=CHUNK, nb=z.shape[0], chunks=chunks,
                     name="odd_mixer_prompt", G=z.shape[0])


def odd_mixer_sample(z, params, states):
    return _odd_call(z, params, states, L=S_LEN, nvalid=DEC_SEQ, nb=z.shape[0], chunks=1,
                     name="odd_mixer_sample", G=ODD_SAMPLE_GROUP)


def _odd_params(odd_w_in, conv_w, conv_b, dt_bias, a_log, d_skip, ssm_norm_g, shift_mu, rwkv_w0, rwkv_w2, rwkv_a0,
                rwkv_a2, rwkv_g2, rwkv_k_k, rwkv_k_a, rwkv_r_k, lnx_g, lnx_b):
    o_dt = DI_C + CONV_DIM
    o_rw = o_dt + H_C
    w = jnp.concatenate([odd_w_in[:, :o_dt], odd_w_in[:, o_rw:], odd_w_in[:, o_dt:o_rw],
                         jnp.zeros((D_MODEL, ODD_PAD - ODD_IN), odd_w_in.dtype)], axis=1)
    vec = jnp.stack([jnp.repeat(d_skip, P_C), ssm_norm_g, shift_mu[:DI_D], shift_mu[DI_D:2 * DI_D],
                     shift_mu[2 * DI_D:3 * DI_D], rwkv_w0, rwkv_a0, rwkv_k_k, rwkv_k_a, rwkv_r_k.reshape(-1),
                     lnx_g, lnx_b] + [jnp.zeros((DI_D,), F32)] * 4)
    cpack = jnp.concatenate([conv_w, conv_b[None], jnp.zeros((3, CONV_DIM), F32)], axis=0)
    zt = jnp.zeros((T_W,), F32)
    tpack = jnp.stack([zt.at[:T_DT].set(shift_mu[3 * DI_D:]), zt.at[T_DT:T_DT + H_C].set(dt_bias),
                       zt.at[T_DT:T_DT + H_C].set(-jnp.exp(a_log.astype(F32)))] + [zt] * 5)
    zw = jnp.zeros((T_W, DI_D), F32)
    w2p = zw.at[:R_W].set(rwkv_w2)
    a2p = zw.at[R_W:R_W + R_A].set(rwkv_a2).astype(BF16)
    g2p = zw.at[R_W + R_A:T_DT].set(rwkv_g2).astype(BF16)
    return w.astype(BF16), (vec, cpack, tpack, w2p, a2p, g2p)


def _top16_desc(cur):
    vals = []
    for _ in range(PEER_TOPK):
        m = jnp.max(cur, axis=0, keepdims=True)
        vals.append(m)
        cur = jnp.where(cur == m, -jnp.inf, cur)
    return vals


PEER_CAND = 112


def _peer_kernel(x_ref, xb_ref, wq_ref, sk_ref, u_ref, v_ref, g_ref, b_ref, o_ref, ob_ref,
                 q_s, s1_s, s2_s, e1_s, e2_s, tau_s, cand_s, w_s, h_s, p_s, pn_s, *, tm, te):
    s = pl.program_id(1)
    ns = pl.num_programs(1)
    nk = PEER_KEYS
    neg = -jnp.inf

    @pl.when(s == 0)
    def _route():
        q_s[...] = lax.dot_general(wq_ref[...], xb_ref[...], NT, preferred_element_type=F32).astype(BF16)
        a_idx = lax.broadcasted_iota(jnp.int32, (PEER_TOPK, 1), 0)

        def head(h, carry):
            tops = []
            for c in range(2):
                row0 = pl.multiple_of((2 * h + c) * nk, nk)
                sc = jnp.dot(sk_ref[2 * h + c], q_s[pl.ds(row0, nk), :], preferred_element_type=F32)
                if c == 0:
                    s1_s[h] = sc
                else:
                    s2_s[h] = sc
                tops.append(_top16_desc(sc))
            t1, t2 = tops
            t1_all = jnp.concatenate(t1, axis=0)
            t2_all = jnp.concatenate(t2, axis=0)
            for b in range(4):
                cand_s[b * PEER_TOPK:(b + 1) * PEER_TOPK, :] = jnp.where(a_idx < PEER_TOPK // (b + 1), t1_all + t2[b], neg)
            for a in range(3):
                ok = (a_idx >= 4) & (a_idx < PEER_TOPK // (a + 1))
                cand_s[(4 + a) * PEER_TOPK:(5 + a) * PEER_TOPK, :] = jnp.where(ok, t1[a] + t2_all, neg)
            best = _top16_desc(cand_s[...])
            mx = t1[0] + t2[0]
            z = jnp.zeros_like(mx)
            for m in best:
                z = z + jnp.exp(m - mx)
            tau_s[pl.ds(h, 1), :] = best[-1]
            e1_s[h] = jnp.exp(s1_s[h] - t1[0])
            e2_s[h] = jnp.exp(s2_s[h] - t2[0]) / z
            return carry

        lax.fori_loop(0, PEER_HEADS, head, 0)
        o_ref[...] = jnp.zeros_like(o_ref)
        p_s[...] = jnp.zeros_like(p_s)

    blk = jnp.minimum(s, ns - 2)
    nj = te // nk
    d = o_ref.shape[1]
    kp, cp = d // 4, d // 8

    def mm1(k):
        part = lax.dot_general(u_ref[:, k * kp:(k + 1) * kp], xb_ref[:, k * kp:(k + 1) * kp], NT,
                               preferred_element_type=F32)
        if k == 0:
            h_s[...] = part
        else:
            h_s[...] += part

    def wbuild(j):
        i1 = blk * nj + j
        w = jnp.zeros((nk, tm), F32)
        for h in range(PEER_HEADS):
            c = s2_s[h] + s1_s[h, pl.ds(i1, 1), :]
            w = w + jnp.where(c >= tau_s[h:h + 1, :], e2_s[h] * e1_s[h, pl.ds(i1, 1), :], 0.0)
        w_s[j * nk:(j + 1) * nk, :] = w

    def act(j):
        hj = h_s[j * nk:(j + 1) * nk, :]
        g = 0.5 * hj * (1.0 + lax.erf(hj * (2.0 ** -0.5)))
        pn_s[:, j * nk:(j + 1) * nk] = jnp.transpose(w_s[j * nk:(j + 1) * nk, :] * g).astype(BF16)

    def mm2(n):
        o_ref[:, n * cp:(n + 1) * cp] += jnp.dot(p_s[...], v_ref[:, n * cp:(n + 1) * cp], preferred_element_type=F32)

    mxu = [functools.partial(mm1, k) for k in range(d // kp)] + [functools.partial(mm2, n) for n in range(d // cp)]
    vec = [functools.partial(wbuild, j) for j in range(nj)] + [functools.partial(act, j) for j in range(nj)]
    done = 0
    for i, piece in enumerate(mxu):
        piece()
        upto = (i + 1) * len(vec) // len(mxu)
        for f in vec[done:upto]:
            f()
        done = upto
    p_s[...] = pn_s[...]

    @pl.when(s == ns - 1)
    def _fin():
        y = _ln_rows(ALPHA * x_ref[...] + o_ref[...], g_ref[...], b_ref[...])
        o_ref[...] = y
        ob_ref[...] = y.astype(BF16)


def peer_ln(x, xb, wq_t, sk, u, v, g, b, *, layer=0, tm=512, te=1024):
    m, d = x.shape
    kern = functools.partial(_peer_kernel, tm=tm, te=te)
    const = dict(pipeline_mode=pl.Buffered(1))
    ne = PEER_EXPERTS // te
    route = pltpu.VMEM((PEER_HEADS, PEER_KEYS, tm), F32)
    return pl.pallas_call(
        kern,
        out_shape=(jax.ShapeDtypeStruct((m, d), F32), jax.ShapeDtypeStruct((m, d), BF16)),
        grid=(m // tm, ne + 1),
        in_specs=[pl.BlockSpec((tm, d), lambda i, s: (i, 0), **const),
                  pl.BlockSpec((tm, d), lambda i, s: (i, 0), **const),
                  pl.BlockSpec(wq_t.shape, lambda i, s: (0, 0), **const),
                  pl.BlockSpec(sk.shape, lambda i, s: (0, 0, 0), **const),
                  pl.BlockSpec((None, te, d), lambda i, s: (layer, jnp.minimum(s, ne - 1), 0)),
                  pl.BlockSpec((None, te, d), lambda i, s: (layer, jnp.maximum(s - 1, 0), 0)),
                  pl.BlockSpec((1, d), lambda i, s: (0, 0), **const),
                  pl.BlockSpec((1, d), lambda i, s: (0, 0), **const)],
        out_specs=(pl.BlockSpec((tm, d), lambda i, s: (i, 0), **const), pl.BlockSpec((tm, d), lambda i, s: (i, 0), **const)),
        scratch_shapes=[pltpu.VMEM((PEER_HEADS * PEER_QDIM, tm), BF16), route, route, route, route,
                        pltpu.VMEM((PEER_HEADS, tm), F32), pltpu.VMEM((PEER_CAND, tm), F32),
                        pltpu.VMEM((te, tm), F32), pltpu.VMEM((te, tm), F32),
                        pltpu.VMEM((tm, te), BF16), pltpu.VMEM((tm, te), BF16)],
        compiler_params=pltpu.CompilerParams(dimension_semantics=("arbitrary", "arbitrary"),
                                             vmem_limit_bytes=PEER_VMEM_LIMIT),
        name="peer_ln",
    )(x, xb, wq_t, sk, u, v, g.reshape(1, d), b.reshape(1, d))


def _slots(a):
    return a.reshape(BATCH, SLOT_CHUNKS, CHUNK, a.shape[-1])


def _sample_rows(z4):
    zs = z4[:, P_CHUNKS:P_CHUNKS + S_CHUNKS].reshape(DEC_BATCH, DEC_SEQ, z4.shape[-1])
    return jnp.pad(zs, ((0, 0), (0, S_LEN - DEC_SEQ), (0, 0)))


def _merge_rows(buf4, sample_out):
    f = buf4.shape[-1]
    rows = sample_out[:, :DEC_SEQ].reshape(BATCH, S_CHUNKS, CHUNK, f)
    return lax.dynamic_update_slice(buf4, rows, (0, P_CHUNKS, 0, 0)).reshape(M_PAD, f)


def kernel(x_prompt, x_sample, state_hgrn, state_ret, state_ssm, state_conv, state_wkv, state_shift, meta_tokens, ln_g, ln_b, even_w_in, hgrn_lb_logits, hgrn_norm_g, even_w_out, odd_w_in, conv_w, conv_b, dt_bias, a_log, d_skip, ssm_norm_g, shift_mu, rwkv_w0, rwkv_w2, rwkv_a0, rwkv_a2, rwkv_g2, rwkv_k_k, rwkv_k_a, rwkv_r_k, lnx_g, lnx_b, odd_w_out, peer_w_query, peer_sub_keys, peer_u, peer_v):
    dt = x_prompt.dtype
    lead = jnp.concatenate([jnp.zeros((P_LEAD, D_MODEL), dt), meta_tokens.astype(dt)], axis=0)
    x = jnp.concatenate([jnp.broadcast_to(lead[None, None], (BATCH, 1, CHUNK, D_MODEL)),
                         x_prompt.reshape(BATCH, P_CHUNKS - 1, CHUNK, D_MODEL),
                         x_sample.reshape(BATCH, S_CHUNKS, CHUNK, D_MODEL),
                         jnp.zeros((BATCH, 1, CHUNK, D_MODEL), dt)], axis=1).reshape(M_PAD, D_MODEL)
    xb = x.astype(BF16)
    peer_ub, peer_vb = peer_u.astype(BF16), peer_v.astype(BF16)

    cos_p, sin_p = _rope_tables(jnp.arange(P_ROWS) - P_LEAD)
    cos_s, sin_s = _rope_tables(PAST_LEN + jnp.arange(S_LEN))
    lb_table = jnp.cumsum(jax.nn.softmax(hgrn_lb_logits.astype(F32), axis=0), axis=0)

    z = matmul(xb, even_w_in[0].astype(BF16), tm=TOK_TILE, tn=1024, name="even_in")
    lb = lb_table[0].reshape(1, -1)
    ng = hgrn_norm_g[0].reshape(1, -1)
    z = _slots(z)
    mix_p, hgrn_p, ret_p = even_mixer_prompt(z, lb, ng, cos_p, sin_p)
    mix_s, hgrn_s, ret_s = even_mixer_sample(_sample_rows(z), lb, ng, cos_s, sin_s, state_hgrn[0], state_ret[0])
    mix = _merge_rows(mix_p, mix_s)
    x, xb = proj_ln(x, mix, even_w_out[0].astype(BF16), ln_g[0, 0], ln_b[0, 0], tm=TOK_TILE, name="even_out")
    x, xb = peer_ln(x, xb, peer_w_query[0].T.astype(BF16),
                    peer_sub_keys[0].reshape(2 * PEER_HEADS, PEER_KEYS, PEER_QDIM // 2).astype(BF16),
                    peer_ub, peer_vb, ln_g[0, 1], ln_b[0, 1], layer=0)

    w_in1, params = _odd_params(odd_w_in[0], conv_w[0], conv_b[0], dt_bias[0], a_log[0], d_skip[0], ssm_norm_g[0],
                                shift_mu[0], rwkv_w0[0], rwkv_w2[0], rwkv_a0[0], rwkv_a2[0], rwkv_g2[0], rwkv_k_k[0],
                                rwkv_k_a[0], rwkv_r_k[0], lnx_g[0], lnx_b[0])
    z = _slots(matmul(xb, w_in1, tm=TOK_TILE, tn=1024, name="odd_in"))
    mix_p, ssm_p, wkv_p, conv_p, shift_p = odd_mixer_prompt(z, params)
    conv8 = jnp.pad(state_conv[0], ((0, 0), (8 - (CONV_W - 1), 0), (0, 0)))
    shift8 = jnp.pad(state_shift[0][:, None, :], ((0, 0), (7, 0), (0, SHIFT_W - SHIFT_DIM)))
    mix_s, ssm_s, wkv_s, conv_s, shift_s = odd_mixer_sample(_sample_rows(z), params,
                                                            (state_ssm[0], state_wkv[0], conv8, shift8))
    mix = _merge_rows(mix_p, mix_s)
    x, xb = proj_ln(x, mix, odd_w_out[0].astype(BF16), ln_g[1, 0], ln_b[1, 0], tm=TOK_TILE, name="odd_out")
    x, xb = peer_ln(x, xb, peer_w_query[1].T.astype(BF16),
                    peer_sub_keys[1].reshape(2 * PEER_HEADS, PEER_KEYS, PEER_QDIM // 2).astype(BF16),
                    peer_ub, peer_vb, ln_g[1, 1], ln_b[1, 1], layer=1)

    x4 = _slots(x)
    y_prompt = x4[:, 1:P_CHUNKS].reshape(BATCH, SEQ, D_MODEL)
    y_sample = x4[:, P_CHUNKS:P_CHUNKS + S_CHUNKS].reshape(DEC_BATCH, DEC_SEQ, D_MODEL)
    nc = CONV_W - 1
    return (y_prompt, y_sample, hgrn_p[None], hgrn_s[None], ret_p[None], ret_s[None], ssm_p[None], ssm_s[None],
            conv_p[None, :, 8 - nc:], conv_s[None, :, DEC_SEQ - nc:DEC_SEQ], wkv_p[None], wkv_s[None],
            shift_p[None, :, 7, :SHIFT_DIM], shift_s[None, :, DEC_SEQ - 1, :SHIFT_DIM])
```

```python
import functools
import math

import jax
import jax.numpy as jnp
from jax import lax
from jax.experimental import pallas as pl
from jax.experimental.pallas import tpu as pltpu

D_MODEL = 2048
BATCH = 4
SEQ = 2048
DEPTH = 2
DEC_BATCH = 128
DEC_SEQ = 4
PAST_LEN = 16384
N_META = 16
CHUNK = 64

H_A, DK_A, DV_A = 8, 128, 128
H_B, DK_B, DV_B = 4, 128, 256
EVEN_IN = 4 * H_A * DK_A + 2 * H_B * DK_B + 2 * H_B * DV_B
EVEN_MIX = H_A * DV_A + H_B * DV_B

H_C, P_C, N_C, G_C, CONV_W = 16, 64, 128, 2, 4
DI_C = H_C * P_C
CONV_DIM = DI_C + 2 * G_C * N_C
H_D, P_D = 16, 64
DI_D = H_D * P_D
R_W, R_A, R_G = 64, 64, 160
SHIFT_DIM = 3 * DI_D + R_W + R_A + R_G
ODD_IN = DI_C + CONV_DIM + H_C + SHIFT_DIM
ODD_MIX = DI_C + DI_D

PEER_KEYS = 128
PEER_EXPERTS = PEER_KEYS * PEER_KEYS
PEER_HEADS = 8
PEER_TOPK = 16
PEER_QDIM = 256

ALPHA = (2.0 * DEPTH) ** 0.25
LN_EPS = 1e-5
RMS_EPS = 1e-6
RWKV_GN_EPS = 64e-5
ROPE_BASE = 10000.0

F32 = jnp.float32
BF16 = jnp.bfloat16

P_LEAD = CHUNK - N_META
P_ROWS = CHUNK + SEQ
P_CHUNKS = P_ROWS // CHUNK
S_CHUNKS = DEC_BATCH * DEC_SEQ // (BATCH * CHUNK)
SLOT_CHUNKS = P_CHUNKS + S_CHUNKS + 1
S_LEN = 8
SAMPLE_GROUP = 8
ODD_SAMPLE_GROUP = 4
TOK_TILE = 512
M_PAD = BATCH * SLOT_CHUNKS * CHUNK
assert M_PAD % 1024 == 0 and DEC_BATCH * DEC_SEQ == BATCH * S_CHUNKS * CHUNK

VMEM_LIMIT = 56 * 1024 * 1024
PEER_VMEM_LIMIT = 60 * 1024 * 1024


def _dot_bf(a, b, dims=(((1,), (0,)), ((), ()))):
    return lax.dot_general(a.astype(BF16), b.astype(BF16), dims, preferred_element_type=F32)


def _parts(x, n):
    out, rem = [], x
    for i in range(n):
        p = rem.astype(BF16)
        out.append(p)
        if i + 1 < n:
            rem = rem - p.astype(F32)
    return out


def _dot_sel(x, sel, dims=(((1,), (0,)), ((), ())), n=3, sel_first=False):
    sel = sel.astype(BF16)
    acc = None
    for p in _parts(x, n):
        d = lax.dot_general(sel, p, dims, preferred_element_type=F32) if sel_first else \
            lax.dot_general(p, sel, dims, preferred_element_type=F32)
        acc = d if acc is None else acc + d
    return acc


NT = (((1,), (1,)), ((), ()))
TN = (((0,), (0,)), ((), ()))


def _sigmoid(x):
    return 1.0 / (1.0 + jnp.exp(-x))


def _silu(x):
    return x * _sigmoid(x)


def _mm_kernel(x_ref, w_ref, o_ref):
    o_ref[...] = jnp.dot(x_ref[...].astype(BF16), w_ref[...], preferred_element_type=F32).astype(o_ref.dtype)


def matmul(x, w, *, tm, tn, out_dtype=F32, name="matmul"):
    m, k = x.shape
    n = w.shape[1]
    return pl.pallas_call(
        _mm_kernel,
        out_shape=jax.ShapeDtypeStruct((m, n), out_dtype),
        grid=(n // tn, m // tm),
        in_specs=[pl.BlockSpec((tm, k), lambda j, i: (i, 0)),
                  pl.BlockSpec((k, tn), lambda j, i: (0, j))],
        out_specs=pl.BlockSpec((tm, tn), lambda j, i: (i, j)),
        compiler_params=pltpu.CompilerParams(dimension_semantics=("arbitrary", "arbitrary"),
                                             vmem_limit_bytes=VMEM_LIMIT),
        name=name,
    )(x, w)


def _ln_rows(v, g, b):
    mu = jnp.mean(v, -1, keepdims=True)
    c = v - mu
    var = jnp.mean(c * c, -1, keepdims=True)
    return c * lax.rsqrt(var + LN_EPS) * g + b


def _proj_ln_kernel(x_ref, m_ref, w_ref, g_ref, b_ref, o_ref, ob_ref):
    acc = jnp.dot(m_ref[...].astype(BF16), w_ref[...], preferred_element_type=F32)
    y = _ln_rows(ALPHA * x_ref[...] + acc, g_ref[...], b_ref[...])
    o_ref[...] = y
    ob_ref[...] = y.astype(BF16)


def proj_ln(x, mix, w, g, b, *, tm, name="proj_ln"):
    m, d = x.shape
    k = mix.shape[1]
    return pl.pallas_call(
        _proj_ln_kernel,
        out_shape=(jax.ShapeDtypeStruct((m, d), F32), jax.ShapeDtypeStruct((m, d), BF16)),
        grid=(m // tm,),
        in_specs=[pl.BlockSpec((tm, d), lambda i: (i, 0)),
                  pl.BlockSpec((tm, k), lambda i: (i, 0)),
                  pl.BlockSpec((k, d), lambda i: (0, 0)),
                  pl.BlockSpec((1, d), lambda i: (0, 0)),
                  pl.BlockSpec((1, d), lambda i: (0, 0))],
        out_specs=(pl.BlockSpec((tm, d), lambda i: (i, 0)), pl.BlockSpec((tm, d), lambda i: (i, 0))),
        compiler_params=pltpu.CompilerParams(dimension_semantics=("arbitrary",),
                                             vmem_limit_bytes=VMEM_LIMIT),
        name=name,
    )(x, mix, w, g.reshape(1, d), b.reshape(1, d))


def _row_info(L, lead, nvalid):
    r = lax.broadcasted_iota(jnp.int32, (L, 1), 0)
    return r, (r >= lead) & (r < lead + nvalid)


def _seq_views(ref, G):
    if len(ref.shape) == 2:
        return [ref]
    if len(ref.shape) == 3:
        return [ref.at[g] for g in range(G)]
    return [ref.at[g, 0] for g in range(G)]


def _tril(L, strict=False):
    r = lax.broadcasted_iota(jnp.int32, (L, L), 0)
    c = lax.broadcasted_iota(jnp.int32, (L, L), 1)
    return (c < r) if strict else (c <= r)


def _hgrn_heads(qs, ks, vs, logfs, s0s, L):
    n = range(len(qs))
    tril = _tril(L).astype(F32)
    ones_l = jnp.ones((L, DK_A), F32)
    ones_k = jnp.ones((DK_A, DK_A), BF16)
    b = [_dot_sel(logfs[i], tril, sel_first=True) for i in n]
    b_end_col = [_dot_sel(logfs[i], ones_l, TN) for i in n]
    o = [_dot_bf(qs[i] * jnp.exp(b[i]), s0s[i]) for i in n]
    upd = [_dot_bf(ks[i] * jnp.exp(b[i][L - 1:L, :] - b[i]), vs[i], TN) for i in n]
    s_new = [jnp.exp(b_end_col[i]) * s0s[i] + upd[i] for i in n]
    sub = min(16, L)
    t_idx = lax.broadcasted_iota(jnp.int32, (sub, 1), 0)
    outs = [[] for _ in n]
    for blk in range(L // sub):
        r0 = blk * sub
        acc = [o[i][r0:r0 + sub] for i in n]
        if blk > 0:
            a = [_dot_bf(qs[i][r0:r0 + sub] * jnp.exp(b[i][r0:r0 + sub] - b[i][r0 - 1:r0, :]),
                         ks[i][0:r0] * jnp.exp(b[i][r0 - 1:r0, :] - b[i][0:r0]), NT) for i in n]
            acc = [acc[i] + _dot_bf(a[i], vs[i][0:r0]) for i in n]
        prod = [jnp.concatenate(
            [qs[i][r0:r0 + sub] * ks[i][r0 + s:r0 + s + 1, :] *
             jnp.exp(jnp.where(t_idx >= s, b[i][r0:r0 + sub] - b[i][r0 + s:r0 + s + 1, :], -jnp.inf))
             for s in range(sub)], axis=0).astype(BF16) for i in n]
        score = [jnp.dot(prod[i], ones_k, preferred_element_type=F32) for i in n]
        for i in n:
            t = acc[i]
            for s in range(sub):
                t = t + score[i][s * sub:(s + 1) * sub, :] * vs[i][r0 + s:r0 + s + 1, :]
            outs[i].append(t)
    return [jnp.concatenate(x, axis=0) if len(x) > 1 else x[0] for x in outs], s_new


def _chunk_or_zero(chunk_fn, o_idx, *refs, chunks, **kw):
    c = pl.program_id(1)
    o_ref = refs[o_idx]

    @pl.when(c < chunks)
    def _():
        chunk_fn(*refs, chunks=chunks, **kw)

    @pl.when(c >= chunks)
    def _():
        o_ref[...] = jnp.zeros_like(o_ref)


def _even_kernel(*refs, L, nvalid, chunks, has_s0, G):
    if has_s0:
        z_ref, lb_ref, ng_ref, cos_ref, sin_ref, sh0_ref, sr0_ref, o_ref, sho_ref, sro_ref, sh_s, sr_s = refs
    else:
        z_ref, lb_ref, ng_ref, cos_ref, sin_ref, o_ref, sho_ref, sro_ref, sh_s, sr_s = refs
    c = pl.program_id(1)

    @pl.when(c == 0)
    def _():
        if has_s0:
            for g in range(G):
                sh_s[g * H_A:(g + 1) * H_A] = sh0_ref[g]
                sr_s[g * H_B:(g + 1) * H_B] = sr0_ref[g]
        else:
            sh_s[...] = jnp.zeros_like(sh_s)
            sr_s[...] = jnp.zeros_like(sr_s)

    lead = jnp.where(c == 0, P_LEAD, 0) if chunks > 1 else 0
    r, valid = _row_info(L, lead, nvalid)
    cnt_col = jnp.clip(r + 1 - lead, 0, nvalid).astype(F32)
    rr = lax.broadcasted_iota(jnp.int32, (1, L), 1)
    cnt_row = jnp.clip(rr + 1 - lead, 0, nvalid).astype(F32)

    off_q, off_f, off_i, off_g = 0, H_A * DK_A, 2 * H_A * DK_A, 2 * H_A * DK_A + H_A * DV_A
    base_b = 2 * H_A * DK_A + 2 * H_A * DV_A
    off_bq, off_bk, off_bv = base_b, base_b + H_B * DK_B, base_b + 2 * H_B * DK_B
    off_bg = off_bv + H_B * DV_B
    ng = ng_ref[...]
    zv, ov = _seq_views(z_ref, G), _seq_views(o_ref, G)
    ia = [(g, h) for g in range(G) for h in range(H_A)]
    ib = [(g, h) for g in range(G) for h in range(H_B)]
    na, nb_ = range(len(ia)), range(len(ib))
    af = [zv[g][:, off_f + h * DK_A: off_f + (h + 1) * DK_A] for g, h in ia]
    lbs = [lb_ref[:, h * DK_A:(h + 1) * DK_A] for g, h in ia]
    logf = [jnp.where(valid, jnp.log(lbs[i] + (1.0 - lbs[i]) * _sigmoid(af[i])), 0.0) for i in na]
    ka = [jnp.where(valid, (1.0 - lbs[i]) * _sigmoid(-af[i]), 0.0) for i in na]
    qa = [_silu(zv[g][:, off_q + h * DK_A: off_q + (h + 1) * DK_A]) for g, h in ia]
    va = [zv[g][:, off_i + h * DV_A: off_i + (h + 1) * DV_A] for g, h in ia]
    cosv, sinv = cos_ref[...], sin_ref[...]
    bq = [zv[g][:, off_bq + h * DK_B: off_bq + (h + 1) * DK_B] for g, h in ib]
    bk = [zv[g][:, off_bk + h * DK_B: off_bk + (h + 1) * DK_B] for g, h in ib]
    vb = [zv[g][:, off_bv + h * DV_B: off_bv + (h + 1) * DV_B] for g, h in ib]
    qb = [bq[i] * cosv + pltpu.roll(bq[i], DK_B // 2, 1) * sinv for i in nb_]
    kb = [jnp.where(valid, (bk[i] * cosv + pltpu.roll(bk[i], DK_B // 2, 1) * sinv) * (DK_B ** -0.5), 0.0) for i in nb_]
    lg = [math.log(1.0 - 2.0 ** (-5.0 - h)) for g, h in ib]

    oa, sa_new = _hgrn_heads(qa, ka, va, logf, [sh_s[i] for i in na], L)
    causal = _tril(L)
    cnt_end = cnt_col[L - 1:L, :]
    scores = [_dot_bf(qb[i], kb[i], NT) * jnp.where(causal, jnp.exp((cnt_col - cnt_row) * lg[i]), 0.0) for i in nb_]
    inter = [_dot_bf(qb[i] * jnp.exp(cnt_col * lg[i]), sr_s[i]) for i in nb_]
    ob = [_dot_bf(scores[i], vb[i]) + inter[i] for i in nb_]
    updb = [_dot_bf(kb[i] * jnp.exp((cnt_end - cnt_col) * lg[i]), vb[i], TN) for i in nb_]
    for i, (g, h) in enumerate(ia):
        sh_s[i] = sa_new[i]
        o = oa[i] * lax.rsqrt(jnp.mean(oa[i] * oa[i], -1, keepdims=True) + RMS_EPS) * ng
        ov[g][:, h * DV_A:(h + 1) * DV_A] = o * _silu(zv[g][:, off_g + h * DV_A: off_g + (h + 1) * DV_A])
    for i, (g, h) in enumerate(ib):
        sr_s[i] = jnp.exp(cnt_end * lg[i]) * sr_s[i] + updb[i]
        mu = jnp.mean(ob[i], -1, keepdims=True)
        cc = ob[i] - mu
        o = cc * lax.rsqrt(jnp.mean(cc * cc, -1, keepdims=True) + LN_EPS)
        ov[g][:, H_A * DV_A + h * DV_B: H_A * DV_A + (h + 1) * DV_B] = \
            o * _silu(zv[g][:, off_bg + h * DV_B: off_bg + (h + 1) * DV_B])

    @pl.when(c == chunks - 1)
    def _():
        for g in range(G):
            sho_ref[g] = sh_s[g * H_A:(g + 1) * H_A]
            sro_ref[g] = sr_s[g * H_B:(g + 1) * H_B]


def even_mixer_prompt(z, lb, ng, cos, sin, *, chunks=P_CHUNKS):
    nb = z.shape[0]
    kern = functools.partial(_chunk_or_zero, _even_kernel, 5, L=CHUNK, nvalid=CHUNK, chunks=chunks, has_s0=False, G=nb)
    blk = lambda b, c: (0, c, 0, 0)
    whole = lambda b, c: (0, 0, 0, 0)
    rope = lambda b, c: (jnp.minimum(c, chunks - 1), 0)
    return pl.pallas_call(
        kern,
        out_shape=(jax.ShapeDtypeStruct(z.shape[:3] + (EVEN_MIX,), F32),
                   jax.ShapeDtypeStruct((nb, H_A, DK_A, DV_A), F32),
                   jax.ShapeDtypeStruct((nb, H_B, DK_B, DV_B), F32)),
        grid=(1, z.shape[1]),
        in_specs=[pl.BlockSpec((nb, 1, CHUNK, EVEN_IN), blk),
                  pl.BlockSpec((1, H_A * DK_A), lambda b, c: (0, 0)),
                  pl.BlockSpec((1, DV_A), lambda b, c: (0, 0)),
                  pl.BlockSpec((CHUNK, DK_B), rope),
                  pl.BlockSpec((CHUNK, DK_B), rope)],
        out_specs=(pl.BlockSpec((nb, 1, CHUNK, EVEN_MIX), blk),
                   pl.BlockSpec((nb, H_A, DK_A, DV_A), whole),
                   pl.BlockSpec((nb, H_B, DK_B, DV_B), whole)),
        scratch_shapes=[pltpu.VMEM((nb * H_A, DK_A, DV_A), F32), pltpu.VMEM((nb * H_B, DK_B, DV_B), F32)],
        compiler_params=pltpu.CompilerParams(dimension_semantics=("arbitrary", "arbitrary"),
                                             vmem_limit_bytes=VMEM_LIMIT),
        name="even_mixer_prompt",
    )(z, lb, ng, cos, sin)


def even_mixer_sample(z, lb, ng, cos, sin, s_hgrn, s_ret):
    nb = z.shape[0]
    G = SAMPLE_GROUP
    kern = functools.partial(_even_kernel, L=S_LEN, nvalid=DEC_SEQ, chunks=1, has_s0=True, G=G)
    return pl.pallas_call(
        kern,
        out_shape=(jax.ShapeDtypeStruct((nb, S_LEN, EVEN_MIX), F32),
                   jax.ShapeDtypeStruct((nb, H_A, DK_A, DV_A), F32),
                   jax.ShapeDtypeStruct((nb, H_B, DK_B, DV_B), F32)),
        grid=(nb // G, 1),
        in_specs=[pl.BlockSpec((G, S_LEN, EVEN_IN), lambda b, c: (b, 0, 0)),
                  pl.BlockSpec((1, H_A * DK_A), lambda b, c: (0, 0)),
                  pl.BlockSpec((1, DV_A), lambda b, c: (0, 0)),
                  pl.BlockSpec((S_LEN, DK_B), lambda b, c: (0, 0)),
                  pl.BlockSpec((S_LEN, DK_B), lambda b, c: (0, 0)),
                  pl.BlockSpec((G, H_A, DK_A, DV_A), lambda b, c: (b, 0, 0, 0)),
                  pl.BlockSpec((G, H_B, DK_B, DV_B), lambda b, c: (b, 0, 0, 0))],
        out_specs=(pl.BlockSpec((G, S_LEN, EVEN_MIX), lambda b, c: (b, 0, 0)),
                   pl.BlockSpec((G, H_A, DK_A, DV_A), lambda b, c: (b, 0, 0, 0)),
                   pl.BlockSpec((G, H_B, DK_B, DV_B), lambda b, c: (b, 0, 0, 0))),
        scratch_shapes=[pltpu.VMEM((G * H_A, DK_A, DV_A), F32), pltpu.VMEM((G * H_B, DK_B, DV_B), F32)],
        compiler_params=pltpu.CompilerParams(dimension_semantics=("arbitrary", "arbitrary"),
                                             vmem_limit_bytes=VMEM_LIMIT),
        name="even_mixer_sample",
    )(z, lb, ng, cos, sin, s_hgrn, s_ret)


def _rope_tables(pos):
    half = DK_B // 2
    inv = ROPE_BASE ** (-jnp.arange(half, dtype=F32) / half)
    ang = pos.astype(F32)[:, None] * inv
    cos, sin = jnp.cos(ang), jnp.sin(ang)
    return jnp.concatenate([cos, cos], -1), jnp.concatenate([-sin, sin], -1)


ODD_PAD = 6144
O_Z, O_X, O_R, O_K, O_V, O_T = 0, DI_C, DI_C + CONV_DIM, DI_C + CONV_DIM + DI_D, DI_C + CONV_DIM + 2 * DI_D, \
    DI_C + CONV_DIM + 3 * DI_D
T_W = 384
T_DT = R_W + R_A + R_G
SHIFT_W = 3 * DI_D + T_W
HALF = 64


def _softplus(x):
    return jnp.maximum(x, 0.0) + jnp.log1p(jnp.exp(-jnp.abs(x)))


def _lane_lt(n, width):
    return lax.broadcasted_iota(jnp.int32, (1, width), 1) < n


def _pair_ones():
    r = lax.broadcasted_iota(jnp.int32, (2 * HALF, 2 * HALF), 0) < HALF
    c = lax.broadcasted_iota(jnp.int32, (2 * HALF, 2 * HALF), 1) < HALF
    return r == c


def _stack_pair(x, m_a):
    return jnp.concatenate([jnp.where(m_a, x, 0.0), jnp.where(m_a, 0.0, x)], axis=0)


def _rwkv_pairs(rs, ks_, vs_, als, bes, logws, Gs, bds, L):
    n = range(len(rs))
    m_a = _lane_lt(HALF, 2 * HALF)
    m_l = _lane_lt(L, 2 * L)
    ri = lax.broadcasted_iota(jnp.int32, (2 * L, 2 * L), 0)
    ci = lax.broadcasted_iota(jnp.int32, (2 * L, 2 * L), 1)
    cm = jnp.where(ci >= L, ci - L, ci)
    keep = cm < jnp.where(ri >= L, ri - L + 1, ri)
    eye = (ri == ci).astype(F32)
    blk4 = (ri // 4) == (ci // 4)
    pair_blk = _pair_ones()

    e_inv = [jnp.exp(-Gs[i]) for i in n]
    lhs = [jnp.concatenate([als[i] * jnp.exp(Gs[i] - logws[i]), rs[i] * jnp.exp(Gs[i])], axis=0).astype(BF16) for i in n]
    ks = [_stack_pair(ks_[i] * e_inv[i], m_a).astype(BF16) for i in n]
    bs = [_stack_pair(bes[i] * e_inv[i], m_a).astype(BF16) for i in n]
    vs = [_stack_pair(vs_[i], m_a).astype(BF16) for i in n]
    a_k = [jnp.where(keep, _dot_bf(lhs[i], ks[i], NT), 0.0) for i in n]
    a_b = [jnp.where(keep, _dot_bf(lhs[i], bs[i], NT), 0.0) for i in n]
    sb = [_dot_bf(lhs[i], bds[i], NT) for i in n]
    base = [sb[i] + _dot_bf(a_k[i], vs[i]) for i in n]
    n_bd = [jnp.concatenate([jnp.where(m_l, a_b[i][0:L], 0.0), jnp.where(m_l, 0.0, a_b[i][0:L])], axis=0) for i in n]
    n4 = [jnp.where(blk4, n_bd[i], 0.0) for i in n]
    n4sq = [_dot_bf(n4[i], n4[i]) for i in n]
    t_inv = [eye - n4[i] for i in n]
    t_inv = [t_inv[i] + _dot_bf(t_inv[i], n4sq[i]) for i in n]
    blk = 4
    while blk < L:
        off = ((ri // (2 * blk)) == (ci // (2 * blk))) & ((ri % (2 * blk)) >= blk) & ((ci % (2 * blk)) < blk)
        ct = [_dot_bf(jnp.where(off, n_bd[i], 0.0), t_inv[i]) for i in n]
        t_inv = [t_inv[i] - _dot_bf(t_inv[i], ct[i]) for i in n]
        blk *= 2
    u = [_dot_bf(t_inv[i][0:L, :] + t_inv[i][L:2 * L, :], _stack_pair(base[i][0:L], m_a)) for i in n]
    y = [base[i][L:2 * L] - _dot_bf(a_b[i][L:2 * L], _stack_pair(u[i], m_a)) for i in n]
    gl = [Gs[i][L - 1:L, :] for i in n]
    dec = [jnp.exp(gl[i] - Gs[i]) for i in n]
    upd = [_dot_bf(jnp.concatenate([vs_[i], u[i]], axis=0),
                   jnp.concatenate([ks_[i] * dec[i], -(bes[i] * dec[i])], axis=0), TN) for i in n]
    bd_new = [bds[i] * jnp.exp(gl[i]) + jnp.where(pair_blk, upd[i], 0.0) for i in n]
    return y, bd_new


def _ssd_pair(xdt, cg, bg, cb, bcs, brows, sp, L):
    m_a = _lane_lt(HALF, 2 * HALF)
    causal = _tril(L)
    out = jnp.zeros((L, 2 * HALF), F32)
    ends = [bc[L - 1:L, :] for bc in bcs]
    sp_new = sp * jnp.where(m_a, jnp.exp(ends[0]), jnp.exp(ends[1]))
    for x in range(2):
        keep = m_a if x == 0 else jnp.logical_not(m_a)
        seg = jnp.where(causal, jnp.exp(bcs[x] - brows[x]), 0.0)
        xm = jnp.where(keep, xdt, 0.0)
        out = out + _dot_bf(cb * seg, xm) + _dot_bf(cg * jnp.exp(bcs[x]), jnp.where(keep, sp, 0.0))
        sp_new = sp_new + _dot_bf(bg * jnp.exp(ends[x] - bcs[x]), xm, TN)
    return out, sp_new


def _odd_kernel(*refs, L, nvalid, chunks, has_s0, G):
    if has_s0:
        (z_ref, vec_ref, cp_ref, tp_ref, w2_ref, a2_ref, g2_ref, ssm0_ref, wkv0_ref, conv0_ref, shift0_ref,
         o_ref, ssmo_ref, wkvo_ref, convo_ref, shifto_ref, ssm_s, wkv_s, conv_c, shift_c) = refs
    else:
        (z_ref, vec_ref, cp_ref, tp_ref, w2_ref, a2_ref, g2_ref,
         o_ref, ssmo_ref, wkvo_ref, convo_ref, shifto_ref, ssm_s, wkv_s, conv_c, shift_c) = refs
    c = pl.program_id(1)
    npair = H_C // 2

    @pl.when(c == 0)
    def _():
        if has_s0:
            zpad = jnp.zeros((HALF, HALF), F32)
            for g in range(G):
                for p in range(npair):
                    ssm_s[g * npair + p] = jnp.concatenate([ssm0_ref[g, 2 * p], ssm0_ref[g, 2 * p + 1]], axis=1)
                    wkv_s[g * npair + p] = jnp.concatenate(
                        [jnp.concatenate([wkv0_ref[g, 2 * p], zpad], axis=1),
                         jnp.concatenate([zpad, wkv0_ref[g, 2 * p + 1]], axis=1)], axis=0)
                conv_c[g * 8:(g + 1) * 8, :] = conv0_ref[g]
                shift_c[g * 8:(g + 1) * 8, :] = shift0_ref[g]
        else:
            ssm_s[...] = jnp.zeros_like(ssm_s)
            wkv_s[...] = jnp.zeros_like(wkv_s)
            conv_c[...] = jnp.zeros_like(conv_c)
            shift_c[...] = jnp.zeros_like(shift_c)

    lead = jnp.where(c == 0, P_LEAD, 0) if chunks > 1 else 0
    r_idx, valid = _row_info(L, lead, nvalid)
    m_a = _lane_lt(HALF, 2 * HALF)
    tril = _tril(L).astype(F32)
    triu = (lax.broadcasted_iota(jnp.int32, (L, L), 0) <= lax.broadcasted_iota(jnp.int32, (L, L), 1)).astype(F32)
    ones_blk = _pair_ones().astype(F32)
    r8 = lax.broadcasted_iota(jnp.int32, (8, 1), 0)
    zv, ov = _seq_views(z_ref, G), _seq_views(o_ref, G)
    pairs = range(npair)
    sls = [slice(p * 2 * HALF, (p + 1) * 2 * HALF) for p in pairs]
    w2_hi, w2_lo = _parts(w2_ref[...], 2)

    def shift_mix(p_raw, carry_row, mu):
        pv = jnp.where(valid, p_raw, 0.0)
        prev = jnp.where(r_idx == 0, carry_row, pltpu.roll(pv, 1, 0))
        return pv, pv + (prev - pv) * mu

    rw = []
    for g in range(G):
        zg, og = zv[g], ov[g]
        cs = slice(g * 8, (g + 1) * 8)
        xbc = jnp.where(valid, zg[:, O_X:O_X + CONV_DIM], 0.0)
        c8 = conv_c[cs, :]
        conv = cp_ref[CONV_W:CONV_W + 1, :] + xbc * cp_ref[CONV_W - 1:CONV_W, :]
        for j in range(1, CONV_W):
            rolled = pltpu.roll(xbc, j, 0)
            head = jnp.where(r8 < j, pltpu.roll(c8, j, 0), rolled[0:8])
            sh = jnp.concatenate([head, rolled[8:]], axis=0) if L > 8 else head
            conv = conv + sh * cp_ref[CONV_W - 1 - j:CONV_W - j, :]
        conv_c[cs, :] = xbc[L - 8:L]
        act = _silu(conv)
        xc, bm, cm = act[:, :DI_C], act[:, DI_C:DI_C + G_C * N_C], act[:, DI_C + G_C * N_C:]

        t_raw = zg[:, O_T:O_T + T_W]
        dt = jnp.where(valid, _softplus(t_raw + tp_ref[1:2, :])[:, T_DT:T_DT + H_C], 0.0)
        logf = dt * tp_ref[2:3, T_DT:T_DT + H_C]
        bc_all = _dot_sel(logf, tril, sel_first=True)
        brow_all = _dot_sel(logf, triu, TN)
        per_g = npair // G_C
        for grp in range(G_C):
            cg = cm[:, grp * N_C:(grp + 1) * N_C]
            bg = bm[:, grp * N_C:(grp + 1) * N_C]
            cb = _dot_bf(cg, bg, NT)
            ys, ss = [], jnp.zeros((L, 1), F32)
            for pp in range(per_g):
                p = grp * per_g + pp
                sl = sls[p]
                h_a, h_b = 2 * p, 2 * p + 1
                xcp = xc[:, sl]
                xdt = xcp * jnp.where(m_a, dt[:, h_a:h_a + 1], dt[:, h_b:h_b + 1])
                o, sp_new = _ssd_pair(xdt, cg, bg, cb, [bc_all[:, h_a:h_a + 1], bc_all[:, h_b:h_b + 1]],
                                      [brow_all[h_a:h_a + 1, :], brow_all[h_b:h_b + 1, :]], ssm_s[g * npair + p], L)
                ssm_s[g * npair + p] = sp_new
                y = (o + xcp * vec_ref[0:1, sl]) * _silu(zg[:, O_Z + p * 2 * HALF:O_Z + (p + 1) * 2 * HALF])
                ys.append(y)
                ss = ss + jnp.sum(y * y, axis=-1, keepdims=True)
            scale = lax.rsqrt(ss / (DI_C // G_C) + RMS_EPS)
            for pp in range(per_g):
                sl = sls[grp * per_g + pp]
                og[:, sl] = ys[pp] * scale * vec_ref[1:2, sl]

        r_raw, r = shift_mix(zg[:, O_R:O_R + DI_D], shift_c[g * 8 + 7:g * 8 + 8, 0:DI_D], vec_ref[2:3, :])
        k_raw, k = shift_mix(zg[:, O_K:O_K + DI_D], shift_c[g * 8 + 7:g * 8 + 8, DI_D:2 * DI_D], vec_ref[3:4, :])
        v_raw, v = shift_mix(zg[:, O_V:O_V + DI_D], shift_c[g * 8 + 7:g * 8 + 8, 2 * DI_D:3 * DI_D], vec_ref[4:5, :])
        t_rawm, tm = shift_mix(t_raw, shift_c[g * 8 + 7:g * 8 + 8, 3 * DI_D:SHIFT_W], tp_ref[0:1, :])
        shift_c[cs, 0:DI_D] = r_raw[L - 8:L]
        shift_c[cs, DI_D:2 * DI_D] = k_raw[L - 8:L]
        shift_c[cs, 2 * DI_D:3 * DI_D] = v_raw[L - 8:L]
        shift_c[cs, 3 * DI_D:SHIFT_W] = t_rawm[L - 8:L]

        th_hi, th_lo = _parts(jnp.tanh(tm), 2)
        w_pre = vec_ref[5:6, :] + (jnp.dot(th_hi, w2_hi, preferred_element_type=F32) +
                                   jnp.dot(th_lo, w2_hi, preferred_element_type=F32) +
                                   jnp.dot(th_hi, w2_lo, preferred_element_type=F32))
        logw = jnp.where(valid, -jnp.exp(-_softplus(-w_pre) - 0.5), 0.0)
        g_cum = _dot_sel(logw, tril, sel_first=True)
        a = _sigmoid(vec_ref[6:7, :] + _dot_bf(tm, a2_ref[...]))
        gate = _dot_bf(_sigmoid(tm), g2_ref[...])
        kkr = k * vec_ref[7:8, :]
        k2 = k * (1.0 + (a - 1.0) * vec_ref[8:9, :])
        rw.append(dict(r=r, v=v, a=a, gate=gate, kkr=kkr, k2=k2, rk=r * k2 * vec_ref[9:10, :], logw=logw, g_cum=g_cum))

    gp = [(g, p) for g in range(G) for p in pairs]
    n = range(len(gp))
    nrm = [jnp.sqrt(_dot_sel(rw[g]["kkr"][:, sls[p]] * rw[g]["kkr"][:, sls[p]], ones_blk, n=2)) for g, p in gp]
    al = [jnp.where(valid, rw[g]["kkr"][:, sls[p]] / jnp.maximum(nrm[i], 1e-12), 0.0) for i, (g, p) in enumerate(gp)]
    be = [al[i] * rw[g]["a"][:, sls[p]] for i, (g, p) in enumerate(gp)]
    ys, bd_new = _rwkv_pairs([rw[g]["r"][:, sls[p]] for g, p in gp],
                             [jnp.where(valid, rw[g]["k2"][:, sls[p]], 0.0) for g, p in gp],
                             [rw[g]["v"][:, sls[p]] for g, p in gp], al, be,
                             [rw[g]["logw"][:, sls[p]] for g, p in gp], [rw[g]["g_cum"][:, sls[p]] for g, p in gp],
                             [wkv_s[i] for i in n], L)
    for i in n:
        wkv_s[i] = bd_new[i]
    mu = [_dot_sel(ys[i], ones_blk, n=2) * (1.0 / P_D) for i in n]
    yc = [ys[i] - mu[i] for i in n]
    var = [_dot_sel(yc[i] * yc[i], ones_blk, n=2) * (1.0 / P_D) for i in n]
    bonus = [_dot_sel(rw[g]["rk"][:, sls[p]], ones_blk, n=2) for g, p in gp]
    for i, (g, p) in enumerate(gp):
        sl = sls[p]
        yn = yc[i] * lax.rsqrt(var[i] + RWKV_GN_EPS) * vec_ref[10:11, sl] + vec_ref[11:12, sl]
        ov[g][:, DI_C + p * 2 * HALF:DI_C + (p + 1) * 2 * HALF] = (yn + bonus[i] * rw[g]["v"][:, sl]) * rw[g]["gate"][:, sl]

    @pl.when(c == chunks - 1)
    def _():
        for g in range(G):
            for p in range(npair):
                sp = ssm_s[g * npair + p]
                ssmo_ref[g, 2 * p] = sp[:, 0:HALF]
                ssmo_ref[g, 2 * p + 1] = sp[:, HALF:2 * HALF]
                bd = wkv_s[g * npair + p]
                wkvo_ref[g, 2 * p] = bd[0:HALF, 0:HALF]
                wkvo_ref[g, 2 * p + 1] = bd[HALF:2 * HALF, HALF:2 * HALF]
            convo_ref[g] = conv_c[g * 8:(g + 1) * 8, :]
            shifto_ref[g] = shift_c[g * 8:(g + 1) * 8, :]


def _odd_call(z, params, states, *, L, nvalid, nb, chunks, name, G):
    has_s0 = states is not None
    kern = functools.partial(_chunk_or_zero, _odd_kernel, 11 if has_s0 else 7, L=L, nvalid=nvalid, chunks=chunks,
                             has_s0=has_s0, G=G)
    npair = H_C // 2
    steps = z.shape[1] if z.ndim == 4 else chunks
    if z.ndim == 4:
        z_spec = pl.BlockSpec((G, 1, L, ODD_PAD), lambda b, c: (0, c, 0, 0))
        o_spec = pl.BlockSpec((G, 1, L, ODD_MIX), lambda b, c: (0, c, 0, 0))
        o_shape = jax.ShapeDtypeStruct(z.shape[:3] + (ODD_MIX,), F32)
    else:
        z_spec = pl.BlockSpec((G, L, ODD_PAD), lambda b, c: (b, 0, 0))
        o_spec = pl.BlockSpec((G, L, ODD_MIX), lambda b, c: (b, 0, 0))
        o_shape = jax.ShapeDtypeStruct((nb, L, ODD_MIX), F32)
    const2 = lambda b, c: (0, 0)
    per_b = lambda b, c: (b, 0, 0, 0)
    per_b3 = lambda b, c: (b, 0, 0)
    in_specs = [z_spec] + [pl.BlockSpec(p.shape, const2) for p in params]
    args = [z] + list(params)
    st_specs = [pl.BlockSpec((G, H_C, N_C, P_C), per_b), pl.BlockSpec((G, H_D, P_D, P_D), per_b),
                pl.BlockSpec((G, 8, CONV_DIM), per_b3), pl.BlockSpec((G, 8, SHIFT_W), per_b3)]
    if has_s0:
        in_specs += st_specs
        args += list(states)
    return pl.pallas_call(
        kern,
        out_shape=(o_shape,
                   jax.ShapeDtypeStruct((nb, H_C, N_C, P_C), F32), jax.ShapeDtypeStruct((nb, H_D, P_D, P_D), F32),
                   jax.ShapeDtypeStruct((nb, 8, CONV_DIM), F32), jax.ShapeDtypeStruct((nb, 8, SHIFT_W), F32)),
        grid=(nb // G, steps),
        in_specs=in_specs,
        out_specs=tuple([o_spec] + st_specs),
        scratch_shapes=[pltpu.VMEM((G * npair, N_C, 2 * HALF), F32), pltpu.VMEM((G * npair, 2 * HALF, 2 * HALF), F32),
                        pltpu.VMEM((G * 8, CONV_DIM), F32), pltpu.VMEM((G * 8, SHIFT_W), F32)],
        compiler_params=pltpu.CompilerParams(dimension_semantics=("arbitrary", "arbitrary"),
                                             vmem_limit_bytes=VMEM_LIMIT),
        name=name,
    )(*args)


def odd_mixer_prompt(z, params, *, chunks=P_CHUNKS):
    return _odd_call(z, params, None, L=CHUNK, nvalid=CHUNK, nb=z.shape[0], chunks=chunks,
                     name="odd_mixer_prompt", G=z.shape[0])


def odd_mixer_sample(z, params, states):
    return _odd_call(z, params, states, L=S_LEN, nvalid=DEC_SEQ, nb=z.shape[0], chunks=1,
                     name="odd_mixer_sample", G=ODD_SAMPLE_GROUP)


def _odd_params(odd_w_in, conv_w, conv_b, dt_bias, a_log, d_skip, ssm_norm_g, shift_mu, rwkv_w0, rwkv_w2, rwkv_a0,
                rwkv_a2, rwkv_g2, rwkv_k_k, rwkv_k_a, rwkv_r_k, lnx_g, lnx_b):
    o_dt = DI_C + CONV_DIM
    o_rw = o_dt + H_C
    w = jnp.concatenate([odd_w_in[:, :o_dt], odd_w_in[:, o_rw:], odd_w_in[:, o_dt:o_rw],
                         jnp.zeros((D_MODEL, ODD_PAD - ODD_IN), odd_w_in.dtype)], axis=1)
    vec = jnp.stack([jnp.repeat(d_skip, P_C), ssm_norm_g, shift_mu[:DI_D], shift_mu[DI_D:2 * DI_D],
                     shift_mu[2 * DI_D:3 * DI_D], rwkv_w0, rwkv_a0, rwkv_k_k, rwkv_k_a, rwkv_r_k.reshape(-1),
                     lnx_g, lnx_b] + [jnp.zeros((DI_D,), F32)] * 4)
    cpack = jnp.concatenate([conv_w, conv_b[None], jnp.zeros((3, CONV_DIM), F32)], axis=0)
    zt = jnp.zeros((T_W,), F32)
    tpack = jnp.stack([zt.at[:T_DT].set(shift_mu[3 * DI_D:]), zt.at[T_DT:T_DT + H_C].set(dt_bias),
                       zt.at[T_DT:T_DT + H_C].set(-jnp.exp(a_log.astype(F32)))] + [zt] * 5)
    zw = jnp.zeros((T_W, DI_D), F32)
    w2p = zw.at[:R_W].set(rwkv_w2)
    a2p = zw.at[R_W:R_W + R_A].set(rwkv_a2).astype(BF16)
    g2p = zw.at[R_W + R_A:T_DT].set(rwkv_g2).astype(BF16)
    return w.astype(BF16), (vec, cpack, tpack, w2p, a2p, g2p)


def _top16_desc(cur):
    vals = []
    for _ in range(PEER_TOPK):
        m = jnp.max(cur, axis=0, keepdims=True)
        vals.append(m)
        cur = jnp.where(cur == m, -jnp.inf, cur)
    return vals


PEER_CAND = 112


def _peer_kernel(x_ref, xb_ref, wq_ref, sk_ref, u_ref, v_ref, g_ref, b_ref, o_ref, ob_ref,
                 q_s, s1_s, s2_s, e1_s, e2_s, tau_s, cand_s, w_s, h_s, p_s, pn_s, *, tm, te):
    s = pl.program_id(1)
    ns = pl.num_programs(1)
    nk = PEER_KEYS
    neg = -jnp.inf

    @pl.when(s == 0)
    def _route():
        q_s[...] = lax.dot_general(wq_ref[...], xb_ref[...], NT, preferred_element_type=F32).astype(BF16)
        a_idx = lax.broadcasted_iota(jnp.int32, (PEER_TOPK, 1), 0)

        def head(h, carry):
            scs = []
            for c in range(2):
                row0 = pl.multiple_of((2 * h + c) * nk, nk)
                sc = jnp.dot(sk_ref[2 * h + c], q_s[pl.ds(row0, nk), :], preferred_element_type=F32)
                if c == 0:
                    s1_s[h] = sc
                else:
                    s2_s[h] = sc
                scs.append(sc)
            half = tm // 2
            for l0 in range(0, tm, half):
                ln = slice(l0, l0 + half)
                t1 = _top16_desc(scs[0][:, ln])
                t2 = _top16_desc(scs[1][:, ln])
                t1_all = jnp.concatenate(t1, axis=0)
                t2_all = jnp.concatenate(t2, axis=0)
                for b in range(4):
                    cand_s[b * PEER_TOPK:(b + 1) * PEER_TOPK, ln] = jnp.where(a_idx < PEER_TOPK // (b + 1), t1_all + t2[b], neg)
                for a in range(3):
                    ok = (a_idx >= 4) & (a_idx < PEER_TOPK // (a + 1))
                    cand_s[(4 + a) * PEER_TOPK:(5 + a) * PEER_TOPK, ln] = jnp.where(ok, t1[a] + t2_all, neg)
                best = _top16_desc(cand_s[:, ln])
                mx = t1[0] + t2[0]
                z = jnp.zeros_like(mx)
                for m in best:
                    z = z + jnp.exp(m - mx)
                tau_s[pl.ds(h, 1), ln] = best[-1]
                e1_s[h, :, ln] = jnp.exp(scs[0][:, ln] - t1[0])
                e2_s[h, :, ln] = jnp.exp(scs[1][:, ln] - t2[0]) / z
            return carry

        lax.fori_loop(0, PEER_HEADS, head, 0)
        o_ref[...] = jnp.zeros_like(o_ref)
        p_s[...] = jnp.zeros_like(p_s)

    blk = jnp.minimum(s, ns - 2)
    nj = te // nk
    d = o_ref.shape[1]
    kp, cp = d // 4, d // 8

    def mm1(k):
        part = lax.dot_general(u_ref[:, k * kp:(k + 1) * kp], xb_ref[:, k * kp:(k + 1) * kp], NT,
                               preferred_element_type=F32)
        if k == 0:
            h_s[...] = part
        else:
            h_s[...] += part

    def wbuild(j):
        i1 = blk * nj + j
        w = jnp.zeros((nk, tm), F32)
        for h in range(PEER_HEADS):
            c = s2_s[h] + s1_s[h, pl.ds(i1, 1), :]
            w = w + jnp.where(c >= tau_s[h:h + 1, :], e2_s[h] * e1_s[h, pl.ds(i1, 1), :], 0.0)
        w_s[j * nk:(j + 1) * nk, :] = w

    def act(j):
        hj = h_s[j * nk:(j + 1) * nk, :]
        g = 0.5 * hj * (1.0 + lax.erf(hj * (2.0 ** -0.5)))
        pn_s[:, j * nk:(j + 1) * nk] = jnp.transpose(w_s[j * nk:(j + 1) * nk, :] * g).astype(BF16)

    def mm2(n):
        o_ref[:, n * cp:(n + 1) * cp] += jnp.dot(p_s[...], v_ref[:, n * cp:(n + 1) * cp], preferred_element_type=F32)

    mxu = [functools.partial(mm1, k) for k in range(d // kp)] + [functools.partial(mm2, n) for n in range(d // cp)]
    vec = [functools.partial(wbuild, j) for j in range(nj)] + [functools.partial(act, j) for j in range(nj)]
    done = 0
    for i, piece in enumerate(mxu):
        piece()
        upto = (i + 1) * len(vec) // len(mxu)
        for f in vec[done:upto]:
            f()
        done = upto
    p_s[...] = pn_s[...]

    @pl.when(s == ns - 1)
    def _fin():
        y = _ln_rows(ALPHA * x_ref[...] + o_ref[...], g_ref[...], b_ref[...])
        o_ref[...] = y
        ob_ref[...] = y.astype(BF16)


def peer_ln(x, xb, wq_t, sk, u, v, g, b, *, layer=0, tm=512, te=1024):
    m, d = x.shape
    kern = functools.partial(_peer_kernel, tm=tm, te=te)
    const = dict(pipeline_mode=pl.Buffered(1))
    ne = PEER_EXPERTS // te
    route = pltpu.VMEM((PEER_HEADS, PEER_KEYS, tm), F32)
    return pl.pallas_call(
        kern,
        out_shape=(jax.ShapeDtypeStruct((m, d), F32), jax.ShapeDtypeStruct((m, d), BF16)),
        grid=(m // tm, ne + 1),
        in_specs=[pl.BlockSpec((tm, d), lambda i, s: (i, 0), **const),
                  pl.BlockSpec((tm, d), lambda i, s: (i, 0), **const),
                  pl.BlockSpec(wq_t.shape, lambda i, s: (0, 0), **const),
                  pl.BlockSpec(sk.shape, lambda i, s: (0, 0, 0), **const),
                  pl.BlockSpec((None, te, d), lambda i, s: (layer, jnp.minimum(s, ne - 1), 0)),
                  pl.BlockSpec((None, te, d), lambda i, s: (layer, jnp.maximum(s - 1, 0), 0)),
                  pl.BlockSpec((1, d), lambda i, s: (0, 0), **const),
                  pl.BlockSpec((1, d), lambda i, s: (0, 0), **const)],
        out_specs=(pl.BlockSpec((tm, d), lambda i, s: (i, 0), **const), pl.BlockSpec((tm, d), lambda i, s: (i, 0), **const)),
        scratch_shapes=[pltpu.VMEM((PEER_HEADS * PEER_QDIM, tm), BF16), route, route, route, route,
                        pltpu.VMEM((PEER_HEADS, tm), F32), pltpu.VMEM((PEER_CAND, tm), F32),
                        pltpu.VMEM((te, tm), F32), pltpu.VMEM((te, tm), F32),
                        pltpu.VMEM((tm, te), BF16), pltpu.VMEM((tm, te), BF16)],
        compiler_params=pltpu.CompilerParams(dimension_semantics=("arbitrary", "arbitrary"),
                                             vmem_limit_bytes=PEER_VMEM_LIMIT),
        name="peer_ln",
    )(x, xb, wq_t, sk, u, v, g.reshape(1, d), b.reshape(1, d))


def _slots(a):
    return a.reshape(BATCH, SLOT_CHUNKS, CHUNK, a.shape[-1])


def _sample_rows(z4):
    zs = z4[:, P_CHUNKS:P_CHUNKS + S_CHUNKS].reshape(DEC_BATCH, DEC_SEQ, z4.shape[-1])
    return jnp.pad(zs, ((0, 0), (0, S_LEN - DEC_SEQ), (0, 0)))


def _merge_rows(buf4, sample_out):
    f = buf4.shape[-1]
    rows = sample_out[:, :DEC_SEQ].reshape(BATCH, S_CHUNKS, CHUNK, f)
    return lax.dynamic_update_slice(buf4, rows, (0, P_CHUNKS, 0, 0)).reshape(M_PAD, f)


def kernel(x_prompt, x_sample, state_hgrn, state_ret, state_ssm, state_conv, state_wkv, state_shift, meta_tokens, ln_g, ln_b, even_w_in, hgrn_lb_logits, hgrn_norm_g, even_w_out, odd_w_in, conv_w, conv_b, dt_bias, a_log, d_skip, ssm_norm_g, shift_mu, rwkv_w0, rwkv_w2, rwkv_a0, rwkv_a2, rwkv_g2, rwkv_k_k, rwkv_k_a, rwkv_r_k, lnx_g, lnx_b, odd_w_out, peer_w_query, peer_sub_keys, peer_u, peer_v):
    dt = x_prompt.dtype
    lead = jnp.concatenate([jnp.zeros((P_LEAD, D_MODEL), dt), meta_tokens.astype(dt)], axis=0)
    x = jnp.concatenate([jnp.broadcast_to(lead[None, None], (BATCH, 1, CHUNK, D_MODEL)),
                         x_prompt.reshape(BATCH, P_CHUNKS - 1, CHUNK, D_MODEL),
                         x_sample.reshape(BATCH, S_CHUNKS, CHUNK, D_MODEL),
                         jnp.zeros((BATCH, 1, CHUNK, D_MODEL), dt)], axis=1).reshape(M_PAD, D_MODEL)
    xb = x.astype(BF16)
    peer_ub, peer_vb = peer_u.astype(BF16), peer_v.astype(BF16)

    cos_p, sin_p = _rope_tables(jnp.arange(P_ROWS) - P_LEAD)
    cos_s, sin_s = _rope_tables(PAST_LEN + jnp.arange(S_LEN))
    lb_table = jnp.cumsum(jax.nn.softmax(hgrn_lb_logits.astype(F32), axis=0), axis=0)

    z = matmul(xb, even_w_in[0].astype(BF16), tm=TOK_TILE, tn=1024, name="even_in")
    lb = lb_table[0].reshape(1, -1)
    ng = hgrn_norm_g[0].reshape(1, -1)
    z = _slots(z)
    mix_p, hgrn_p, ret_p = even_mixer_prompt(z, lb, ng, cos_p, sin_p)
    mix_s, hgrn_s, ret_s = even_mixer_sample(_sample_rows(z), lb, ng, cos_s, sin_s, state_hgrn[0], state_ret[0])
    mix = _merge_rows(mix_p, mix_s)
    x, xb = proj_ln(x, mix, even_w_out[0].astype(BF16), ln_g[0, 0], ln_b[0, 0], tm=TOK_TILE, name="even_out")
    x, xb = peer_ln(x, xb, peer_w_query[0].T.astype(BF16),
                    peer_sub_keys[0].reshape(2 * PEER_HEADS, PEER_KEYS, PEER_QDIM // 2).astype(BF16),
                    peer_ub, peer_vb, ln_g[0, 1], ln_b[0, 1], layer=0)

    w_in1, params = _odd_params(odd_w_in[0], conv_w[0], conv_b[0], dt_bias[0], a_log[0], d_skip[0], ssm_norm_g[0],
                                shift_mu[0], rwkv_w0[0], rwkv_w2[0], rwkv_a0[0], rwkv_a2[0], rwkv_g2[0], rwkv_k_k[0],
                                rwkv_k_a[0], rwkv_r_k[0], lnx_g[0], lnx_b[0])
    z = _slots(matmul(xb, w_in1, tm=TOK_TILE, tn=1024, name="odd_in"))
    mix_p, ssm_p, wkv_p, conv_p, shift_p = odd_mixer_prompt(z, params)
    conv8 = jnp.pad(state_conv[0], ((0, 0), (8 - (CONV_W - 1), 0), (0, 0)))
    shift8 = jnp.pad(state_shift[0][:, None, :], ((0, 0), (7, 0), (0, SHIFT_W - SHIFT_DIM)))
    mix_s, ssm_s, wkv_s, conv_s, shift_s = odd_mixer_sample(_sample_rows(z), params,
                                                            (state_ssm[0], state_wkv[0], conv8, shift8))
    mix = _merge_rows(mix_p, mix_s)
    x, xb = proj_ln(x, mix, odd_w_out[0].astype(BF16), ln_g[1, 0], ln_b[1, 0], tm=TOK_TILE, name="odd_out")
    x, xb = peer_ln(x, xb, peer_w_query[1].T.astype(BF16),
                    peer_sub_keys[1].reshape(2 * PEER_HEADS, PEER_KEYS, PEER_QDIM // 2).astype(BF16),
                    peer_ub, peer_vb, ln_g[1, 1], ln_b[1, 1], layer=1)

    x4 = _slots(x)
    y_prompt = x4[:, 1:P_CHUNKS].reshape(BATCH, SEQ, D_MODEL)
    y_sample = x4[:, P_CHUNKS:P_CHUNKS + S_CHUNKS].reshape(DEC_BATCH, DEC_SEQ, D_MODEL)
    nc = CONV_W - 1
    return (y_prompt, y_sample, hgrn_p[None], hgrn_s[None], ret_p[None], ret_s[None], ssm_p[None], ssm_s[None],
            conv_p[None, :, 8 - nc:], conv_s[None, :, DEC_SEQ - nc:DEC_SEQ], wkv_p[None], wkv_s[None],
            shift_p[None, :, 7, :SHIFT_DIM], shift_s[None, :, DEC_SEQ - 1, :SHIFT_DIM])
```
